```python
import jax
import jax.numpy as jnp
from jax import lax
import numpy as np

D_MODEL = 1024
BATCH = 8
SEQ = 2048
DEPTH = 4
DEC_BATCH = 128
DEC_SEQ = 1
PAST_LEN = 16384
PAGE_SIZE = 128

POOL_WINDOWS = (2, 4, 8, 16)
POOL_GROUPS = len(POOL_WINDOWS)
POOL_WIDTH = D_MODEL // 4
POOL_GDIM = POOL_WIDTH // POOL_GROUPS
POOL_BUF = max(POOL_WINDOWS) - 1
MLSTM_HEADS = 4
MLSTM_WIDTH = D_MODEL // 2
MLSTM_HDIM = MLSTM_WIDTH // MLSTM_HEADS
MLSTM_CHUNK = 128
GMLP_GROUPS = 4
GMLP_WIDTH = D_MODEL // 4
GMLP_GDIM = GMLP_WIDTH // GMLP_GROUPS
GMLP_CHUNK = 128
MEM_TOKENS = 256
MEM_HEADS = 4
MEM_HDIM = D_MODEL // MEM_HEADS
D_FF = 2816
N_BRANCH = 3
N_SUBLAYERS = 4
RMS_EPS = 1e-6
LN_EPS = 1e-5
D_IN = POOL_WIDTH + 4 * MLSTM_WIDTH + 2 * MLSTM_HEADS + 2 * GMLP_WIDTH + N_BRANCH * D_MODEL

kernel_name = 'hybrid_pool_mlstm_gmlp_memory_decoder_step'


def rms_norm(x, g):
    xf = x.astype(jnp.float32)
    y = xf * lax.rsqrt(jnp.mean(xf * xf, axis=-1, keepdims=True) + RMS_EPS)
    return (y * g.astype(jnp.float32)).astype(x.dtype)


def layer_norm(x, g, b):
    xf = x.astype(jnp.float32)
    xc = xf - jnp.mean(xf, axis=-1, keepdims=True)
    y = xc * lax.rsqrt(jnp.mean(xc * xc, axis=-1, keepdims=True) + LN_EPS)
    return (y * g.astype(jnp.float32) + b.astype(jnp.float32)).astype(x.dtype)


def swiglu(x, w_up, w_down):
    gate, up = jnp.split(x @ w_up, 2, axis=-1)
    return (jax.nn.silu(gate) * up) @ w_down


def split_in(z):
    sizes = (POOL_WIDTH, MLSTM_WIDTH, MLSTM_WIDTH, MLSTM_WIDTH, MLSTM_WIDTH,
             MLSTM_HEADS, MLSTM_HEADS, GMLP_WIDTH, GMLP_WIDTH, N_BRANCH * D_MODEL)
    return jnp.split(z, np.cumsum(sizes)[:-1].tolist(), axis=-1)


def pool_mix(u, buf, pos0, w_pool, s_pool):
    B, T, _ = u.shape
    full = jnp.concatenate([buf.astype(u.dtype), u], axis=1)
    cs = jnp.cumsum(full.astype(jnp.float32), axis=1)
    cs = jnp.concatenate([jnp.zeros_like(cs[:, :1]), cs], axis=1)
    end = cs[:, POOL_BUF + 1:]
    pos = pos0 + jnp.arange(T, dtype=jnp.int32)
    means = []
    for g, w in enumerate(POOL_WINDOWS):
        ch = slice(g * POOL_GDIM, (g + 1) * POOL_GDIM)
        start = cs[:, POOL_BUF + 1 - w:POOL_BUF + 1 - w + T, ch]
        count = jnp.minimum(pos + 1, w).astype(jnp.float32)[None, :, None]
        means.append((end[..., ch] - start) / count)
    diff = jnp.concatenate(means, axis=-1) - u.astype(jnp.float32)
    y = jnp.einsum('btgc,gcd->btgd', diff.reshape(B, T, POOL_GROUPS, POOL_GDIM), w_pool.astype(jnp.float32))
    y = y.reshape(B, T, POOL_WIDTH) * s_pool.astype(jnp.float32)
    return y.astype(u.dtype), full[:, -POOL_BUF:]


def mlstm_chunk(carry, xs):
    C, n, m = carry
    q, k, v, ig, lf = xs
    L = q.shape[2]
    b = jnp.cumsum(lf, axis=-1)
    causal = jnp.tril(jnp.ones((L, L), dtype=bool))
    dmat = jnp.where(causal, b[..., :, None] - b[..., None, :] + ig[..., None, :], -jnp.inf)
    inter = b + m[..., None]
    m_t = jnp.maximum(inter, jnp.max(dmat, axis=-1))
    w_intra = jnp.exp(dmat - m_t[..., None])
    w_inter = jnp.exp(inter - m_t)
    s = jnp.einsum('bhtd,bhsd->bhts', q, k) * w_intra
    num = jnp.einsum('bhts,bhsd->bhtd', s, v) + w_inter[..., None] * jnp.einsum('bhvk,bhtk->bhtv', C, q)
    den = jnp.sum(s, axis=-1) + w_inter * jnp.einsum('bhk,bhtk->bht', n, q)
    h = num / jnp.maximum(jnp.abs(den), jnp.exp(-m_t))[..., None]
    m_new = m_t[..., -1]
    w_state = jnp.exp(b[..., -1:] - b + ig - m_new[..., None])
    decay = jnp.exp(b[..., -1] + m - m_new)
    C_new = decay[..., None, None] * C + jnp.einsum('bhs,bhsv,bhsk->bhvk', w_state, v, k)
    n_new = decay[..., None] * n + jnp.einsum('bhs,bhsk->bhk', w_state, k)
    return (C_new, n_new, m_new), h


def mlstm_mix(q, k, v, ig, lf, C0, n0, m0):
    B, H, T, d = q.shape
    L = MLSTM_CHUNK if T % MLSTM_CHUNK == 0 else T
    nc = T // L

    def to_chunks(a):
        return jnp.moveaxis(a.reshape(a.shape[:2] + (nc, L) + a.shape[3:]), 2, 0)

    xs = (to_chunks(q), to_chunks(k), to_chunks(v), to_chunks(ig), to_chunks(lf))
    (C, n, m), h = lax.scan(mlstm_chunk, (C0, n0, m0), xs)
    h = jnp.moveaxis(h, 0, 2).reshape(B, H, T, d)
    return h, C, n, m


def gmlp_mix(u, v, ln_g, ln_b, w_s, b_s):
    B, T, _ = v.shape
    vn = layer_norm(v, ln_g, ln_b)
    pad = (-T) % GMLP_CHUNK
    nc = (T + pad) // GMLP_CHUNK
    vp = jnp.pad(vn, ((0, 0), (0, pad), (0, 0))).reshape(B, nc, GMLP_CHUNK, GMLP_GROUPS, GMLP_GDIM)
    ws = jnp.where(jnp.tril(jnp.ones((GMLP_CHUNK, GMLP_CHUNK), dtype=bool)), w_s, 0)
    mixed = jnp.einsum('gts,bcsgd->bctgd', ws, vp) + b_s.T[None, None, :, :, None]
    mixed = mixed.reshape(B, nc * GMLP_CHUNK, GMLP_WIDTH)[:, :T]
    return (u * mixed).astype(u.dtype), vn


def mem_kv(mem, g_mem, w_mk, w_mv):
    B = mem.shape[0]
    mn = rms_norm(mem, g_mem)
    k = (mn @ w_mk).reshape(B, MEM_TOKENS, MEM_HEADS, MEM_HDIM)
    v = (mn @ w_mv).reshape(B, MEM_TOKENS, MEM_HEADS, MEM_HDIM)
    return k, v


def mem_attend(xn, mem_k, mem_v, w_mq, w_mo):
    B, T, _ = xn.shape
    q = (xn @ w_mq).reshape(B, T, MEM_HEADS, MEM_HDIM)
    s = jnp.einsum('bthd,bmhd->bhtm', q, mem_k.astype(q.dtype)).astype(jnp.float32) * (MEM_HDIM ** -0.5)
    p = jax.nn.softmax(s, axis=-1).astype(q.dtype)
    o = jnp.einsum('bhtm,bmhd->bthd', p, mem_v.astype(q.dtype)).reshape(B, T, D_MODEL)
    return o @ w_mo


def trunk_layer(x, mem_k, mem_v, pool_buf, C0, n0, m0, pos0,
                norm_g, w_ff_in, w_ff_out, w_in, b_igate, b_fgate, w_pool, s_pool, g_mlstm,
                gmlp_ln_g, gmlp_ln_b, w_s, b_s, w_br_pool, w_br_mlstm, w_br_gmlp, w_out, w_mq, w_mo):
    B, T, _ = x.shape
    f32 = jnp.float32
    x = x + 0.5 * rms_norm(swiglu(rms_norm(x, norm_g[0, 0]), w_ff_in[0], w_ff_out[0]), norm_g[0, 1])
    xn = rms_norm(x, norm_g[1, 0])
    z_pool, z_q, z_k, z_v, z_o, z_i, z_f, z_u, z_gv, z_gate = split_in(xn @ w_in)
    h_pool, new_pool_buf = pool_mix(z_pool, pool_buf, pos0, w_pool, s_pool)
    def heads(a):
        return a.reshape(B, T, MLSTM_HEADS, MLSTM_HDIM).transpose(0, 2, 1, 3).astype(f32)
    q = heads(z_q)
    k = heads(z_k) * (MLSTM_HDIM ** -0.5)
    v = heads(z_v)
    ig = (z_i.astype(f32) + b_igate.astype(f32)).transpose(0, 2, 1)
    lf = jax.nn.log_sigmoid(z_f.astype(f32) + b_fgate.astype(f32)).transpose(0, 2, 1)
    h, C, n, m = mlstm_mix(q, k, v, ig, lf, C0.astype(f32), n0.astype(f32), m0.astype(f32))
    h = h.transpose(0, 2, 1, 3)
    h = h * lax.rsqrt(jnp.mean(h * h, axis=-1, keepdims=True) + RMS_EPS)
    h_mlstm = (h.reshape(B, T, MLSTM_WIDTH) * g_mlstm.astype(f32) * jax.nn.sigmoid(z_o.astype(f32))).astype(x.dtype)
    h_gmlp, v_rows = gmlp_mix(jax.nn.gelu(z_u), jax.nn.gelu(z_gv), gmlp_ln_g, gmlp_ln_b, w_s, b_s)
    gates = jax.nn.sigmoid(z_gate).reshape(B, T, N_BRANCH, D_MODEL)
    merged = (gates[:, :, 0] * (h_pool @ w_br_pool)
              + gates[:, :, 1] * (h_mlstm @ w_br_mlstm)
              + gates[:, :, 2] * (h_gmlp @ w_br_gmlp))
    x = x + rms_norm(merged @ w_out, norm_g[1, 1])
    x = x + rms_norm(mem_attend(rms_norm(x, norm_g[2, 0]), mem_k, mem_v, w_mq, w_mo), norm_g[2, 1])
    x = x + 0.5 * rms_norm(swiglu(rms_norm(x, norm_g[3, 0]), w_ff_in[1], w_ff_out[1]), norm_g[3, 1])
    return x, new_pool_buf, C.astype(x.dtype), n.astype(x.dtype), m.astype(x.dtype), v_rows


def setup_inputs(seed: int = 0) -> dict:
    key = jax.random.key(seed)
    ks = jax.random.split(key, 40)
    f32 = jnp.float32

    def nrm(k, shape, scale):
        return jax.random.normal(k, shape, f32) * scale

    return {
        'x_prompt': nrm(ks[0], (BATCH, SEQ, D_MODEL), 1.0),
        'x_sample': nrm(ks[1], (DEC_BATCH, DEC_SEQ, D_MODEL), 1.0),
        'state_pool': nrm(ks[2], (DEPTH, DEC_BATCH, POOL_BUF, POOL_WIDTH), 1.0),
        'state_mlstm_C': nrm(ks[3], (DEPTH, DEC_BATCH, MLSTM_HEADS, MLSTM_HDIM, MLSTM_HDIM), 0.3),
        'state_mlstm_n': nrm(ks[4], (DEPTH, DEC_BATCH, MLSTM_HEADS, MLSTM_HDIM), 0.3),
        'state_mlstm_m': nrm(ks[5], (DEPTH, DEC_BATCH, MLSTM_HEADS), 1.0),
        'cache_mem_k': nrm(ks[6], (DEPTH, DEC_BATCH, MEM_TOKENS, MEM_HEADS, MEM_HDIM), 1.0),
        'cache_mem_v': nrm(ks[7], (DEPTH, DEC_BATCH, MEM_TOKENS, MEM_HEADS, MEM_HDIM), 1.0),
        'mem_prompt': nrm(ks[8], (BATCH, MEM_TOKENS, D_MODEL), 1.0),
        'norm_g': 1.0 + nrm(ks[9], (DEPTH, N_SUBLAYERS, 2, D_MODEL), 0.05),
        'w_ff_in': nrm(ks[10], (DEPTH, 2, D_MODEL, 2 * D_FF), D_MODEL ** -0.5),
        'w_ff_out': nrm(ks[11], (DEPTH, 2, D_FF, D_MODEL), D_FF ** -0.5),
        'w_in': nrm(ks[12], (DEPTH, D_MODEL, D_IN), D_MODEL ** -0.5),
        'b_igate': nrm(ks[13], (DEPTH, MLSTM_HEADS), 0.1),
        'b_fgate': jnp.linspace(3.0, 6.0, MLSTM_HEADS, dtype=f32) + nrm(ks[14], (DEPTH, MLSTM_HEADS), 0.1),
        'w_pool': nrm(ks[15], (DEPTH, POOL_GROUPS, POOL_GDIM, POOL_GDIM), POOL_GDIM ** -0.5),
        's_pool': 1.0 + nrm(ks[16], (DEPTH, POOL_WIDTH), 0.05),
        'g_mlstm': 1.0 + nrm(ks[17], (DEPTH, MLSTM_WIDTH), 0.05),
        'gmlp_ln_g': 1.0 + nrm(ks[18], (DEPTH, GMLP_WIDTH), 0.05),
        'gmlp_ln_b': nrm(ks[19], (DEPTH, GMLP_WIDTH), 0.02),
        'w_s': nrm(ks[20], (DEPTH, GMLP_GROUPS, GMLP_CHUNK, GMLP_CHUNK), GMLP_CHUNK ** -0.5),
        'b_s': 1.0 + nrm(ks[21], (DEPTH, GMLP_GROUPS, GMLP_CHUNK), 0.05),
        'w_br_pool': nrm(ks[22], (DEPTH, POOL_WIDTH, D_MODEL), POOL_WIDTH ** -0.5),
        'w_br_mlstm': nrm(ks[23], (DEPTH, MLSTM_WIDTH, D_MODEL), MLSTM_WIDTH ** -0.5),
        'w_br_gmlp': nrm(ks[24], (DEPTH, GMLP_WIDTH, D_MODEL), GMLP_WIDTH ** -0.5),
        'w_out': nrm(ks[25], (DEPTH, D_MODEL, D_MODEL), D_MODEL ** -0.5),
        'g_mem': 1.0 + nrm(ks[26], (DEPTH, D_MODEL), 0.05),
        'w_mq': nrm(ks[27], (DEPTH, D_MODEL, D_MODEL), D_MODEL ** -0.5),
        'w_mk': nrm(ks[28], (DEPTH, D_MODEL, D_MODEL), D_MODEL ** -0.5),
        'w_mv': nrm(ks[29], (DEPTH, D_MODEL, D_MODEL), D_MODEL ** -0.5),
        'w_mo': nrm(ks[30], (DEPTH, D_MODEL, D_MODEL), D_MODEL ** -0.5),
    }


def reference(x_prompt, x_sample, state_pool, state_mlstm_C, state_mlstm_n, state_mlstm_m,
              cache_mem_k, cache_mem_v, mem_prompt,
              norm_g, w_ff_in, w_ff_out, w_in, b_igate, b_fgate, w_pool, s_pool, g_mlstm,
              gmlp_ln_g, gmlp_ln_b, w_s, b_s, w_br_pool, w_br_mlstm, w_br_gmlp, w_out,
              g_mem, w_mq, w_mk, w_mv, w_mo):
    f32 = jnp.float32
    xp = x_prompt
    xs = x_sample
    pool_p, pool_s = [], []
    Cp, n_p, mp, Cs, n_s, ms = [], [], [], [], [], []
    mkp, mvp, vrows_s = [], [], []
    for l in range(DEPTH):
        lw = (norm_g[l], w_ff_in[l], w_ff_out[l], w_in[l], b_igate[l], b_fgate[l], w_pool[l], s_pool[l],
              g_mlstm[l], gmlp_ln_g[l], gmlp_ln_b[l], w_s[l], b_s[l], w_br_pool[l], w_br_mlstm[l],
              w_br_gmlp[l], w_out[l], w_mq[l], w_mo[l])
        mk, mv = mem_kv(mem_prompt, g_mem[l], w_mk[l], w_mv[l])
        buf0 = jnp.zeros((BATCH, POOL_BUF, POOL_WIDTH), xp.dtype)
        C0 = jnp.zeros((BATCH, MLSTM_HEADS, MLSTM_HDIM, MLSTM_HDIM), f32)
        n0 = jnp.zeros((BATCH, MLSTM_HEADS, MLSTM_HDIM), f32)
        m0 = jnp.zeros((BATCH, MLSTM_HEADS), f32)
        xp, pb, C, n, m, _ = trunk_layer(xp, mk, mv, buf0, C0, n0, m0, 0, *lw)
        pool_p.append(pb); Cp.append(C); n_p.append(n); mp.append(m); mkp.append(mk); mvp.append(mv)
        xs, pb, C, n, m, vr = trunk_layer(xs, cache_mem_k[l], cache_mem_v[l], state_pool[l],
                                          state_mlstm_C[l], state_mlstm_n[l], state_mlstm_m[l],
                                          PAST_LEN, *lw)
        pool_s.append(pb); Cs.append(C); n_s.append(n); ms.append(m); vrows_s.append(vr)
    return (xp, xs,
            jnp.stack(pool_p), jnp.stack(pool_s),
            jnp.stack(Cp), jnp.stack(n_p), jnp.stack(mp),
            jnp.stack(Cs), jnp.stack(n_s), jnp.stack(ms),
            jnp.stack(mkp), jnp.stack(mvp), jnp.stack(vrows_s))
```

```python
import functools

import jax
import jax.numpy as jnp
from jax import lax
from jax.experimental import pallas as pl
from jax.experimental.pallas import tpu as pltpu

F32 = jnp.float32
BF16 = jnp.bfloat16

D_MODEL = 1024
DEPTH = 4
PAST_LEN = 16384
POOL_WINDOWS = (2, 4, 8, 16)
POOL_WIDTH = 256
POOL_GDIM = 64
POOL_BUF = 15
HEADS = 4
MLSTM_WIDTH = 512
HDIM = 128
CHUNK = 128
GMLP_WIDTH = 256
GMLP_GROUPS = 4
GMLP_GDIM = 64
MEM_TOKENS = 256
MEM_HEADS = 4
MEM_HDIM = 256
D_FF = 2816
FF_CHUNK = 256
N_FF_CHUNKS = D_FF // FF_CHUNK
RMS_EPS = 1e-6
LN_EPS = 1e-5
OFF_Q = POOL_WIDTH
OFF_IF = OFF_Q + 4 * MLSTM_WIDTH
OFF_U = OFF_IF + 2 * HEADS
OFF_GATE = OFF_U + 2 * GMLP_WIDTH
W1_QKV = POOL_WIDTH
W1_O = W1_QKV + 3 * MLSTM_WIDTH
W1_UGV = W1_O + MLSTM_WIDTH
W1_IF = W1_UGV + 2 * GMLP_WIDTH
W1_COLS = W1_IF + 128

VMEM_LIMIT_BYTES = 56 * 1024 * 1024


def _params(*sem):
    return pltpu.CompilerParams(dimension_semantics=sem, vmem_limit_bytes=VMEM_LIMIT_BYTES)


def _resident(shape):
    nd = len(shape)
    return pl.BlockSpec(shape, lambda *_: (0,) * nd, pipeline_mode=pl.Buffered(1))


def _rms(x, g):
    return x * lax.rsqrt(jnp.mean(x * x, axis=-1, keepdims=True) + RMS_EPS) * g


def _dot(a, b):
    return jnp.dot(a, b, preferred_element_type=F32)


def _dot_nt(a, b):
    return lax.dot_general(a, b, (((1,), (1,)), ((), ())), preferred_element_type=F32)


def _dot_tn(a, b):
    return lax.dot_general(a, b, (((0,), (0,)), ((), ())), preferred_element_type=F32)


def _sigmoid(x):
    return 1.0 / (1.0 + jnp.exp(-x))


def _gelu_tanh(x):
    return 0.5 * x * (1.0 + jnp.tanh(0.7978845608028654 * (x + 0.044715 * (x * x * x))))


def _log_sigmoid(x):
    return -(jnp.maximum(-x, 0.0) + jnp.log(1.0 + jnp.exp(-jnp.abs(x))))


def _ffn_body(x_ref, g_ref, wg_ref, wu_ref, wd_ref, o_ref, acc_ref):
    x = x_ref[...]
    xn = _rms(x, g_ref[0:1, :]).astype(BF16)
    for c in range(N_FF_CHUNKS):
        gate = _dot(xn, wg_ref[c])
        up = _dot(xn, wu_ref[c])
        act = (gate * _sigmoid(gate) * up).astype(BF16)
        part = _dot(act, wd_ref[c])
        if c == 0:
            acc_ref[...] = part
        else:
            acc_ref[...] += part
    o_ref[...] = x + 0.5 * _rms(acc_ref[...], g_ref[1:2, :])


def _ffn(x, g, wg, wu, wd, tm):
    n = x.shape[0]
    return pl.pallas_call(
        _ffn_body,
        grid=(n // tm,),
        in_specs=[
            pl.BlockSpec((tm, D_MODEL), lambda i: (i, 0)),
            _resident((2, D_MODEL)),
            _resident(wg.shape),
            _resident(wu.shape),
            _resident(wd.shape),
        ],
        out_specs=pl.BlockSpec((tm, D_MODEL), lambda i: (i, 0)),
        out_shape=jax.ShapeDtypeStruct((n, D_MODEL), F32),
        scratch_shapes=[pltpu.VMEM((tm, D_MODEL), F32)],
        compiler_params=_params("arbitrary"),
        name="ffn",
    )(x, g, wg, wu, wd)


def _inproj_body(x_ref, g_ref, w_ref, zp_ref, qkv_ref, zo_ref, ugv_ref, zif_ref):
    xn = _rms(x_ref[...], g_ref[0:1, :]).astype(BF16)
    z = _dot(xn, w_ref[...])
    zp_ref[...] = z[:, :W1_QKV]
    q = z[:, W1_QKV:W1_QKV + MLSTM_WIDTH]
    k = z[:, W1_QKV + MLSTM_WIDTH:W1_QKV + 2 * MLSTM_WIDTH] * (HDIM ** -0.5)
    v = z[:, W1_QKV + 2 * MLSTM_WIDTH:W1_O]
    qkv_ref[:, 0:MLSTM_WIDTH] = q.astype(qkv_ref.dtype)
    qkv_ref[:, MLSTM_WIDTH:2 * MLSTM_WIDTH] = k.astype(qkv_ref.dtype)
    qkv_ref[:, 2 * MLSTM_WIDTH:] = v.astype(qkv_ref.dtype)
    zo_ref[...] = z[:, W1_O:W1_UGV]
    ugv_ref[...] = z[:, W1_UGV:W1_IF]
    zif_ref[...] = z[:, W1_IF:]


def _inproj(x, g, w1, tm, qkv_dtype):
    n = x.shape[0]
    row = lambda w: pl.BlockSpec((tm, w), lambda i: (i, 0))
    return pl.pallas_call(
        _inproj_body,
        grid=(n // tm,),
        in_specs=[row(D_MODEL), _resident((2, D_MODEL)), _resident(w1.shape)],
        out_specs=[row(POOL_WIDTH), row(3 * MLSTM_WIDTH), row(MLSTM_WIDTH), row(2 * GMLP_WIDTH), row(128)],
        out_shape=[
            jax.ShapeDtypeStruct((n, POOL_WIDTH), F32),
            jax.ShapeDtypeStruct((n, 3 * MLSTM_WIDTH), qkv_dtype),
            jax.ShapeDtypeStruct((n, MLSTM_WIDTH), F32),
            jax.ShapeDtypeStruct((n, 2 * GMLP_WIDTH), F32),
            jax.ShapeDtypeStruct((n, 128), F32),
        ],
        compiler_params=_params("arbitrary"),
        name="inproj",
    )(x, g, w1)


def _pool_select(sums, u, count_of):
    lane = lax.broadcasted_iota(jnp.int32, u.shape, u.ndim - 1)
    mean = None
    for gi, w in enumerate(POOL_WINDOWS):
        m = sums[w] / count_of(w)
        mean = m if mean is None else jnp.where(lane >= gi * POOL_GDIM, m, mean)
    return mean - u


def _pool_prompt_body(zp_ref, wbd_ref, sp_ref, o_ref, full_ref, *, tt):
    t = pl.program_id(1)

    @pl.when(t == 0)
    def _():
        full_ref[0:16, :] = jnp.zeros((16, POOL_WIDTH), F32)

    @pl.when(t > 0)
    def _():
        full_ref[0:16, :] = full_ref[tt:tt + 16, :]

    u = zp_ref[0]
    full_ref[16:16 + tt, :] = u
    acc = u
    sums = {}
    for j in range(1, max(POOL_WINDOWS)):
        acc = acc + full_ref[pl.ds(16 - j, tt), :]
        if j + 1 in POOL_WINDOWS:
            sums[j + 1] = acc
    pos = t * tt + lax.broadcasted_iota(jnp.int32, (tt, 1), 0)
    diff = _pool_select(sums, u, lambda w: jnp.minimum(pos + 1, w).astype(F32))
    y = _dot(diff.astype(BF16), wbd_ref[...]) * sp_ref[...]
    o_ref[0] = y.astype(o_ref.dtype)


def _pool_prompt(zp, wbd, sp, tt):
    b, t, _ = zp.shape
    return pl.pallas_call(
        functools.partial(_pool_prompt_body, tt=tt),
        grid=(b, t // tt),
        in_specs=[
            pl.BlockSpec((1, tt, POOL_WIDTH), lambda i, j: (i, j, 0)),
            _resident(wbd.shape),
            _resident(sp.shape),
        ],
        out_specs=pl.BlockSpec((1, tt, POOL_WIDTH), lambda i, j: (i, j, 0)),
        out_shape=jax.ShapeDtypeStruct((b, t, POOL_WIDTH), BF16),
        scratch_shapes=[pltpu.VMEM((tt + 16, POOL_WIDTH), F32)],
        compiler_params=_params("arbitrary", "arbitrary"),
        name="pool_prompt",
    )(zp, wbd, sp)


def _pool_step_body(zp_ref, buf_ref, wbd_ref, sp_ref, o_ref):
    u = zp_ref[...]
    acc = u
    sums = {}
    for j in range(1, max(POOL_WINDOWS)):
        r = POOL_BUF - j
        acc = acc + buf_ref[:, r * POOL_WIDTH:(r + 1) * POOL_WIDTH]
        if j + 1 in POOL_WINDOWS:
            sums[j + 1] = acc
    diff = _pool_select(sums, u, lambda w: float(w))
    y = _dot(diff.astype(BF16), wbd_ref[...]) * sp_ref[...]
    o_ref[...] = y.astype(o_ref.dtype)


def _pool_step(zp, buf_flat, wbd, sp):
    b = zp.shape[0]
    return pl.pallas_call(
        _pool_step_body,
        out_shape=jax.ShapeDtypeStruct((b, POOL_WIDTH), BF16),
        compiler_params=pltpu.CompilerParams(vmem_limit_bytes=VMEM_LIMIT_BYTES),
        name="pool_step",
    )(zp, buf_flat, wbd, sp)


def _mlstm_chunk_body(qkv_ref, zif_ref, zo_ref, bif_ref, gm_ref, c0_ref, n0_ref, m0_ref,
                      h_ref, c_ref, n_ref, m_ref):
    ci = pl.program_id(1)

    @pl.when(ci == 0)
    def _():
        c_ref[...] = c0_ref[...]
        n_ref[...] = n0_ref[...]
        m_ref[...] = m0_ref[...]

    L = CHUNK
    g = zif_ref[0] + bif_ref[...]
    lane = lax.broadcasted_iota(jnp.int32, (L, 128), 1)
    gl = jnp.where((lane >= HEADS) & (lane < 2 * HEADS), _log_sigmoid(g), g)
    row = lax.broadcasted_iota(jnp.int32, (L, L), 0)
    col = lax.broadcasted_iota(jnp.int32, (L, L), 1)
    causal = col <= row
    tri = jnp.where(causal, 1.0, 0.0).astype(BF16)
    hi = gl.astype(BF16)
    r1 = gl - hi.astype(F32)
    mid = r1.astype(BF16)
    lo = (r1 - mid.astype(F32)).astype(BF16)
    cs = _dot(tri, hi) + _dot(tri, mid) + _dot(tri, lo)
    gl_t = gl.T
    cs_t = cs.T
    m_all = m_ref[0]
    m_new_vec = jnp.zeros((1, 128), F32)
    lane_row = lax.broadcasted_iota(jnp.int32, (1, 128), 1)
    for h in range(HEADS):
        hs = slice(h * HDIM, (h + 1) * HDIM)
        ig_col = gl[:, h:h + 1]
        b_col = cs[:, HEADS + h:HEADS + h + 1]
        ig_row = gl_t[h:h + 1, :]
        b_row = cs_t[HEADS + h:HEADS + h + 1, :]
        m_prev = m_all[:, h:h + 1]
        dmat = jnp.where(causal, b_col - b_row + ig_row, -jnp.inf)
        inter = b_col + m_prev
        m_t = jnp.maximum(inter, jnp.max(dmat, axis=-1, keepdims=True))
        w_intra = jnp.exp(dmat - m_t)
        w_inter = jnp.exp(inter - m_t)
        qh = qkv_ref[0, :, hs]
        kh = qkv_ref[0, :, MLSTM_WIDTH + h * HDIM:MLSTM_WIDTH + (h + 1) * HDIM]
        vh = qkv_ref[0, :, 2 * MLSTM_WIDTH + h * HDIM:2 * MLSTM_WIDTH + (h + 1) * HDIM]
        s = _dot_nt(qh, kh) * w_intra
        c_h = c_ref[0, h]
        n_h = n_ref[0, h:h + 1, :]
        num = _dot(s.astype(BF16), vh) + w_inter * _dot_nt(qh, c_h.astype(BF16))
        qn = jnp.sum(qh.astype(F32) * n_h, axis=-1, keepdims=True)
        den = jnp.sum(s, axis=-1, keepdims=True) + w_inter * qn
        hh = num / jnp.maximum(jnp.abs(den), jnp.exp(-m_t))
        hh = hh * lax.rsqrt(jnp.mean(hh * hh, axis=-1, keepdims=True) + RMS_EPS)
        hh = hh * gm_ref[:, hs] * _sigmoid(zo_ref[0, :, hs])
        h_ref[0, :, hs] = hh.astype(h_ref.dtype)
        m_new = m_t[L - 1:L, :]
        b_last = b_col[L - 1:L, :]
        w_state = jnp.exp(b_last - b_col + ig_col - m_new)
        decay = jnp.exp(b_last + m_prev - m_new)
        vw = (vh.astype(F32) * w_state).astype(BF16)
        c_ref[0, h] = decay * c_h + _dot_tn(vw, kh)
        n_ref[0, h:h + 1, :] = decay * n_h + jnp.sum(kh.astype(F32) * w_state, axis=0, keepdims=True)
        m_new_vec = jnp.where(lane_row == h, m_new, m_new_vec)
    m_ref[0] = m_new_vec


def _mlstm_chunks(qkv, zif, zo, bif, gm, c0, n0, m0):
    b, t, _ = qkv.shape
    blk = lambda w: pl.BlockSpec((1, CHUNK, w), lambda i, j: (i, j, 0))
    st_c = pl.BlockSpec((1, HEADS, HDIM, HDIM), lambda i, j: (i, 0, 0, 0))
    st_n = pl.BlockSpec((1, HEADS, HDIM), lambda i, j: (i, 0, 0))
    st_m = pl.BlockSpec((1, 1, 128), lambda i, j: (i, 0, 0))
    return pl.pallas_call(
        _mlstm_chunk_body,
        grid=(b, t // CHUNK),
        in_specs=[blk(3 * MLSTM_WIDTH), blk(128), blk(MLSTM_WIDTH), _resident(bif.shape), _resident(gm.shape),
                  st_c, st_n, st_m],
        out_specs=[blk(MLSTM_WIDTH), st_c, st_n, st_m],
        out_shape=[
            jax.ShapeDtypeStruct((b, t, MLSTM_WIDTH), BF16),
            jax.ShapeDtypeStruct((b, HEADS, HDIM, HDIM), F32),
            jax.ShapeDtypeStruct((b, HEADS, HDIM), F32),
            jax.ShapeDtypeStruct((b, 1, 128), F32),
        ],
        compiler_params=_params("arbitrary", "arbitrary"),
        name="mlstm_chunks",
    )(qkv, zif, zo, bif, gm, c0, n0, m0)


def _mlstm_step_body(qkv_ref, zif_ref, zo_ref, bif_ref, gm_ref, c0_ref, n0_ref, m0_ref,
                     h_ref, c_ref, n_ref, m_ref):
    g = zif_ref[...] + bif_ref[...]
    r = lax.broadcasted_iota(jnp.int32, (HDIM, HDIM), 0)
    c = lax.broadcasted_iota(jnp.int32, (HDIM, HDIM), 1)
    eye = (r == c)[None]
    for h in range(HEADS):
        hs = slice(h * HDIM, (h + 1) * HDIM)
        ig = g[:, :, h:h + 1]
        lf = _log_sigmoid(g[:, :, HEADS + h:HEADS + h + 1])
        m_prev = m0_ref[:, :, h:h + 1]
        q = qkv_ref[:, :, hs]
        k = qkv_ref[:, :, MLSTM_WIDTH + h * HDIM:MLSTM_WIDTH + (h + 1) * HDIM]
        v = qkv_ref[:, :, 2 * MLSTM_WIDTH + h * HDIM:2 * MLSTM_WIDTH + (h + 1) * HDIM]
        c_h = c0_ref[:, h]
        n_h = n0_ref[:, :, hs]
        inter = lf + m_prev
        m_t = jnp.maximum(inter, ig)
        w_intra = jnp.exp(ig - m_t)
        w_inter = jnp.exp(inter - m_t)
        s = jnp.sum(q * k, axis=-1, keepdims=True) * w_intra
        cq = jnp.sum(c_h * q, axis=-1, keepdims=True)
        v_col = jnp.sum(jnp.where(eye, v, 0.0), axis=-1, keepdims=True)
        num = s * v_col + w_inter * cq
        den = s + w_inter * jnp.sum(n_h * q, axis=-1, keepdims=True)
        hcol = num / jnp.maximum(jnp.abs(den), jnp.exp(-m_t))
        hcol = hcol * lax.rsqrt(jnp.mean(hcol * hcol, axis=1, keepdims=True) + RMS_EPS)
        hrow = jnp.sum(jnp.where(eye, hcol, 0.0), axis=1, keepdims=True)
        hrow = hrow * gm_ref[:, hs] * _sigmoid(zo_ref[:, :, hs])
        h_ref[:, :, hs] = hrow.astype(h_ref.dtype)
        w_state = jnp.exp(ig - m_t)
        decay = jnp.exp(lf + m_prev - m_t)
        c_ref[:, h] = decay * c_h + (w_state * v_col) * k
        n_ref[:, :, hs] = decay * n_h + w_state * k
        m_ref[:, :, h:h + 1] = m_t


def _mlstm_step(qkv, zif, zo, bif, gm, c0, n0, m0, bb):
    b = qkv.shape[0]
    blk = lambda w: pl.BlockSpec((bb, 1, w), lambda i: (i, 0, 0))
    st_c = pl.BlockSpec((bb, HEADS, HDIM, HDIM), lambda i: (i, 0, 0, 0))
    return pl.pallas_call(
        _mlstm_step_body,
        grid=(b // bb,),
        in_specs=[blk(3 * MLSTM_WIDTH), blk(128), blk(MLSTM_WIDTH), _resident(bif.shape), _resident(gm.shape),
                  st_c, blk(MLSTM_WIDTH), blk(HEADS)],
        out_specs=[blk(MLSTM_WIDTH), st_c, blk(MLSTM_WIDTH), blk(HEADS)],
        out_shape=[
            jax.ShapeDtypeStruct((b, 1, MLSTM_WIDTH), BF16),
            jax.ShapeDtypeStruct((b, HEADS, HDIM, HDIM), F32),
            jax.ShapeDtypeStruct((b, 1, MLSTM_WIDTH), F32),
            jax.ShapeDtypeStruct((b, 1, HEADS), F32),
        ],
        compiler_params=_params("arbitrary"),
        name="mlstm_step",
    )(qkv, zif, zo, bif, gm, c0, n0, m0)


def _layer_norm(v, g, b):
    vc = v - jnp.mean(v, axis=-1, keepdims=True)
    return vc * lax.rsqrt(jnp.mean(vc * vc, axis=-1, keepdims=True) + LN_EPS) * g + b


def _gmlp_prompt_body(ugv_ref, lng_ref, lnb_ref, ws_ref, bst_ref, o_ref, *, nchunk):
    row = lax.broadcasted_iota(jnp.int32, (CHUNK, CHUNK), 0)
    col = lax.broadcasted_iota(jnp.int32, (CHUNK, CHUNK), 1)
    causal = col <= row
    lane = lax.broadcasted_iota(jnp.int32, (CHUNK, GMLP_WIDTH), 1)
    ws = [jnp.where(causal, ws_ref[g], jnp.zeros((), ws_ref.dtype)) for g in range(GMLP_GROUPS)]
    for c in range(nchunk):
        rows = slice(c * CHUNK, (c + 1) * CHUNK)
        u = _gelu_tanh(ugv_ref[0, rows, 0:GMLP_WIDTH])
        v = _gelu_tanh(ugv_ref[0, rows, GMLP_WIDTH:])
        vn = _layer_norm(v, lng_ref[...], lnb_ref[...]).astype(BF16)
        mixed = None
        for g in range(GMLP_GROUPS):
            mg = _dot(ws[g], vn) + bst_ref[:, g:g + 1]
            mixed = mg if mixed is None else jnp.where(lane >= g * GMLP_GDIM, mg, mixed)
        o_ref[0, rows, :] = (u * mixed).astype(o_ref.dtype)


def _gmlp_prompt(ugv, lng, lnb, ws, bst, tt):
    b, t, _ = ugv.shape
    return pl.pallas_call(
        functools.partial(_gmlp_prompt_body, nchunk=tt // CHUNK),
        grid=(b, t // tt),
        in_specs=[pl.BlockSpec((1, tt, 2 * GMLP_WIDTH), lambda i, j: (i, j, 0)),
                  _resident(lng.shape), _resident(lnb.shape), _resident(ws.shape), _resident(bst.shape)],
        out_specs=pl.BlockSpec((1, tt, GMLP_WIDTH), lambda i, j: (i, j, 0)),
        out_shape=jax.ShapeDtypeStruct((b, t, GMLP_WIDTH), BF16),
        compiler_params=_params("arbitrary", "arbitrary"),
        name="gmlp_prompt",
    )(ugv, lng, lnb, ws, bst)


def _gmlp_step_body(ugv_ref, lng_ref, lnb_ref, w00_ref, b0_ref, o_ref, vn_ref):
    u = _gelu_tanh(ugv_ref[:, 0:GMLP_WIDTH])
    v = _gelu_tanh(ugv_ref[:, GMLP_WIDTH:])
    vn = _layer_norm(v, lng_ref[...], lnb_ref[...])
    vn_ref[...] = vn
    o_ref[...] = (u * (w00_ref[...] * vn + b0_ref[...])).astype(o_ref.dtype)


def _gmlp_step(ugv, lng, lnb, w00, b0):
    b = ugv.shape[0]
    return pl.pallas_call(
        _gmlp_step_body,
        out_shape=[jax.ShapeDtypeStruct((b, GMLP_WIDTH), BF16), jax.ShapeDtypeStruct((b, GMLP_WIDTH), F32)],
        compiler_params=pltpu.CompilerParams(vmem_limit_bytes=VMEM_LIMIT_BYTES),
        name="gmlp_step",
    )(ugv, lng, lnb, w00, b0)


def _merge_body(x_ref, hp_ref, hm_ref, hg_ref, g_ref, wgate_ref, wbp_ref, wbm_ref, wbg_ref, wout_ref, o_ref):
    x = x_ref[...]
    xn = _rms(x, g_ref[0:1, :]).astype(BF16)
    merged = None
    for i, (h_ref, w_ref) in enumerate(((hp_ref, wbp_ref), (hm_ref, wbm_ref), (hg_ref, wbg_ref))):
        gate = _sigmoid(_dot(xn, wgate_ref[i]))
        term = gate * _dot(h_ref[...], w_ref[...])
        merged = term if merged is None else merged + term
    y = _dot(merged.astype(BF16), wout_ref[...])
    o_ref[...] = x + _rms(y, g_ref[1:2, :])


def _merge(x, hp, hm, hg, g, wgate, wbp, wbm, wbg, wout, tm):
    n = x.shape[0]
    row = lambda w: pl.BlockSpec((tm, w), lambda i: (i, 0))
    return pl.pallas_call(
        _merge_body,
        grid=(n // tm,),
        in_specs=[row(D_MODEL), row(POOL_WIDTH), row(MLSTM_WIDTH), row(GMLP_WIDTH), _resident((2, D_MODEL)),
                  _resident(wgate.shape), _resident(wbp.shape), _resident(wbm.shape), _resident(wbg.shape),
                  _resident(wout.shape)],
        out_specs=row(D_MODEL),
        out_shape=jax.ShapeDtypeStruct((n, D_MODEL), F32),
        compiler_params=_params("arbitrary"),
        name="merge",
    )(x, hp, hm, hg, g, wgate, wbp, wbm, wbg, wout)


def _memkv_body(mem_ref, g_ref, wk_ref, wv_ref, k_ref, v_ref, kb_ref, vb_ref):
    mn = _rms(mem_ref[...], g_ref[...]).astype(BF16)
    k = _dot(mn, wk_ref[...])
    v = _dot(mn, wv_ref[...])
    k_ref[...] = k
    v_ref[...] = v
    kb_ref[...] = k.astype(BF16)
    vb_ref[...] = v.astype(BF16)


def _memkv(mem, g, wk, wv, tm):
    n = mem.shape[0]
    row = pl.BlockSpec((tm, D_MODEL), lambda i: (i, 0))
    return pl.pallas_call(
        _memkv_body,
        grid=(n // tm,),
        in_specs=[row, _resident(g.shape), _resident(wk.shape), _resident(wv.shape)],
        out_specs=[row, row, row, row],
        out_shape=[jax.ShapeDtypeStruct((n, D_MODEL), F32), jax.ShapeDtypeStruct((n, D_MODEL), F32),
                   jax.ShapeDtypeStruct((n, D_MODEL), BF16), jax.ShapeDtypeStruct((n, D_MODEL), BF16)],
        compiler_params=_params("arbitrary"),
        name="memkv",
    )(mem, g, wk, wv)


def _attn_prompt_body(x_ref, k_ref, v_ref, g_ref, wq_ref, wo_ref, o_ref):
    x = x_ref[0]
    xn = _rms(x, g_ref[0:1, :]).astype(BF16)
    q = _dot(xn, wq_ref[...]).astype(BF16)
    outs = []
    for h in range(MEM_HEADS):
        hs = slice(h * MEM_HDIM, (h + 1) * MEM_HDIM)
        s = _dot_nt(q[:, hs], k_ref[0, :, hs]) * (MEM_HDIM ** -0.5)
        e = jnp.exp(s - jnp.max(s, axis=-1, keepdims=True))
        p = e / jnp.sum(e, axis=-1, keepdims=True)
        outs.append(_dot(p.astype(BF16), v_ref[0, :, hs]).astype(BF16))
    o = jnp.concatenate(outs, axis=-1)
    y = _dot(o, wo_ref[...])
    o_ref[0] = x + _rms(y, g_ref[1:2, :])


def _attn_prompt(x, kb, vb, g, wq, wo, tq):
    b, t, _ = x.shape
    xs = pl.BlockSpec((1, tq, D_MODEL), lambda i, j: (i, j, 0))
    ms = pl.BlockSpec((1, MEM_TOKENS, D_MODEL), lambda i, j: (i, 0, 0))
    return pl.pallas_call(
        _attn_prompt_body,
        grid=(b, t // tq),
        in_specs=[xs, ms, ms, _resident((2, D_MODEL)), _resident(wq.shape), _resident(wo.shape)],
        out_specs=xs,
        out_shape=jax.ShapeDtypeStruct((b, t, D_MODEL), F32),
        compiler_params=_params("arbitrary", "arbitrary"),
        name="attn_prompt",
    )(x, kb, vb, g, wq, wo)


def _qproj_body(x_ref, g_ref, wq_ref, q_ref):
    xn = _rms(x_ref[...], g_ref[0:1, :]).astype(BF16)
    q_ref[...] = _dot(xn, wq_ref[...])


def _qproj(x, g, wq):
    n = x.shape[0]
    return pl.pallas_call(
        _qproj_body,
        out_shape=jax.ShapeDtypeStruct((n, D_MODEL), F32),
        compiler_params=pltpu.CompilerParams(vmem_limit_bytes=VMEM_LIMIT_BYTES),
        name="qproj",
    )(x, g, wq)


def _attn_step_body(q_ref, k_ref, v_ref, o_ref):
    for h in range(MEM_HEADS):
        hs = slice(h * MEM_HDIM, (h + 1) * MEM_HDIM)
        s = jnp.sum(k_ref[:, :, hs] * q_ref[:, :, hs], axis=-1, keepdims=True) * (MEM_HDIM ** -0.5)
        e = jnp.exp(s - jnp.max(s, axis=1, keepdims=True))
        p = e / jnp.sum(e, axis=1, keepdims=True)
        o_ref[:, :, hs] = jnp.sum(p * v_ref[:, :, hs], axis=1, keepdims=True)


def _attn_step(q, k, v, bb):
    b = q.shape[0]
    qs = pl.BlockSpec((bb, 1, D_MODEL), lambda i: (i, 0, 0))
    ms = pl.BlockSpec((bb, MEM_TOKENS, D_MODEL), lambda i: (i, 0, 0))
    return pl.pallas_call(
        _attn_step_body,
        grid=(b // bb,),
        in_specs=[qs, ms, ms],
        out_specs=qs,
        out_shape=jax.ShapeDtypeStruct((b, 1, D_MODEL), F32),
        compiler_params=_params("arbitrary"),
        name="attn_step",
    )(q, k, v)


def _oproj_body(x_ref, o_ref_in, g_ref, wo_ref, out_ref):
    y = _dot(o_ref_in[...].astype(BF16), wo_ref[...])
    out_ref[...] = x_ref[...] + _rms(y, g_ref[1:2, :])


def _oproj(x, o, g, wo):
    n = x.shape[0]
    return pl.pallas_call(
        _oproj_body,
        out_shape=jax.ShapeDtypeStruct((n, D_MODEL), F32),
        compiler_params=pltpu.CompilerParams(vmem_limit_bytes=VMEM_LIMIT_BYTES),
        name="oproj",
    )(x, o, g, wo)


def _prep_layer(l, norm_g, w_ff_in, w_ff_out, w_in, b_igate, b_fgate, w_pool, s_pool, g_mlstm,
                gmlp_ln_g, gmlp_ln_b, w_s, b_s, w_br_pool, w_br_mlstm, w_br_gmlp, w_out,
                g_mem, w_mq, w_mk, w_mv, w_mo):
    p = {"norm": norm_g[l]}
    for i in range(2):
        wi = w_ff_in[l, i].astype(BF16)
        p[f"wg{i}"] = wi[:, :D_FF].reshape(D_MODEL, N_FF_CHUNKS, FF_CHUNK).transpose(1, 0, 2)
        p[f"wu{i}"] = wi[:, D_FF:].reshape(D_MODEL, N_FF_CHUNKS, FF_CHUNK).transpose(1, 0, 2)
        p[f"wd{i}"] = w_ff_out[l, i].astype(BF16).reshape(N_FF_CHUNKS, FF_CHUNK, D_MODEL)
    wl = w_in[l]
    w_if = jnp.pad(wl[:, OFF_IF:OFF_U], ((0, 0), (0, 128 - 2 * HEADS)))
    p["w1"] = jnp.concatenate([wl[:, :OFF_IF], wl[:, OFF_U:OFF_GATE], w_if], axis=1).astype(BF16)
    p["wgate"] = wl[:, OFF_GATE:].astype(BF16).reshape(D_MODEL, 3, D_MODEL).transpose(1, 0, 2)
    p["bif"] = jnp.pad(jnp.concatenate([b_igate[l], b_fgate[l]]), (0, 128 - 2 * HEADS)).reshape(1, 128)
    wbd = jnp.zeros((POOL_WIDTH, POOL_WIDTH), F32)
    for g in range(len(POOL_WINDOWS)):
        gs = slice(g * POOL_GDIM, (g + 1) * POOL_GDIM)
        wbd = wbd.at[gs, gs].set(w_pool[l, g])
    p["wbd"] = wbd.astype(BF16)
    p["sp"] = s_pool[l].reshape(1, POOL_WIDTH)
    p["gm"] = g_mlstm[l].reshape(1, MLSTM_WIDTH)
    p["lng"] = gmlp_ln_g[l].reshape(1, GMLP_WIDTH)
    p["lnb"] = gmlp_ln_b[l].reshape(1, GMLP_WIDTH)
    p["ws"] = w_s[l].astype(BF16)
    p["bst"] = b_s[l].T
    p["w00"] = jnp.repeat(w_s[l, :, 0, 0], GMLP_GDIM).reshape(1, GMLP_WIDTH)
    p["b0"] = jnp.repeat(b_s[l, :, 0], GMLP_GDIM).reshape(1, GMLP_WIDTH)
    p["wbp"] = w_br_pool[l].astype(BF16)
    p["wbm"] = w_br_mlstm[l].astype(BF16)
    p["wbg"] = w_br_gmlp[l].astype(BF16)
    p["wout"] = w_out[l].astype(BF16)
    p["gmem"] = g_mem[l].reshape(1, D_MODEL)
    p["wq"] = w_mq[l].astype(BF16)
    p["wk"] = w_mk[l].astype(BF16)
    p["wv"] = w_mv[l].astype(BF16)
    p["wo"] = w_mo[l].astype(BF16)
    return p


def _prompt_layer(x, mem2d, p, batch, seq):
    tm = 512
    k32, v32, kb, vb = _memkv(mem2d, p["gmem"], p["wk"], p["wv"], tm)
    x = _ffn(x, p["norm"][0], p["wg0"], p["wu0"], p["wd0"], tm)
    zp, qkv, zo, ugv, zif = _inproj(x, p["norm"][1], p["w1"], tm, BF16)
    b3 = lambda a: a.reshape(batch, seq, a.shape[-1])
    hp = _pool_prompt(b3(zp), p["wbd"], p["sp"], 512)
    c0 = jnp.zeros((batch, HEADS, HDIM, HDIM), F32)
    n0 = jnp.zeros((batch, HEADS, HDIM), F32)
    m0 = jnp.zeros((batch, 1, 128), F32)
    hm, c_new, n_new, m_new = _mlstm_chunks(b3(qkv), b3(zif), b3(zo), p["bif"], p["gm"], c0, n0, m0)
    hg = _gmlp_prompt(b3(ugv), p["lng"], p["lnb"], p["ws"], p["bst"], 512)
    f2 = lambda a: a.reshape(batch * seq, a.shape[-1])
    x = _merge(x, f2(hp), f2(hm), f2(hg), p["norm"][1], p["wgate"], p["wbp"], p["wbm"], p["wbg"], p["wout"], tm)
    x = _attn_prompt(b3(x), kb.reshape(batch, MEM_TOKENS, D_MODEL), vb.reshape(batch, MEM_TOKENS, D_MODEL),
                     p["norm"][2], p["wq"], p["wo"], 512).reshape(batch * seq, D_MODEL)
    x = _ffn(x, p["norm"][3], p["wg1"], p["wu1"], p["wd1"], tm)
    pool_buf = b3(zp)[:, seq - POOL_BUF:, :]
    mk = k32.reshape(batch, MEM_TOKENS, MEM_HEADS, MEM_HDIM)
    mv = v32.reshape(batch, MEM_TOKENS, MEM_HEADS, MEM_HDIM)
    return x, pool_buf, c_new, n_new, m_new[:, 0, :HEADS], mk, mv


def _sample_layer(x, p, pool_buf, c0, n0, m0, mem_k, mem_v):
    batch = x.shape[0]
    tm = batch
    x = _ffn(x, p["norm"][0], p["wg0"], p["wu0"], p["wd0"], tm)
    zp, qkv, zo, ugv, zif = _inproj(x, p["norm"][1], p["w1"], tm, F32)
    hp = _pool_step(zp, pool_buf.reshape(batch, POOL_BUF * POOL_WIDTH), p["wbd"], p["sp"])
    r3 = lambda a: a.reshape(batch, 1, a.shape[-1])
    hm, c_new, n_new, m_new = _mlstm_step(r3(qkv), r3(zif), r3(zo), p["bif"], p["gm"],
                                          c0, n0.reshape(batch, 1, MLSTM_WIDTH), r3(m0), 16)
    hg, vn = _gmlp_step(ugv, p["lng"], p["lnb"], p["w00"], p["b0"])
    x = _merge(x, hp, hm.reshape(batch, MLSTM_WIDTH), hg, p["norm"][1], p["wgate"], p["wbp"], p["wbm"], p["wbg"],
               p["wout"], tm)
    q = _qproj(x, p["norm"][2], p["wq"])
    o = _attn_step(r3(q), mem_k.reshape(batch, MEM_TOKENS, D_MODEL), mem_v.reshape(batch, MEM_TOKENS, D_MODEL), 4)
    x = _oproj(x, o.reshape(batch, D_MODEL), p["norm"][2], p["wo"])
    x = _ffn(x, p["norm"][3], p["wg1"], p["wu1"], p["wd1"], tm)
    new_buf = jnp.concatenate([pool_buf[:, 1:, :], zp[:, None, :]], axis=1)
    return (x, new_buf, c_new, n_new.reshape(batch, HEADS, HDIM), m_new.reshape(batch, HEADS),
            vn.reshape(batch, 1, GMLP_WIDTH))


def kernel(x_prompt, x_sample, state_pool, state_mlstm_C, state_mlstm_n, state_mlstm_m, cache_mem_k, cache_mem_v, mem_prompt, norm_g, w_ff_in, w_ff_out, w_in, b_igate, b_fgate, w_pool, s_pool, g_mlstm, gmlp_ln_g, gmlp_ln_b, w_s, b_s, w_br_pool, w_br_mlstm, w_br_gmlp, w_out, g_mem, w_mq, w_mk, w_mv, w_mo):
    batch, seq, _ = x_prompt.shape
    dec_batch = x_sample.shape[0]
    assert x_sample.shape[1] == 1 and seq % CHUNK == 0 and PAST_LEN % CHUNK == 0 and PAST_LEN >= max(POOL_WINDOWS)
    xp = x_prompt.reshape(batch * seq, D_MODEL)
    xs = x_sample.reshape(dec_batch, D_MODEL)
    mem2d = mem_prompt.reshape(batch * MEM_TOKENS, D_MODEL)
    outs = [[] for _ in range(11)]
    for l in range(DEPTH):
        p = _prep_layer(l, norm_g, w_ff_in, w_ff_out, w_in, b_igate, b_fgate, w_pool, s_pool, g_mlstm,
                        gmlp_ln_g, gmlp_ln_b, w_s, b_s, w_br_pool, w_br_mlstm, w_br_gmlp, w_out,
                        g_mem, w_mq, w_mk, w_mv, w_mo)
        xp, pb, c, n, m, mk, mv = _prompt_layer(xp, mem2d, p, batch, seq)
        xs, pbs, cs, ns, ms, vr = _sample_layer(xs, p, state_pool[l], state_mlstm_C[l], state_mlstm_n[l],
                                                state_mlstm_m[l], cache_mem_k[l], cache_mem_v[l])
        for lst, val in zip(outs, (pb, pbs, c, n, m, cs, ns, ms, mk, mv, vr)):
            lst.append(val)
    st = [jnp.stack(o) for o in outs]
    return (xp.reshape(batch, seq, D_MODEL), xs.reshape(dec_batch, 1, D_MODEL), *st)
```

```python
import functools

import jax
import jax.numpy as jnp
from jax import lax
from jax.experimental import pallas as pl
from jax.experimental.pallas import tpu as pltpu

F32 = jnp.float32
BF16 = jnp.bfloat16

D_MODEL = 1024
DEPTH = 4
PAST_LEN = 16384
POOL_WINDOWS = (2, 4, 8, 16)
POOL_WIDTH = 256
POOL_GDIM = 64
POOL_BUF = 15
HEADS = 4
MLSTM_WIDTH = 512
HDIM = 128
CHUNK = 128
GMLP_WIDTH = 256
GMLP_GROUPS = 4
GMLP_GDIM = 64
MEM_TOKENS = 256
MEM_HEADS = 4
MEM_HDIM = 256
D_FF = 2816
FF_CHUNK = 256
N_FF_CHUNKS = D_FF // FF_CHUNK
RMS_EPS = 1e-6
LN_EPS = 1e-5
OFF_Q = POOL_WIDTH
OFF_IF = OFF_Q + 4 * MLSTM_WIDTH
OFF_U = OFF_IF + 2 * HEADS
OFF_GATE = OFF_U + 2 * GMLP_WIDTH
W1_QKV = POOL_WIDTH
W1_O = W1_QKV + 3 * MLSTM_WIDTH
W1_UGV = W1_O + MLSTM_WIDTH
W1_IF = W1_UGV + 2 * GMLP_WIDTH
W1_ROWS = W1_IF + 128

VMEM_LIMIT_BYTES = 56 * 1024 * 1024


def _params(*sem):
    return pltpu.CompilerParams(dimension_semantics=sem, vmem_limit_bytes=VMEM_LIMIT_BYTES)


def _pick(tail, *idx):
    nt = len(tail)
    return pl.BlockSpec((None,) * len(idx) + tuple(tail), lambda *_: tuple(idx) + (0,) * nt,
                        pipeline_mode=pl.Buffered(1))


def _rms(x, g):
    return x * lax.rsqrt(jnp.mean(x * x, axis=-1, keepdims=True) + RMS_EPS) * g


def _dot(a, b):
    return jnp.dot(a, b, preferred_element_type=F32)


def _dot_nt(a, b):
    return lax.dot_general(a, b, (((1,), (1,)), ((), ())), preferred_element_type=F32)


def _dot_tn(a, b):
    return lax.dot_general(a, b, (((0,), (0,)), ((), ())), preferred_element_type=F32)


def _sigmoid(x):
    return 1.0 / (1.0 + jnp.exp(-x))


def _gelu_tanh(x):
    return 0.5 * x * (1.0 + jnp.tanh(0.7978845608028654 * (x + 0.044715 * (x * x * x))))


def _log_sigmoid(x):
    return -(jnp.maximum(-x, 0.0) + jnp.log(1.0 + jnp.exp(-jnp.abs(x))))


def _ffn_body(x_ref, g_ref, wi_ref, wo_ref, o_ref, acc_ref):
    x = x_ref[...]
    xn = _rms(x, g_ref[0:1, :]).astype(BF16)
    for c in range(N_FF_CHUNKS):
        cols = slice(c * FF_CHUNK, (c + 1) * FF_CHUNK)
        gate = _dot(xn, wi_ref[:, cols])
        up = _dot(xn, wi_ref[:, D_FF + c * FF_CHUNK:D_FF + (c + 1) * FF_CHUNK])
        act = (gate * _sigmoid(gate) * up).astype(BF16)
        part = _dot(act, wo_ref[cols, :])
        if c == 0:
            acc_ref[...] = part
        else:
            acc_ref[...] += part
    o_ref[...] = x + 0.5 * _rms(acc_ref[...], g_ref[1:2, :])


def _ffn(x, norm_g, w_in, w_out, l, sub, half, tm):
    n = x.shape[0]
    return pl.pallas_call(
        _ffn_body,
        grid=(n // tm,),
        in_specs=[
            pl.BlockSpec((tm, D_MODEL), lambda i: (i, 0)),
            _pick((2, D_MODEL), l, sub),
            _pick((D_MODEL, 2 * D_FF), l, half),
            _pick((D_FF, D_MODEL), l, half),
        ],
        out_specs=pl.BlockSpec((tm, D_MODEL), lambda i: (i, 0)),
        out_shape=jax.ShapeDtypeStruct((n, D_MODEL), F32),
        scratch_shapes=[pltpu.VMEM((tm, D_MODEL), F32)],
        compiler_params=_params("arbitrary"),
        name="ffn",
    )(x, norm_g, w_in, w_out)


def _inproj_body(x_ref, g_ref, w_ref, zp_ref, qkv_ref, zo_ref, ugv_ref, zif_ref):
    xn = _rms(x_ref[...], g_ref[0:1, :]).astype(BF16)
    z = _dot_nt(xn, w_ref[...])
    zp_ref[...] = z[:, :W1_QKV]
    q = z[:, W1_QKV:W1_QKV + MLSTM_WIDTH]
    k = z[:, W1_QKV + MLSTM_WIDTH:W1_QKV + 2 * MLSTM_WIDTH] * (HDIM ** -0.5)
    v = z[:, W1_QKV + 2 * MLSTM_WIDTH:W1_O]
    qkv_ref[:, 0:MLSTM_WIDTH] = q.astype(qkv_ref.dtype)
    qkv_ref[:, MLSTM_WIDTH:2 * MLSTM_WIDTH] = k.astype(qkv_ref.dtype)
    qkv_ref[:, 2 * MLSTM_WIDTH:] = v.astype(qkv_ref.dtype)
    zo_ref[...] = z[:, W1_O:W1_UGV]
    ugv_ref[...] = z[:, W1_UGV:W1_IF]
    zif_ref[...] = z[:, W1_IF:]


def _inproj(x, norm_g, w1t, l, tm, qkv_dtype):
    n = x.shape[0]
    row = lambda w: pl.BlockSpec((tm, w), lambda i: (i, 0))
    return pl.pallas_call(
        _inproj_body,
        grid=(n // tm,),
        in_specs=[row(D_MODEL), _pick((2, D_MODEL), l, 1), _pick((W1_ROWS, D_MODEL), l)],
        out_specs=[row(POOL_WIDTH), row(3 * MLSTM_WIDTH), row(MLSTM_WIDTH), row(2 * GMLP_WIDTH), row(128)],
        out_shape=[
            jax.ShapeDtypeStruct((n, POOL_WIDTH), F32),
            jax.ShapeDtypeStruct((n, 3 * MLSTM_WIDTH), qkv_dtype),
            jax.ShapeDtypeStruct((n, MLSTM_WIDTH), F32),
            jax.ShapeDtypeStruct((n, 2 * GMLP_WIDTH), F32),
            jax.ShapeDtypeStruct((n, 128), F32),
        ],
        compiler_params=_params("arbitrary"),
        name="inproj",
    )(x, norm_g, w1t)


def _pool_select(sums, u, count_of):
    lane = lax.broadcasted_iota(jnp.int32, u.shape, u.ndim - 1)
    mean = None
    for gi, w in enumerate(POOL_WINDOWS):
        m = sums[w] / count_of(w)
        mean = m if mean is None else jnp.where(lane >= gi * POOL_GDIM, m, mean)
    return mean - u


def _pool_prompt_body(zp_ref, wbd_ref, sp_ref, o_ref, full_ref, *, tt):
    t = pl.program_id(1)

    @pl.when(t == 0)
    def _():
        full_ref[0:16, :] = jnp.zeros((16, POOL_WIDTH), F32)

    @pl.when(t > 0)
    def _():
        full_ref[0:16, :] = full_ref[tt:tt + 16, :]

    u = zp_ref[0]
    full_ref[16:16 + tt, :] = u
    acc = u
    sums = {}
    for j in range(1, max(POOL_WINDOWS)):
        acc = acc + full_ref[pl.ds(16 - j, tt), :]
        if j + 1 in POOL_WINDOWS:
            sums[j + 1] = acc
    pos = t * tt + lax.broadcasted_iota(jnp.int32, (tt, 1), 0)
    diff = _pool_select(sums, u, lambda w: jnp.minimum(pos + 1, w).astype(F32))
    y = _dot(diff.astype(BF16), wbd_ref[...]) * sp_ref[...]
    o_ref[0] = y.astype(o_ref.dtype)


def _pool_prompt(zp, wbd, sp, l, tt):
    b, t, _ = zp.shape
    return pl.pallas_call(
        functools.partial(_pool_prompt_body, tt=tt),
        grid=(b, t // tt),
        in_specs=[
            pl.BlockSpec((1, tt, POOL_WIDTH), lambda i, j: (i, j, 0)),
            _pick((POOL_WIDTH, POOL_WIDTH), l),
            _pick((1, POOL_WIDTH), l),
        ],
        out_specs=pl.BlockSpec((1, tt, POOL_WIDTH), lambda i, j: (i, j, 0)),
        out_shape=jax.ShapeDtypeStruct((b, t, POOL_WIDTH), BF16),
        scratch_shapes=[pltpu.VMEM((tt + 16, POOL_WIDTH), F32)],
        compiler_params=_params("arbitrary", "arbitrary"),
        name="pool_prompt",
    )(zp, wbd, sp)


def _pool_step_body(zp_ref, buf_ref, wbd_ref, sp_ref, o_ref, nbuf_ref):
    u = zp_ref[...]
    acc = u
    sums = {}
    for j in range(1, max(POOL_WINDOWS)):
        acc = acc + buf_ref[POOL_BUF - j]
        if j + 1 in POOL_WINDOWS:
            sums[j + 1] = acc
    diff = _pool_select(sums, u, lambda w: float(w))
    y = _dot(diff.astype(BF16), wbd_ref[...]) * sp_ref[...]
    o_ref[...] = y.astype(o_ref.dtype)
    nbuf_ref[0:POOL_BUF - 1] = buf_ref[1:POOL_BUF]
    nbuf_ref[POOL_BUF - 1] = u


def _pool_step(zp, pool_t, wbd, sp, l):
    b = zp.shape[0]
    full = lambda shape: pl.BlockSpec(shape, lambda i: (0,) * len(shape))
    return pl.pallas_call(
        _pool_step_body,
        grid=(1,),
        in_specs=[full((b, POOL_WIDTH)), _pick((POOL_BUF, b, POOL_WIDTH), l),
                  _pick((POOL_WIDTH, POOL_WIDTH), l), _pick((1, POOL_WIDTH), l)],
        out_specs=[full((b, POOL_WIDTH)), full((POOL_BUF, b, POOL_WIDTH))],
        out_shape=[jax.ShapeDtypeStruct((b, POOL_WIDTH), BF16),
                   jax.ShapeDtypeStruct((POOL_BUF, b, POOL_WIDTH), F32)],
        compiler_params=_params("arbitrary"),
        name="pool_step",
    )(zp, pool_t, wbd, sp)


def _mlstm_chunk_one(bi, causal, tri, qkv_ref, zif_ref, zo_ref, bif_ref, gm_ref, h_ref, c_ref, n_ref, m_ref):
    L = CHUNK
    g = zif_ref[bi] + bif_ref[...]
    lane = lax.broadcasted_iota(jnp.int32, (L, 128), 1)
    gl = jnp.where((lane >= HEADS) & (lane < 2 * HEADS), _log_sigmoid(g), g)
    hi = gl.astype(BF16)
    r1 = gl - hi.astype(F32)
    mid = r1.astype(BF16)
    lo = (r1 - mid.astype(F32)).astype(BF16)
    cs = _dot(tri, hi) + _dot(tri, mid) + _dot(tri, lo)
    gl_t = gl.T
    cs_t = cs.T
    m_all = m_ref[bi]
    m_new_vec = jnp.zeros((1, 128), F32)
    lane_row = lax.broadcasted_iota(jnp.int32, (1, 128), 1)
    for h in range(HEADS):
        hs = slice(h * HDIM, (h + 1) * HDIM)
        ig_col = gl[:, h:h + 1]
        b_col = cs[:, HEADS + h:HEADS + h + 1]
        ig_row = gl_t[h:h + 1, :]
        b_row = cs_t[HEADS + h:HEADS + h + 1, :]
        m_prev = m_all[:, h:h + 1]
        dmat = jnp.where(causal, b_col - b_row + ig_row, -jnp.inf)
        inter = b_col + m_prev
        m_t = jnp.maximum(inter, jnp.max(dmat, axis=-1, keepdims=True))
        w_intra = jnp.exp(dmat - m_t)
        w_inter = jnp.exp(inter - m_t)
        qh = qkv_ref[bi, :, hs]
        kh = qkv_ref[bi, :, MLSTM_WIDTH + h * HDIM:MLSTM_WIDTH + (h + 1) * HDIM]
        vh = qkv_ref[bi, :, 2 * MLSTM_WIDTH + h * HDIM:2 * MLSTM_WIDTH + (h + 1) * HDIM]
        s = _dot_nt(qh, kh) * w_intra
        c_h = c_ref[bi, h]
        n_h = n_ref[bi, h:h + 1, :]
        num = _dot(s.astype(BF16), vh) + w_inter * _dot_nt(qh, c_h.astype(BF16))
        qn = jnp.sum(qh.astype(F32) * n_h, axis=-1, keepdims=True)
        den = jnp.sum(s, axis=-1, keepdims=True) + w_inter * qn
        hh = num / jnp.maximum(jnp.abs(den), jnp.exp(-m_t))
        hh = hh * lax.rsqrt(jnp.mean(hh * hh, axis=-1, keepdims=True) + RMS_EPS)
        hh = hh * gm_ref[:, hs] * _sigmoid(zo_ref[bi, :, hs])
        h_ref[bi, :, hs] = hh.astype(h_ref.dtype)
        m_new = m_t[L - 1:L, :]
        b_last = b_col[L - 1:L, :]
        w_state = jnp.exp(b_last - b_col + ig_col - m_new)
        decay = jnp.exp(b_last + m_prev - m_new)
        vw = (vh.astype(F32) * w_state).astype(BF16)
        c_ref[bi, h] = decay * c_h + _dot_tn(vw, kh)
        n_ref[bi, h:h + 1, :] = decay * n_h + jnp.sum(kh.astype(F32) * w_state, axis=0, keepdims=True)
        m_new_vec = jnp.where(lane_row == h, m_new, m_new_vec)
    m_ref[bi] = m_new_vec


def _mlstm_chunk_body(qkv_ref, zif_ref, zo_ref, bif_ref, gm_ref, h_ref, c_ref, n_ref, m_ref, *, bb):
    @pl.when(pl.program_id(1) == 0)
    def _():
        c_ref[...] = jnp.zeros(c_ref.shape, F32)
        n_ref[...] = jnp.zeros(n_ref.shape, F32)
        m_ref[...] = jnp.zeros(m_ref.shape, F32)

    row = lax.broadcasted_iota(jnp.int32, (CHUNK, CHUNK), 0)
    col = lax.broadcasted_iota(jnp.int32, (CHUNK, CHUNK), 1)
    causal = col <= row
    tri = jnp.where(causal, 1.0, 0.0).astype(BF16)
    for bi in range(bb):
        _mlstm_chunk_one(bi, causal, tri, qkv_ref, zif_ref, zo_ref, bif_ref, gm_ref, h_ref, c_ref, n_ref, m_ref)


def _mlstm_chunks(qkv, zif, zo, bif, gm, l, bb):
    b, t, _ = qkv.shape
    blk = lambda w: pl.BlockSpec((bb, CHUNK, w), lambda i, j: (i, j, 0))
    st_c = pl.BlockSpec((bb, HEADS, HDIM, HDIM), lambda i, j: (i, 0, 0, 0))
    st_n = pl.BlockSpec((bb, HEADS, HDIM), lambda i, j: (i, 0, 0))
    st_m = pl.BlockSpec((bb, 1, 128), lambda i, j: (i, 0, 0))
    return pl.pallas_call(
        functools.partial(_mlstm_chunk_body, bb=bb),
        grid=(b // bb, t // CHUNK),
        in_specs=[blk(3 * MLSTM_WIDTH), blk(128), blk(MLSTM_WIDTH), _pick((1, 128), l), _pick((1, MLSTM_WIDTH), l)],
        out_specs=[blk(MLSTM_WIDTH), st_c, st_n, st_m],
        out_shape=[
            jax.ShapeDtypeStruct((b, t, MLSTM_WIDTH), BF16),
            jax.ShapeDtypeStruct((b, HEADS, HDIM, HDIM), F32),
            jax.ShapeDtypeStruct((b, HEADS, HDIM), F32),
            jax.ShapeDtypeStruct((b, 1, 128), F32),
        ],
        compiler_params=_params("arbitrary", "arbitrary"),
        name="mlstm_chunks",
    )(qkv, zif, zo, bif, gm)


def _mlstm_step_body(*refs, aliased):
    qkv_ref, zif_ref, zo_ref, bif_ref, gm_ref, c0_ref, n0_ref, m0_ref = refs[:8]
    h_ref, c_ref, n_ref, m_ref = refs[9:] if aliased else refs[8:]
    g = zif_ref[...] + bif_ref[...]
    r = lax.broadcasted_iota(jnp.int32, (HDIM, HDIM), 0)
    c = lax.broadcasted_iota(jnp.int32, (HDIM, HDIM), 1)
    eye = (r == c)[None]
    for h in range(HEADS):
        hs = slice(h * HDIM, (h + 1) * HDIM)
        ig = g[:, :, h:h + 1]
        lf = _log_sigmoid(g[:, :, HEADS + h:HEADS + h + 1])
        m_prev = m0_ref[:, :, h:h + 1]
        q = qkv_ref[:, :, hs]
        k = qkv_ref[:, :, MLSTM_WIDTH + h * HDIM:MLSTM_WIDTH + (h + 1) * HDIM]
        v = qkv_ref[:, :, 2 * MLSTM_WIDTH + h * HDIM:2 * MLSTM_WIDTH + (h + 1) * HDIM]
        c_h = c0_ref[:, h]
        n_h = n0_ref[:, h:h + 1, :]
        inter = lf + m_prev
        m_t = jnp.maximum(inter, ig)
        w_intra = jnp.exp(ig - m_t)
        w_inter = jnp.exp(inter - m_t)
        s = jnp.sum(q * k, axis=-1, keepdims=True) * w_intra
        cq = jnp.sum(c_h * q, axis=-1, keepdims=True)
        v_col = jnp.sum(jnp.where(eye, v, 0.0), axis=-1, keepdims=True)
        num = s * v_col + w_inter * cq
        den = s + w_inter * jnp.sum(n_h * q, axis=-1, keepdims=True)
        hcol = num / jnp.maximum(jnp.abs(den), jnp.exp(-m_t))
        hcol = hcol * lax.rsqrt(jnp.mean(hcol * hcol, axis=1, keepdims=True) + RMS_EPS)
        hrow = jnp.sum(jnp.where(eye, hcol, 0.0), axis=1, keepdims=True)
        hrow = hrow * gm_ref[:, hs] * _sigmoid(zo_ref[:, :, hs])
        h_ref[:, :, hs] = hrow.astype(h_ref.dtype)
        w_state = jnp.exp(ig - m_t)
        decay = jnp.exp(lf + m_prev - m_t)
        c_ref[:, h] = decay * c_h + (w_state * v_col) * k
        n_ref[:, h:h + 1, :] = decay * n_h + w_state * k
        m_ref[:, :, h:h + 1] = m_t


def _mlstm_step(qkv, zif, zo, bif, gm, c_all, n_all, m_all, c_prev, l, bb):
    b = qkv.shape[0]
    blk = lambda w: pl.BlockSpec((bb, 1, w), lambda i: (i, 0, 0))
    st_c = pl.BlockSpec((None, bb, HEADS, HDIM, HDIM), lambda i: (l, i, 0, 0, 0))
    st_n = pl.BlockSpec((None, bb, HEADS, HDIM), lambda i: (l, i, 0, 0))
    st_m = pl.BlockSpec((None, bb, 1, HEADS), lambda i: (l, i, 0, 0))
    in_specs = [blk(3 * MLSTM_WIDTH), blk(128), blk(MLSTM_WIDTH), _pick((1, 128), l), _pick((1, MLSTM_WIDTH), l),
                st_c, st_n, st_m]
    args = [qkv, zif, zo, bif, gm, c_all, n_all, m_all]
    aliases = {}
    if c_prev is not None:
        in_specs.append(pl.BlockSpec(memory_space=pl.ANY))
        args.append(c_prev)
        aliases = {8: 1}
    return pl.pallas_call(
        functools.partial(_mlstm_step_body, aliased=c_prev is not None),
        grid=(b // bb,),
        in_specs=in_specs,
        out_specs=[blk(MLSTM_WIDTH), st_c, pl.BlockSpec((bb, HEADS, HDIM), lambda i: (i, 0, 0)),
                   pl.BlockSpec((bb, 1, HEADS), lambda i: (i, 0, 0))],
        out_shape=[
            jax.ShapeDtypeStruct((b, 1, MLSTM_WIDTH), BF16),
            jax.ShapeDtypeStruct(c_all.shape, F32),
            jax.ShapeDtypeStruct((b, HEADS, HDIM), F32),
            jax.ShapeDtypeStruct((b, 1, HEADS), F32),
        ],
        input_output_aliases=aliases,
        compiler_params=_params("arbitrary"),
        name="mlstm_step",
    )(*args)


def _layer_norm(v, g, b):
    vc = v - jnp.mean(v, axis=-1, keepdims=True)
    return vc * lax.rsqrt(jnp.mean(vc * vc, axis=-1, keepdims=True) + LN_EPS) * g + b


def _gmlp_prompt_body(ugv_ref, lng_ref, lnb_ref, ws_ref, bst_ref, o_ref, *, nchunk):
    row = lax.broadcasted_iota(jnp.int32, (CHUNK, CHUNK), 0)
    col = lax.broadcasted_iota(jnp.int32, (CHUNK, CHUNK), 1)
    causal = col <= row
    lane = lax.broadcasted_iota(jnp.int32, (CHUNK, GMLP_WIDTH), 1)
    ws = [jnp.where(causal, ws_ref[g], jnp.zeros((), ws_ref.dtype)) for g in range(GMLP_GROUPS)]
    for c in range(nchunk):
        rows = slice(c * CHUNK, (c + 1) * CHUNK)
        u = _gelu_tanh(ugv_ref[0, rows, 0:GMLP_WIDTH])
        v = _gelu_tanh(ugv_ref[0, rows, GMLP_WIDTH:])
        vn = _layer_norm(v, lng_ref[...], lnb_ref[...]).astype(BF16)
        mixed = None
        for g in range(GMLP_GROUPS):
            mg = _dot(ws[g], vn) + bst_ref[:, g:g + 1]
            mixed = mg if mixed is None else jnp.where(lane >= g * GMLP_GDIM, mg, mixed)
        o_ref[0, rows, :] = (u * mixed).astype(o_ref.dtype)


def _gmlp_prompt(ugv, lng, lnb, ws, bst, l, tt):
    b, t, _ = ugv.shape
    return pl.pallas_call(
        functools.partial(_gmlp_prompt_body, nchunk=tt // CHUNK),
        grid=(b, t // tt),
        in_specs=[pl.BlockSpec((1, tt, 2 * GMLP_WIDTH), lambda i, j: (i, j, 0)),
                  _pick((1, GMLP_WIDTH), l), _pick((1, GMLP_WIDTH), l),
                  _pick((GMLP_GROUPS, CHUNK, CHUNK), l), _pick((CHUNK, GMLP_GROUPS), l)],
        out_specs=pl.BlockSpec((1, tt, GMLP_WIDTH), lambda i, j: (i, j, 0)),
        out_shape=jax.ShapeDtypeStruct((b, t, GMLP_WIDTH), BF16),
        compiler_params=_params("arbitrary", "arbitrary"),
        name="gmlp_prompt",
    )(ugv, lng, lnb, ws, bst)


def _gmlp_step_body(ugv_ref, lng_ref, lnb_ref, w00_ref, b0_ref, o_ref, vn_ref):
    u = _gelu_tanh(ugv_ref[:, 0:GMLP_WIDTH])
    v = _gelu_tanh(ugv_ref[:, GMLP_WIDTH:])
    vn = _layer_norm(v, lng_ref[...], lnb_ref[...])
    vn_ref[...] = vn
    o_ref[...] = (u * (w00_ref[...] * vn + b0_ref[...])).astype(o_ref.dtype)


def _gmlp_step(ugv, lng, lnb, w00, b0, l):
    b = ugv.shape[0]
    full = lambda w: pl.BlockSpec((b, w), lambda i: (0, 0))
    vec = _pick((1, GMLP_WIDTH), l)
    return pl.pallas_call(
        _gmlp_step_body,
        grid=(1,),
        in_specs=[full(2 * GMLP_WIDTH), vec, vec, vec, vec],
        out_specs=[full(GMLP_WIDTH), full(GMLP_WIDTH)],
        out_shape=[jax.ShapeDtypeStruct((b, GMLP_WIDTH), BF16), jax.ShapeDtypeStruct((b, GMLP_WIDTH), F32)],
        compiler_params=_params("arbitrary"),
        name="gmlp_step",
    )(ugv, lng, lnb, w00, b0)


def _merge_body(x_ref, hp_ref, hm_ref, hg_ref, g_ref, wgt_ref, wbp_ref, wbm_ref, wbg_ref, wout_ref, o_ref):
    x = x_ref[...]
    xn = _rms(x, g_ref[0:1, :]).astype(BF16)
    merged = None
    for i, (h_ref, w_ref) in enumerate(((hp_ref, wbp_ref), (hm_ref, wbm_ref), (hg_ref, wbg_ref))):
        gate = _sigmoid(_dot_nt(xn, wgt_ref[i * D_MODEL:(i + 1) * D_MODEL, :]))
        term = gate * _dot(h_ref[...], w_ref[...])
        merged = term if merged is None else merged + term
    y = _dot(merged.astype(BF16), wout_ref[...])
    o_ref[...] = x + _rms(y, g_ref[1:2, :])


def _merge(x, hp, hm, hg, norm_g, wgt, wbp, wbm, wbg, wout, l, tm):
    n = x.shape[0]
    row = lambda w: pl.BlockSpec((tm, w), lambda i: (i, 0))
    return pl.pallas_call(
        _merge_body,
        grid=(n // tm,),
        in_specs=[row(D_MODEL), row(POOL_WIDTH), row(MLSTM_WIDTH), row(GMLP_WIDTH), _pick((2, D_MODEL), l, 1),
                  _pick((3 * D_MODEL, D_MODEL), l), _pick((POOL_WIDTH, D_MODEL), l),
                  _pick((MLSTM_WIDTH, D_MODEL), l), _pick((GMLP_WIDTH, D_MODEL), l), _pick((D_MODEL, D_MODEL), l)],
        out_specs=row(D_MODEL),
        out_shape=jax.ShapeDtypeStruct((n, D_MODEL), F32),
        compiler_params=_params("arbitrary"),
        name="merge",
    )(x, hp, hm, hg, norm_g, wgt, wbp, wbm, wbg, wout)


def _memkv_body(mem_ref, g_ref, wk_ref, wv_ref, k_ref, v_ref, kb_ref, vb_ref):
    mn = _rms(mem_ref[...], g_ref[...]).astype(BF16)
    k = _dot(mn, wk_ref[...])
    v = _dot(mn, wv_ref[...])
    k_ref[...] = k
    v_ref[...] = v
    kb_ref[...] = k.astype(BF16)
    vb_ref[...] = v.astype(BF16)


def _memkv(mem, g, wk, wv, l, tm):
    n = mem.shape[0]
    row = pl.BlockSpec((tm, D_MODEL), lambda i: (i, 0))
    return pl.pallas_call(
        _memkv_body,
        grid=(n // tm,),
        in_specs=[row, _pick((1, D_MODEL), l), _pick((D_MODEL, D_MODEL), l), _pick((D_MODEL, D_MODEL), l)],
        out_specs=[row, row, row, row],
        out_shape=[jax.ShapeDtypeStruct((n, D_MODEL), F32), jax.ShapeDtypeStruct((n, D_MODEL), F32),
                   jax.ShapeDtypeStruct((n, D_MODEL), BF16), jax.ShapeDtypeStruct((n, D_MODEL), BF16)],
        compiler_params=_params("arbitrary"),
        name="memkv",
    )(mem, g, wk, wv)


def _attn_prompt_body(x_ref, k_ref, v_ref, g_ref, wq_ref, wo_ref, o_ref):
    x = x_ref[0]
    xn = _rms(x, g_ref[0:1, :]).astype(BF16)
    q = _dot(xn, wq_ref[...]).astype(BF16)
    outs = []
    for h in range(MEM_HEADS):
        hs = slice(h * MEM_HDIM, (h + 1) * MEM_HDIM)
        s = _dot_nt(q[:, hs], k_ref[0, :, hs]) * (MEM_HDIM ** -0.5)
        e = jnp.exp(s - jnp.max(s, axis=-1, keepdims=True))
        p = e / jnp.sum(e, axis=-1, keepdims=True)
        outs.append(_dot(p.astype(BF16), v_ref[0, :, hs]).astype(BF16))
    o = jnp.concatenate(outs, axis=-1)
    y = _dot(o, wo_ref[...])
    o_ref[0] = x + _rms(y, g_ref[1:2, :])


def _attn_prompt(x, kb, vb, norm_g, wq, wo, l, tq):
    b, t, _ = x.shape
    xs = pl.BlockSpec((1, tq, D_MODEL), lambda i, j: (i, j, 0))
    ms = pl.BlockSpec((1, MEM_TOKENS, D_MODEL), lambda i, j: (i, 0, 0))
    return pl.pallas_call(
        _attn_prompt_body,
        grid=(b, t // tq),
        in_specs=[xs, ms, ms, _pick((2, D_MODEL), l, 2), _pick((D_MODEL, D_MODEL), l), _pick((D_MODEL, D_MODEL), l)],
        out_specs=xs,
        out_shape=jax.ShapeDtypeStruct((b, t, D_MODEL), F32),
        compiler_params=_params("arbitrary", "arbitrary"),
        name="attn_prompt",
    )(x, kb, vb, norm_g, wq, wo)


def _qproj_body(x_ref, g_ref, wq_ref, q_ref):
    xn = _rms(x_ref[...], g_ref[0:1, :]).astype(BF16)
    q_ref[...] = _dot(xn, wq_ref[...])


def _qproj(x, norm_g, wq, l):
    n = x.shape[0]
    full = pl.BlockSpec((n, D_MODEL), lambda i: (0, 0))
    return pl.pallas_call(
        _qproj_body,
        grid=(1,),
        in_specs=[full, _pick((2, D_MODEL), l, 2), _pick((D_MODEL, D_MODEL), l)],
        out_specs=full,
        out_shape=jax.ShapeDtypeStruct((n, D_MODEL), F32),
        compiler_params=_params("arbitrary"),
        name="qproj",
    )(x, norm_g, wq)


def _attn_step_body(q_ref, k_ref, v_ref, o_ref):
    k = k_ref[...]
    s = jnp.sum(k * q_ref[...], axis=-1, keepdims=True) * (MEM_HDIM ** -0.5)
    e = jnp.exp(s - jnp.max(s, axis=1, keepdims=True))
    p = e / jnp.sum(e, axis=1, keepdims=True)
    o_ref[...] = jnp.sum(p * v_ref[...], axis=1, keepdims=True)


def _attn_step(q, k_all, v_all, l, bb):
    b = q.shape[0]
    qs = pl.BlockSpec((bb, 1, MEM_HEADS, MEM_HDIM), lambda i: (i, 0, 0, 0))
    ms = pl.BlockSpec((None, bb, MEM_TOKENS, MEM_HEADS, MEM_HDIM), lambda i: (l, i, 0, 0, 0))
    return pl.pallas_call(
        _attn_step_body,
        grid=(b // bb,),
        in_specs=[qs, ms, ms],
        out_specs=qs,
        out_shape=jax.ShapeDtypeStruct((b, 1, MEM_HEADS, MEM_HDIM), F32),
        compiler_params=_params("arbitrary"),
        name="attn_step",
    )(q, k_all, v_all)


def _oproj_body(x_ref, a_ref, g_ref, wo_ref, out_ref):
    y = _dot(a_ref[...].astype(BF16), wo_ref[...])
    out_ref[...] = x_ref[...] + _rms(y, g_ref[1:2, :])


def _oproj(x, a, norm_g, wo, l):
    n = x.shape[0]
    full = pl.BlockSpec((n, D_MODEL), lambda i: (0, 0))
    return pl.pallas_call(
        _oproj_body,
        grid=(1,),
        in_specs=[full, full, _pick((2, D_MODEL), l, 2), _pick((D_MODEL, D_MODEL), l)],
        out_specs=full,
        out_shape=jax.ShapeDtypeStruct((n, D_MODEL), F32),
        compiler_params=_params("arbitrary"),
        name="oproj",
    )(x, a, norm_g, wo)


def _prep_weights(w_ff_in, w_ff_out, w_in, b_igate, b_fgate, w_pool, s_pool, g_mlstm, gmlp_ln_g, gmlp_ln_b,
                  w_s, b_s, w_br_pool, w_br_mlstm, w_br_gmlp, w_out, g_mem, w_mq, w_mk, w_mv, w_mo):
    p = {}
    p["ffn_in"] = w_ff_in.astype(BF16)
    p["ffn_out"] = w_ff_out.astype(BF16)
    wt = jnp.swapaxes(w_in, 1, 2)
    w_if = jnp.pad(wt[:, OFF_IF:OFF_U], ((0, 0), (0, 128 - 2 * HEADS), (0, 0)))
    p["w1t"] = jnp.concatenate([wt[:, :OFF_IF], wt[:, OFF_U:OFF_GATE], w_if], axis=1).astype(BF16)
    p["wgt"] = wt[:, OFF_GATE:].astype(BF16)
    bif = jnp.concatenate([b_igate, b_fgate], axis=1)
    p["bif"] = jnp.pad(bif, ((0, 0), (0, 128 - 2 * HEADS))).reshape(DEPTH, 1, 128)
    same_group = jnp.eye(len(POOL_WINDOWS), dtype=bool)[None, :, None, :, None]
    wbd = jnp.where(same_group, w_pool[:, :, :, None, :], 0.0)
    p["wbd"] = wbd.reshape(DEPTH, POOL_WIDTH, POOL_WIDTH).astype(BF16)
    p["sp"] = s_pool.reshape(DEPTH, 1, POOL_WIDTH)
    p["gm"] = g_mlstm.reshape(DEPTH, 1, MLSTM_WIDTH)
    p["lng"] = gmlp_ln_g.reshape(DEPTH, 1, GMLP_WIDTH)
    p["lnb"] = gmlp_ln_b.reshape(DEPTH, 1, GMLP_WIDTH)
    p["ws"] = w_s.astype(BF16)
    p["bst"] = jnp.swapaxes(b_s, 1, 2)
    p["w00"] = jnp.repeat(w_s[:, :, 0, 0], GMLP_GDIM, axis=1).reshape(DEPTH, 1, GMLP_WIDTH)
    p["b0"] = jnp.repeat(b_s[:, :, 0], GMLP_GDIM, axis=1).reshape(DEPTH, 1, GMLP_WIDTH)
    p["wbp"] = w_br_pool.astype(BF16)
    p["wbm"] = w_br_mlstm.astype(BF16)
    p["wbg"] = w_br_gmlp.astype(BF16)
    p["wout"] = w_out.astype(BF16)
    p["gmem"] = g_mem.reshape(DEPTH, 1, D_MODEL)
    p["wq"] = w_mq.astype(BF16)
    p["wk"] = w_mk.astype(BF16)
    p["wv"] = w_mv.astype(BF16)
    p["wo"] = w_mo.astype(BF16)
    return p


def _prompt_layer(x, mem2d, norm_g, p, l, batch, seq):
    tm = 512
    k32, v32, kb, vb = _memkv(mem2d, p["gmem"], p["wk"], p["wv"], l, tm)
    x = _ffn(x, norm_g, p["ffn_in"], p["ffn_out"], l, 0, 0, tm)
    zp, qkv, zo, ugv, zif = _inproj(x, norm_g, p["w1t"], l, tm, BF16)
    b3 = lambda a: a.reshape(batch, seq, a.shape[-1])
    hp = _pool_prompt(b3(zp), p["wbd"], p["sp"], l, 512)
    hm, c_new, n_new, m_new = _mlstm_chunks(b3(qkv), b3(zif), b3(zo), p["bif"], p["gm"], l, 4)
    hg = _gmlp_prompt(b3(ugv), p["lng"], p["lnb"], p["ws"], p["bst"], l, 512)
    f2 = lambda a: a.reshape(batch * seq, a.shape[-1])
    x = _merge(x, f2(hp), f2(hm), f2(hg), norm_g, p["wgt"], p["wbp"], p["wbm"], p["wbg"], p["wout"], l, tm)
    x = _attn_prompt(b3(x), kb.reshape(batch, MEM_TOKENS, D_MODEL), vb.reshape(batch, MEM_TOKENS, D_MODEL),
                     norm_g, p["wq"], p["wo"], l, 512).reshape(batch * seq, D_MODEL)
    x = _ffn(x, norm_g, p["ffn_in"], p["ffn_out"], l, 3, 1, tm)
    pool_buf = b3(zp)[:, seq - POOL_BUF:, :]
    mk = k32.reshape(batch, MEM_TOKENS, MEM_HEADS, MEM_HDIM)
    mv = v32.reshape(batch, MEM_TOKENS, MEM_HEADS, MEM_HDIM)
    return x, pool_buf, c_new, n_new, m_new[:, 0, :HEADS], mk, mv


def _sample_layer(x, norm_g, p, l, pool_t, c_all, n_all, m_all, c_prev, k_all, v_all):
    batch = x.shape[0]
    tm = batch
    x = _ffn(x, norm_g, p["ffn_in"], p["ffn_out"], l, 0, 0, tm)
    zp, qkv, zo, ugv, zif = _inproj(x, norm_g, p["w1t"], l, tm, F32)
    hp, new_buf = _pool_step(zp, pool_t, p["wbd"], p["sp"], l)
    r3 = lambda a: a.reshape(batch, 1, a.shape[-1])
    hm, c_out, n_new, m_new = _mlstm_step(r3(qkv), r3(zif), r3(zo), p["bif"], p["gm"],
                                          c_all, n_all, m_all, c_prev, l, 16)
    hg, vn = _gmlp_step(ugv, p["lng"], p["lnb"], p["w00"], p["b0"], l)
    x = _merge(x, hp, hm.reshape(batch, MLSTM_WIDTH), hg, norm_g, p["wgt"], p["wbp"], p["wbm"], p["wbg"],
               p["wout"], l, tm)
    q = _qproj(x, norm_g, p["wq"], l)
    a = _attn_step(q.reshape(batch, 1, MEM_HEADS, MEM_HDIM), k_all, v_all, l, 4)
    x = _oproj(x, a.reshape(batch, D_MODEL), norm_g, p["wo"], l)
    x = _ffn(x, norm_g, p["ffn_in"], p["ffn_out"], l, 3, 1, tm)
    return x, new_buf, c_out, n_new, m_new.reshape(batch, HEADS), vn.reshape(batch, 1, GMLP_WIDTH)


def kernel(x_prompt, x_sample, state_pool, state_mlstm_C, state_mlstm_n, state_mlstm_m, cache_mem_k, cache_mem_v, mem_prompt, norm_g, w_ff_in, w_ff_out, w_in, b_igate, b_fgate, w_pool, s_pool, g_mlstm, gmlp_ln_g, gmlp_ln_b, w_s, b_s, w_br_pool, w_br_mlstm, w_br_gmlp, w_out, g_mem, w_mq, w_mk, w_mv, w_mo):
    batch, seq, _ = x_prompt.shape
    dec_batch = x_sample.shape[0]
    assert x_sample.shape[1] == 1 and seq % CHUNK == 0 and PAST_LEN % CHUNK == 0 and PAST_LEN >= max(POOL_WINDOWS)
    p = _prep_weights(w_ff_in, w_ff_out, w_in, b_igate, b_fgate, w_pool, s_pool, g_mlstm, gmlp_ln_g, gmlp_ln_b,
                      w_s, b_s, w_br_pool, w_br_mlstm, w_br_gmlp, w_out, g_mem, w_mq, w_mk, w_mv, w_mo)
    xp = x_prompt.reshape(batch * seq, D_MODEL)
    xs = x_sample.reshape(dec_batch, D_MODEL)
    mem2d = mem_prompt.reshape(batch * MEM_TOKENS, D_MODEL)
    pool_t = jnp.swapaxes(state_pool, 1, 2)
    m_all = state_mlstm_m.reshape(DEPTH, dec_batch, 1, HEADS)
    outs = [[] for _ in range(10)]
    c_samp = None
    for l in range(DEPTH):
        xp, pb, c, n, m, mk, mv = _prompt_layer(xp, mem2d, norm_g, p, l, batch, seq)
        xs, pbs, c_samp, ns, ms, vr = _sample_layer(xs, norm_g, p, l, pool_t, state_mlstm_C, state_mlstm_n, m_all,
                                                    c_samp, cache_mem_k, cache_mem_v)
        for lst, val in zip(outs, (pb, pbs, c, n, m, ns, ms, mk, mv, vr)):
            lst.append(val)
    pb, pbs, c, n, m, ns, ms, mk, mv, vr = [jnp.stack(o) for o in outs]
    return (xp.reshape(batch, seq, D_MODEL), xs.reshape(dec_batch, 1, D_MODEL),
            pb, jnp.swapaxes(pbs, 1, 2), c, n, m, c_samp, ns, ms, mk, mv, vr)
```

```python
import functools

import jax
import jax.numpy as jnp
from jax import lax
from jax.experimental import pallas as pl
from jax.experimental.pallas import tpu as pltpu

F32 = jnp.float32
BF16 = jnp.bfloat16

D_MODEL = 1024
DEPTH = 4
PAST_LEN = 16384
POOL_WINDOWS = (2, 4, 8, 16)
POOL_WIDTH = 256
POOL_GDIM = 64
POOL_BUF = 15
HEADS = 4
MLSTM_WIDTH = 512
HDIM = 128
CHUNK = 128
GMLP_WIDTH = 256
GMLP_GROUPS = 4
GMLP_GDIM = 64
MEM_TOKENS = 256
MEM_HEADS = 4
MEM_HDIM = 256
D_FF = 2816
FF_CHUNK = 256
N_FF_CHUNKS = D_FF // FF_CHUNK
RMS_EPS = 1e-6
LN_EPS = 1e-5
OFF_Q = POOL_WIDTH
OFF_IF = OFF_Q + 4 * MLSTM_WIDTH
OFF_U = OFF_IF + 2 * HEADS
OFF_GATE = OFF_U + 2 * GMLP_WIDTH
W1_QKV = POOL_WIDTH
W1_O = W1_QKV + 3 * MLSTM_WIDTH
W1_UGV = W1_O + MLSTM_WIDTH
W1_IF = W1_UGV + 2 * GMLP_WIDTH
W1_ROWS = W1_IF + 128

VMEM_LIMIT_BYTES = 56 * 1024 * 1024


def _params(*sem):
    return pltpu.CompilerParams(dimension_semantics=sem, vmem_limit_bytes=VMEM_LIMIT_BYTES)


def _pick(tail, *idx):
    nt = len(tail)
    return pl.BlockSpec((None,) * len(idx) + tuple(tail), lambda *_: tuple(idx) + (0,) * nt,
                        pipeline_mode=pl.Buffered(1))


def _rms(x, g):
    return x * lax.rsqrt(jnp.mean(x * x, axis=-1, keepdims=True) + RMS_EPS) * g


def _dot(a, b):
    return jnp.dot(a, b, preferred_element_type=F32)


def _dot_nt(a, b):
    return lax.dot_general(a, b, (((1,), (1,)), ((), ())), preferred_element_type=F32)


def _dot_tn(a, b):
    return lax.dot_general(a, b, (((0,), (0,)), ((), ())), preferred_element_type=F32)


def _sigmoid(x):
    return 1.0 / (1.0 + jnp.exp(-x))


def _gelu_tanh(x):
    return 0.5 * x * (1.0 + jnp.tanh(0.7978845608028654 * (x + 0.044715 * (x * x * x))))


def _log_sigmoid(x):
    return -(jnp.maximum(-x, 0.0) + jnp.log(1.0 + jnp.exp(-jnp.abs(x))))


def _ffn_body(x_ref, g_ref, wi_ref, wo_ref, o_ref, acc_ref):
    x = x_ref[...]
    xn = _rms(x, g_ref[0:1, :]).astype(BF16)
    for c in range(N_FF_CHUNKS):
        cols = slice(c * FF_CHUNK, (c + 1) * FF_CHUNK)
        gate = _dot(xn, wi_ref[:, cols])
        up = _dot(xn, wi_ref[:, D_FF + c * FF_CHUNK:D_FF + (c + 1) * FF_CHUNK])
        act = (gate * _sigmoid(gate) * up).astype(BF16)
        part = _dot(act, wo_ref[cols, :])
        if c == 0:
            acc_ref[...] = part
        else:
            acc_ref[...] += part
    o_ref[...] = x + 0.5 * _rms(acc_ref[...], g_ref[1:2, :])


def _ffn(x, norm_g, w_in, w_out, l, sub, half, tm):
    n = x.shape[0]
    return pl.pallas_call(
        _ffn_body,
        grid=(n // tm,),
        in_specs=[
            pl.BlockSpec((tm, D_MODEL), lambda i: (i, 0)),
            _pick((2, D_MODEL), l, sub),
            _pick((D_MODEL, 2 * D_FF), l, half),
            _pick((D_FF, D_MODEL), l, half),
        ],
        out_specs=pl.BlockSpec((tm, D_MODEL), lambda i: (i, 0)),
        out_shape=jax.ShapeDtypeStruct((n, D_MODEL), F32),
        scratch_shapes=[pltpu.VMEM((tm, D_MODEL), F32)],
        compiler_params=_params("arbitrary"),
        name="ffn",
    )(x, norm_g, w_in, w_out)


def _inproj_body(x_ref, g_ref, w_ref, zp_ref, qkv_ref, zo_ref, ugv_ref, zif_ref):
    xn = _rms(x_ref[...], g_ref[0:1, :]).astype(BF16)
    z = _dot_nt(xn, w_ref[...])
    zp_ref[...] = z[:, :W1_QKV]
    q = z[:, W1_QKV:W1_QKV + MLSTM_WIDTH]
    k = z[:, W1_QKV + MLSTM_WIDTH:W1_QKV + 2 * MLSTM_WIDTH] * (HDIM ** -0.5)
    v = z[:, W1_QKV + 2 * MLSTM_WIDTH:W1_O]
    qkv_ref[:, 0:MLSTM_WIDTH] = q.astype(qkv_ref.dtype)
    qkv_ref[:, MLSTM_WIDTH:2 * MLSTM_WIDTH] = k.astype(qkv_ref.dtype)
    qkv_ref[:, 2 * MLSTM_WIDTH:] = v.astype(qkv_ref.dtype)
    zo_ref[...] = z[:, W1_O:W1_UGV]
    ugv_ref[...] = z[:, W1_UGV:W1_IF]
    zif_ref[...] = z[:, W1_IF:]


def _inproj(x, norm_g, w1t, l, tm, qkv_dtype):
    n = x.shape[0]
    row = lambda w: pl.BlockSpec((tm, w), lambda i: (i, 0))
    return pl.pallas_call(
        _inproj_body,
        grid=(n // tm,),
        in_specs=[row(D_MODEL), _pick((2, D_MODEL), l, 1), _pick((W1_ROWS, D_MODEL), l)],
        out_specs=[row(POOL_WIDTH), row(3 * MLSTM_WIDTH), row(MLSTM_WIDTH), row(2 * GMLP_WIDTH), row(128)],
        out_shape=[
            jax.ShapeDtypeStruct((n, POOL_WIDTH), F32),
            jax.ShapeDtypeStruct((n, 3 * MLSTM_WIDTH), qkv_dtype),
            jax.ShapeDtypeStruct((n, MLSTM_WIDTH), F32),
            jax.ShapeDtypeStruct((n, 2 * GMLP_WIDTH), F32),
            jax.ShapeDtypeStruct((n, 128), F32),
        ],
        compiler_params=_params("arbitrary"),
        name="inproj",
    )(x, norm_g, w1t)


def _pool_select(sums, u, count_of):
    lane = lax.broadcasted_iota(jnp.int32, u.shape, u.ndim - 1)
    mean = None
    for gi, w in enumerate(POOL_WINDOWS):
        m = sums[w] * (1.0 / count_of(w))
        mean = m if mean is None else jnp.where(lane >= gi * POOL_GDIM, m, mean)
    return mean - u


POOL_HALO = 2 * max(POOL_WINDOWS)


def _pool_prompt_body(zp_ref, wbd_ref, sp_ref, o_ref, full_ref, p2_ref, p4_ref, p8_ref, *, tt):
    t = pl.program_id(1)
    hl = POOL_HALO
    n = hl + tt

    @pl.when(t == 0)
    def _():
        full_ref[0:hl, :] = jnp.zeros((hl, POOL_WIDTH), F32)

    @pl.when(t > 0)
    def _():
        full_ref[0:hl, :] = full_ref[tt:n, :]

    u = zp_ref[0]
    full_ref[hl:n, :] = u
    p2_ref[8:n, :] = full_ref[8:n, :] + full_ref[7:n - 1, :]
    p4_ref[16:n, :] = p2_ref[16:n, :] + p2_ref[14:n - 2, :]
    p8_ref[24:n, :] = p4_ref[24:n, :] + p4_ref[20:n - 4, :]
    sums = {2: p2_ref[hl:n, :], 4: p4_ref[hl:n, :], 8: p8_ref[hl:n, :],
            16: p8_ref[hl:n, :] + p8_ref[hl - 8:n - 8, :]}
    pos = t * tt + lax.broadcasted_iota(jnp.int32, (tt, 1), 0)
    diff = _pool_select(sums, u, lambda w: jnp.minimum(pos + 1, w).astype(F32))
    y = _dot(diff.astype(BF16), wbd_ref[...]) * sp_ref[...]
    o_ref[0] = y.astype(o_ref.dtype)


def _pool_prompt(zp, wbd, sp, l, tt):
    b, t, _ = zp.shape
    return pl.pallas_call(
        functools.partial(_pool_prompt_body, tt=tt),
        grid=(b, t // tt),
        in_specs=[
            pl.BlockSpec((1, tt, POOL_WIDTH), lambda i, j: (i, j, 0)),
            _pick((POOL_WIDTH, POOL_WIDTH), l),
            _pick((1, POOL_WIDTH), l),
        ],
        out_specs=pl.BlockSpec((1, tt, POOL_WIDTH), lambda i, j: (i, j, 0)),
        out_shape=jax.ShapeDtypeStruct((b, t, POOL_WIDTH), BF16),
        scratch_shapes=[pltpu.VMEM((tt + POOL_HALO, POOL_WIDTH), F32)] * 4,
        compiler_params=_params("arbitrary", "arbitrary"),
        name="pool_prompt",
    )(zp, wbd, sp)


def _pool_step_body(zp_ref, buf_ref, wbd_ref, sp_ref, o_ref, nbuf_ref):
    u = zp_ref[...]
    acc = u
    sums = {}
    for j in range(1, max(POOL_WINDOWS)):
        acc = acc + buf_ref[POOL_BUF - j]
        if j + 1 in POOL_WINDOWS:
            sums[j + 1] = acc
    diff = _pool_select(sums, u, lambda w: float(w))
    y = _dot(diff.astype(BF16), wbd_ref[...]) * sp_ref[...]
    o_ref[...] = y.astype(o_ref.dtype)
    nbuf_ref[0:POOL_BUF - 1] = buf_ref[1:POOL_BUF]
    nbuf_ref[POOL_BUF - 1] = u


def _pool_step(zp, pool_t, wbd, sp, l):
    b = zp.shape[0]
    full = lambda shape: pl.BlockSpec(shape, lambda i: (0,) * len(shape))
    return pl.pallas_call(
        _pool_step_body,
        grid=(1,),
        in_specs=[full((b, POOL_WIDTH)), _pick((POOL_BUF, b, POOL_WIDTH), l),
                  _pick((POOL_WIDTH, POOL_WIDTH), l), _pick((1, POOL_WIDTH), l)],
        out_specs=[full((b, POOL_WIDTH)), full((POOL_BUF, b, POOL_WIDTH))],
        out_shape=[jax.ShapeDtypeStruct((b, POOL_WIDTH), BF16),
                   jax.ShapeDtypeStruct((POOL_BUF, b, POOL_WIDTH), F32)],
        compiler_params=_params("arbitrary"),
        name="pool_step",
    )(zp, pool_t, wbd, sp)


def _bdot(a, b, ca, cb):
    return lax.dot_general(a, b, (((ca,), (cb,)), ((0,), (0,))), preferred_element_type=F32)


def _mlstm_chunk_body(qkv_ref, zif_ref, zo_ref, bif_ref, gm_ref, h_ref, c_ref, n_ref, m_ref, *, bb):
    @pl.when(pl.program_id(1) == 0)
    def _():
        c_ref[...] = jnp.zeros(c_ref.shape, F32)
        n_ref[...] = jnp.zeros(n_ref.shape, F32)
        m_ref[...] = jnp.zeros(m_ref.shape, F32)

    L = CHUNK
    G = bb * HEADS
    row = lax.broadcasted_iota(jnp.int32, (L, L), 0)
    col = lax.broadcasted_iota(jnp.int32, (L, L), 1)
    causal = col <= row
    tri = jnp.where(causal, 1.0, 0.0).astype(BF16)
    lane = lax.broadcasted_iota(jnp.int32, (L, 128), 1)
    is_forget = (lane >= HEADS) & (lane < 2 * HEADS)
    gls = []
    for bi in range(bb):
        g = zif_ref[bi] + bif_ref[...]
        gls.append(jnp.where(is_forget, _log_sigmoid(g), g))
    gl_all = jnp.concatenate(gls, axis=1)
    hi = gl_all.astype(BF16)
    r1 = gl_all - hi.astype(F32)
    mid = r1.astype(BF16)
    lo = (r1 - mid.astype(F32)).astype(BF16)
    cs_all = _dot(tri, hi) + _dot(tri, mid) + _dot(tri, lo)
    ig_col, b_col, ig_row, b_row, qs, ks, vs, zos = [], [], [], [], [], [], [], []
    for bi in range(bb):
        gl = gls[bi]
        cs = cs_all[:, bi * 128:(bi + 1) * 128]
        gl_t = gl.T
        cs_t = cs.T
        for h in range(HEADS):
            hs = slice(h * HDIM, (h + 1) * HDIM)
            ig_col.append(gl[:, h:h + 1])
            b_col.append(cs[:, HEADS + h:HEADS + h + 1])
            ig_row.append(gl_t[h:h + 1, :])
            b_row.append(cs_t[HEADS + h:HEADS + h + 1, :])
            qs.append(qkv_ref[bi, :, hs])
            ks.append(qkv_ref[bi, :, MLSTM_WIDTH + h * HDIM:MLSTM_WIDTH + (h + 1) * HDIM])
            vs.append(qkv_ref[bi, :, 2 * MLSTM_WIDTH + h * HDIM:2 * MLSTM_WIDTH + (h + 1) * HDIM])
            zos.append(zo_ref[bi, :, hs])
    ig_col, b_col, ig_row, b_row = (jnp.stack(a) for a in (ig_col, b_col, ig_row, b_row))
    q, k, v, zo = (jnp.stack(a) for a in (qs, ks, vs, zos))
    gm = jnp.stack([gm_ref[:, h * HDIM:(h + 1) * HDIM] for h in range(HEADS)] * bb)
    c_old = c_ref[...].reshape(G, HDIM, HDIM)
    n_old = n_ref[...].reshape(G, 1, HDIM)
    m_prev = m_ref[...].reshape(G, 1, 1)

    dmat = jnp.where(causal[None], b_col - b_row + ig_row, -jnp.inf)
    inter = b_col + m_prev
    m_t = jnp.maximum(inter, jnp.max(dmat, axis=-1, keepdims=True))
    w_intra = jnp.exp(dmat - m_t)
    w_inter = jnp.exp(inter - m_t)
    s = _bdot(q, k, 2, 2) * w_intra
    num = _bdot(s.astype(BF16), v, 2, 1) + w_inter * _bdot(q, c_old.astype(BF16), 2, 2)
    qn = jnp.sum(q.astype(F32) * n_old, axis=-1, keepdims=True)
    den = jnp.sum(s, axis=-1, keepdims=True) + w_inter * qn
    hh = num / jnp.maximum(jnp.abs(den), jnp.exp(-m_t))
    hh = hh * lax.rsqrt(jnp.mean(hh * hh, axis=-1, keepdims=True) + RMS_EPS)
    hh = (hh * gm * _sigmoid(zo)).astype(h_ref.dtype)
    for bi in range(bb):
        for h in range(HEADS):
            h_ref[bi, :, h * HDIM:(h + 1) * HDIM] = hh[bi * HEADS + h]
    m_new = m_t[:, L - 1:L, :]
    b_last = b_col[:, L - 1:L, :]
    w_state = jnp.exp(b_last - b_col + ig_col - m_new)
    decay = jnp.exp(b_last + m_prev - m_new)
    vw = (v.astype(F32) * w_state).astype(BF16)
    c_new = decay * c_old + _bdot(vw, k, 1, 1)
    n_new = decay * n_old + jnp.sum(k.astype(F32) * w_state, axis=1, keepdims=True)
    c_ref[...] = c_new.reshape(c_ref.shape)
    n_ref[...] = n_new.reshape(n_ref.shape)
    m_ref[...] = m_new.reshape(m_ref.shape)


def _mlstm_chunks(qkv, zif, zo, bif, gm, l, bb):
    b, t, _ = qkv.shape
    blk = lambda w: pl.BlockSpec((bb, CHUNK, w), lambda i, j: (i, j, 0))
    state = lambda r, c: pl.BlockSpec((bb, HEADS, r, c), lambda i, j: (i, 0, 0, 0))
    return pl.pallas_call(
        functools.partial(_mlstm_chunk_body, bb=bb),
        grid=(b // bb, t // CHUNK),
        in_specs=[blk(3 * MLSTM_WIDTH), blk(128), blk(MLSTM_WIDTH), _pick((1, 128), l), _pick((1, MLSTM_WIDTH), l)],
        out_specs=[blk(MLSTM_WIDTH), state(HDIM, HDIM), state(1, HDIM), state(1, 1)],
        out_shape=[
            jax.ShapeDtypeStruct((b, t, MLSTM_WIDTH), BF16),
            jax.ShapeDtypeStruct((b, HEADS, HDIM, HDIM), F32),
            jax.ShapeDtypeStruct((b, HEADS, 1, HDIM), F32),
            jax.ShapeDtypeStruct((b, HEADS, 1, 1), F32),
        ],
        compiler_params=_params("arbitrary", "arbitrary"),
        name="mlstm_chunks",
    )(qkv, zif, zo, bif, gm)


def _mlstm_step_body(*refs, aliased):
    qkv_ref, zif_ref, zo_ref, bif_ref, gm_ref, c0_ref, n0_ref, m0_ref = refs[:8]
    h_ref, c_ref, n_ref, m_ref = refs[9:] if aliased else refs[8:]
    g = zif_ref[...] + bif_ref[...]
    r = lax.broadcasted_iota(jnp.int32, (HDIM, HDIM), 0)
    c = lax.broadcasted_iota(jnp.int32, (HDIM, HDIM), 1)
    eye = (r == c)[None]
    for h in range(HEADS):
        hs = slice(h * HDIM, (h + 1) * HDIM)
        ig = g[:, :, h:h + 1]
        lf = _log_sigmoid(g[:, :, HEADS + h:HEADS + h + 1])
        m_prev = m0_ref[:, :, h:h + 1]
        q = qkv_ref[:, :, hs]
        k = qkv_ref[:, :, MLSTM_WIDTH + h * HDIM:MLSTM_WIDTH + (h + 1) * HDIM]
        v = qkv_ref[:, :, 2 * MLSTM_WIDTH + h * HDIM:2 * MLSTM_WIDTH + (h + 1) * HDIM]
        c_h = c0_ref[:, h]
        n_h = n0_ref[:, h:h + 1, :]
        inter = lf + m_prev
        m_t = jnp.maximum(inter, ig)
        w_intra = jnp.exp(ig - m_t)
        w_inter = jnp.exp(inter - m_t)
        s = jnp.sum(q * k, axis=-1, keepdims=True) * w_intra
        cq = jnp.sum(c_h * q, axis=-1, keepdims=True)
        v_col = jnp.sum(jnp.where(eye, v, 0.0), axis=-1, keepdims=True)
        num = s * v_col + w_inter * cq
        den = s + w_inter * jnp.sum(n_h * q, axis=-1, keepdims=True)
        hcol = num / jnp.maximum(jnp.abs(den), jnp.exp(-m_t))
        hcol = hcol * lax.rsqrt(jnp.mean(hcol * hcol, axis=1, keepdims=True) + RMS_EPS)
        hrow = jnp.sum(jnp.where(eye, hcol, 0.0), axis=1, keepdims=True)
        hrow = hrow * gm_ref[:, hs] * _sigmoid(zo_ref[:, :, hs])
        h_ref[:, :, hs] = hrow.astype(h_ref.dtype)
        w_state = jnp.exp(ig - m_t)
        decay = jnp.exp(lf + m_prev - m_t)
        c_ref[:, h] = decay * c_h + (w_state * v_col) * k
        n_ref[:, h:h + 1, :] = decay * n_h + w_state * k
        m_ref[:, :, h:h + 1] = m_t


def _mlstm_step(qkv, zif, zo, bif, gm, c_all, n_all, m_all, c_prev, l, bb):
    b = qkv.shape[0]
    blk = lambda w: pl.BlockSpec((bb, 1, w), lambda i: (i, 0, 0))
    st_c = pl.BlockSpec((None, bb, HEADS, HDIM, HDIM), lambda i: (l, i, 0, 0, 0))
    st_n = pl.BlockSpec((None, bb, HEADS, HDIM), lambda i: (l, i, 0, 0))
    st_m = pl.BlockSpec((None, bb, 1, HEADS), lambda i: (l, i, 0, 0))
    in_specs = [blk(3 * MLSTM_WIDTH), blk(128), blk(MLSTM_WIDTH), _pick((1, 128), l), _pick((1, MLSTM_WIDTH), l),
                st_c, st_n, st_m]
    args = [qkv, zif, zo, bif, gm, c_all, n_all, m_all]
    aliases = {}
    if c_prev is not None:
        in_specs.append(pl.BlockSpec(memory_space=pl.ANY))
        args.append(c_prev)
        aliases = {8: 1}
    return pl.pallas_call(
        functools.partial(_mlstm_step_body, aliased=c_prev is not None),
        grid=(b // bb,),
        in_specs=in_specs,
        out_specs=[blk(MLSTM_WIDTH), st_c, pl.BlockSpec((bb, HEADS, HDIM), lambda i: (i, 0, 0)),
                   pl.BlockSpec((bb, 1, HEADS), lambda i: (i, 0, 0))],
        out_shape=[
            jax.ShapeDtypeStruct((b, 1, MLSTM_WIDTH), BF16),
            jax.ShapeDtypeStruct(c_all.shape, F32),
            jax.ShapeDtypeStruct((b, HEADS, HDIM), F32),
            jax.ShapeDtypeStruct((b, 1, HEADS), F32),
        ],
        input_output_aliases=aliases,
        compiler_params=_params("arbitrary"),
        name="mlstm_step",
    )(*args)


def _layer_norm(v, g, b):
    vc = v - jnp.mean(v, axis=-1, keepdims=True)
    return vc * lax.rsqrt(jnp.mean(vc * vc, axis=-1, keepdims=True) + LN_EPS) * g + b


def _gmlp_prompt_body(ugv_ref, lng_ref, lnb_ref, ws_ref, bst_ref, o_ref, *, nchunk):
    row = lax.broadcasted_iota(jnp.int32, (CHUNK, CHUNK), 0)
    col = lax.broadcasted_iota(jnp.int32, (CHUNK, CHUNK), 1)
    causal = col <= row
    lane = lax.broadcasted_iota(jnp.int32, (CHUNK, GMLP_WIDTH), 1)
    ws = [jnp.where(causal, ws_ref[g], jnp.zeros((), ws_ref.dtype)) for g in range(GMLP_GROUPS)]
    for c in range(nchunk):
        rows = slice(c * CHUNK, (c + 1) * CHUNK)
        u = _gelu_tanh(ugv_ref[0, rows, 0:GMLP_WIDTH])
        v = _gelu_tanh(ugv_ref[0, rows, GMLP_WIDTH:])
        vn = _layer_norm(v, lng_ref[...], lnb_ref[...]).astype(BF16)
        mixed = None
        for g in range(GMLP_GROUPS):
            mg = _dot(ws[g], vn) + bst_ref[:, g:g + 1]
            mixed = mg if mixed is None else jnp.where(lane >= g * GMLP_GDIM, mg, mixed)
        o_ref[0, rows, :] = (u * mixed).astype(o_ref.dtype)


def _gmlp_prompt(ugv, lng, lnb, ws, bst, l, tt):
    b, t, _ = ugv.shape
    return pl.pallas_call(
        functools.partial(_gmlp_prompt_body, nchunk=tt // CHUNK),
        grid=(b, t // tt),
        in_specs=[pl.BlockSpec((1, tt, 2 * GMLP_WIDTH), lambda i, j: (i, j, 0)),
                  _pick((1, GMLP_WIDTH), l), _pick((1, GMLP_WIDTH), l),
                  _pick((GMLP_GROUPS, CHUNK, CHUNK), l), _pick((CHUNK, GMLP_GROUPS), l)],
        out_specs=pl.BlockSpec((1, tt, GMLP_WIDTH), lambda i, j: (i, j, 0)),
        out_shape=jax.ShapeDtypeStruct((b, t, GMLP_WIDTH), BF16),
        compiler_params=_params("arbitrary", "arbitrary"),
        name="gmlp_prompt",
    )(ugv, lng, lnb, ws, bst)


def _gmlp_step_body(ugv_ref, lng_ref, lnb_ref, w00_ref, b0_ref, o_ref, vn_ref):
    u = _gelu_tanh(ugv_ref[:, 0:GMLP_WIDTH])
    v = _gelu_tanh(ugv_ref[:, GMLP_WIDTH:])
    vn = _layer_norm(v, lng_ref[...], lnb_ref[...])
    vn_ref[...] = vn
    o_ref[...] = (u * (w00_ref[...] * vn + b0_ref[...])).astype(o_ref.dtype)


def _gmlp_step(ugv, lng, lnb, w00, b0, l):
    b = ugv.shape[0]
    full = lambda w: pl.BlockSpec((b, w), lambda i: (0, 0))
    vec = _pick((1, GMLP_WIDTH), l)
    return pl.pallas_call(
        _gmlp_step_body,
        grid=(1,),
        in_specs=[full(2 * GMLP_WIDTH), vec, vec, vec, vec],
        out_specs=[full(GMLP_WIDTH), full(GMLP_WIDTH)],
        out_shape=[jax.ShapeDtypeStruct((b, GMLP_WIDTH), BF16), jax.ShapeDtypeStruct((b, GMLP_WIDTH), F32)],
        compiler_params=_params("arbitrary"),
        name="gmlp_step",
    )(ugv, lng, lnb, w00, b0)


def _merge_body(x_ref, hp_ref, hm_ref, hg_ref, g_ref, wgt_ref, wbp_ref, wbm_ref, wbg_ref, wout_ref, o_ref):
    x = x_ref[...]
    xn = _rms(x, g_ref[0:1, :]).astype(BF16)
    merged = None
    for i, (h_ref, w_ref) in enumerate(((hp_ref, wbp_ref), (hm_ref, wbm_ref), (hg_ref, wbg_ref))):
        gate = _sigmoid(_dot_nt(xn, wgt_ref[i * D_MODEL:(i + 1) * D_MODEL, :]))
        term = gate * _dot(h_ref[...], w_ref[...])
        merged = term if merged is None else merged + term
    y = _dot(merged.astype(BF16), wout_ref[...])
    o_ref[...] = x + _rms(y, g_ref[1:2, :])


def _merge(x, hp, hm, hg, norm_g, wgt, wbp, wbm, wbg, wout, l, tm):
    n = x.shape[0]
    row = lambda w: pl.BlockSpec((tm, w), lambda i: (i, 0))
    return pl.pallas_call(
        _merge_body,
        grid=(n // tm,),
        in_specs=[row(D_MODEL), row(POOL_WIDTH), row(MLSTM_WIDTH), row(GMLP_WIDTH), _pick((2, D_MODEL), l, 1),
                  _pick((3 * D_MODEL, D_MODEL), l), _pick((POOL_WIDTH, D_MODEL), l),
                  _pick((MLSTM_WIDTH, D_MODEL), l), _pick((GMLP_WIDTH, D_MODEL), l), _pick((D_MODEL, D_MODEL), l)],
        out_specs=row(D_MODEL),
        out_shape=jax.ShapeDtypeStruct((n, D_MODEL), F32),
        compiler_params=_params("arbitrary"),
        name="merge",
    )(x, hp, hm, hg, norm_g, wgt, wbp, wbm, wbg, wout)


def _memkv_body(mem_ref, g_ref, wk_ref, wv_ref, k_ref, v_ref, kb_ref, vb_ref):
    mn = _rms(mem_ref[...], g_ref[...]).astype(BF16)
    k = _dot(mn, wk_ref[...])
    v = _dot(mn, wv_ref[...])
    k_ref[...] = k
    v_ref[...] = v
    kb_ref[...] = k.astype(BF16)
    vb_ref[...] = v.astype(BF16)


def _memkv(mem, g, wk, wv, l, tm):
    n = mem.shape[0]
    row = pl.BlockSpec((tm, D_MODEL), lambda i: (i, 0))
    return pl.pallas_call(
        _memkv_body,
        grid=(n // tm,),
        in_specs=[row, _pick((1, D_MODEL), l), _pick((D_MODEL, D_MODEL), l), _pick((D_MODEL, D_MODEL), l)],
        out_specs=[row, row, row, row],
        out_shape=[jax.ShapeDtypeStruct((n, D_MODEL), F32), jax.ShapeDtypeStruct((n, D_MODEL), F32),
                   jax.ShapeDtypeStruct((n, D_MODEL), BF16), jax.ShapeDtypeStruct((n, D_MODEL), BF16)],
        compiler_params=_params("arbitrary"),
        name="memkv",
    )(mem, g, wk, wv)


def _attn_prompt_body(x_ref, k_ref, v_ref, g_ref, wq_ref, wo_ref, o_ref):
    x = x_ref[0]
    xn = _rms(x, g_ref[0:1, :]).astype(BF16)
    q = _dot(xn, wq_ref[...]).astype(BF16)
    outs = []
    for h in range(MEM_HEADS):
        hs = slice(h * MEM_HDIM, (h + 1) * MEM_HDIM)
        s = _dot_nt(q[:, hs], k_ref[0, :, hs]) * (MEM_HDIM ** -0.5)
        e = jnp.exp(s - jnp.max(s, axis=-1, keepdims=True))
        p = e / jnp.sum(e, axis=-1, keepdims=True)
        outs.append(_dot(p.astype(BF16), v_ref[0, :, hs]).astype(BF16))
    o = jnp.concatenate(outs, axis=-1)
    y = _dot(o, wo_ref[...])
    o_ref[0] = x + _rms(y, g_ref[1:2, :])


def _attn_prompt(x, kb, vb, norm_g, wq, wo, l, tq):
    b, t, _ = x.shape
    xs = pl.BlockSpec((1, tq, D_MODEL), lambda i, j: (i, j, 0))
    ms = pl.BlockSpec((1, MEM_TOKENS, D_MODEL), lambda i, j: (i, 0, 0))
    return pl.pallas_call(
        _attn_prompt_body,
        grid=(b, t // tq),
        in_specs=[xs, ms, ms, _pick((2, D_MODEL), l, 2), _pick((D_MODEL, D_MODEL), l), _pick((D_MODEL, D_MODEL), l)],
        out_specs=xs,
        out_shape=jax.ShapeDtypeStruct((b, t, D_MODEL), F32),
        compiler_params=_params("arbitrary", "arbitrary"),
        name="attn_prompt",
    )(x, kb, vb, norm_g, wq, wo)


def _qproj_body(x_ref, g_ref, wq_ref, q_ref):
    xn = _rms(x_ref[...], g_ref[0:1, :]).astype(BF16)
    q_ref[...] = _dot(xn, wq_ref[...])


def _qproj(x, norm_g, wq, l):
    n = x.shape[0]
    full = pl.BlockSpec((n, D_MODEL), lambda i: (0, 0))
    return pl.pallas_call(
        _qproj_body,
        grid=(1,),
        in_specs=[full, _pick((2, D_MODEL), l, 2), _pick((D_MODEL, D_MODEL), l)],
        out_specs=full,
        out_shape=jax.ShapeDtypeStruct((n, D_MODEL), F32),
        compiler_params=_params("arbitrary"),
        name="qproj",
    )(x, norm_g, wq)


def _attn_step_body(q_ref, k_ref, v_ref, o_ref):
    q = q_ref[...] * (MEM_HDIM ** -0.5)
    part = jnp.sum(k_ref[...] * q, axis=-1, keepdims=True)
    s = part + pltpu.roll(part, MEM_HEADS, 2)
    e = jnp.exp(s - jnp.max(s, axis=1, keepdims=True))
    o_ref[...] = jnp.sum(e * v_ref[...], axis=1, keepdims=True) / jnp.sum(e, axis=1, keepdims=True)


def _attn_step(q, k_all, v_all, l, bb):
    b = q.shape[0]
    qs = pl.BlockSpec((bb, 1, 2 * MEM_HEADS, 128), lambda i: (i, 0, 0, 0))
    ms = pl.BlockSpec((None, bb, MEM_TOKENS, 2 * MEM_HEADS, 128), lambda i: (l, i, 0, 0, 0))
    return pl.pallas_call(
        _attn_step_body,
        grid=(b // bb,),
        in_specs=[qs, ms, ms],
        out_specs=qs,
        out_shape=jax.ShapeDtypeStruct((b, 1, 2 * MEM_HEADS, 128), F32),
        compiler_params=_params("arbitrary"),
        name="attn_step",
    )(q, k_all, v_all)


def _halves_major(a):
    lead = a.shape[:-2]
    return jnp.swapaxes(a.reshape(*lead, MEM_HEADS, 2, 128), -3, -2).reshape(*lead, 2 * MEM_HEADS, 128)


def _heads_major(a):
    lead = a.shape[:-2]
    return jnp.swapaxes(a.reshape(*lead, 2, MEM_HEADS, 128), -3, -2).reshape(*lead, D_MODEL)


def _oproj_body(x_ref, a_ref, g_ref, wo_ref, out_ref):
    y = _dot(a_ref[...].astype(BF16), wo_ref[...])
    out_ref[...] = x_ref[...] + _rms(y, g_ref[1:2, :])


def _oproj(x, a, norm_g, wo, l):
    n = x.shape[0]
    full = pl.BlockSpec((n, D_MODEL), lambda i: (0, 0))
    return pl.pallas_call(
        _oproj_body,
        grid=(1,),
        in_specs=[full, full, _pick((2, D_MODEL), l, 2), _pick((D_MODEL, D_MODEL), l)],
        out_specs=full,
        out_shape=jax.ShapeDtypeStruct((n, D_MODEL), F32),
        compiler_params=_params("arbitrary"),
        name="oproj",
    )(x, a, norm_g, wo)


def _prep_weights(w_ff_in, w_ff_out, w_in, b_igate, b_fgate, w_pool, s_pool, g_mlstm, gmlp_ln_g, gmlp_ln_b,
                  w_s, b_s, w_br_pool, w_br_mlstm, w_br_gmlp, w_out, g_mem, w_mq, w_mk, w_mv, w_mo):
    p = {}
    p["ffn_in"] = w_ff_in.astype(BF16)
    p["ffn_out"] = w_ff_out.astype(BF16)
    wt = jnp.swapaxes(w_in, 1, 2)
    w_if = jnp.pad(wt[:, OFF_IF:OFF_U], ((0, 0), (0, 128 - 2 * HEADS), (0, 0)))
    p["w1t"] = jnp.concatenate([wt[:, :OFF_IF], wt[:, OFF_U:OFF_GATE], w_if], axis=1).astype(BF16)
    p["wgt"] = wt[:, OFF_GATE:].astype(BF16)
    bif = jnp.concatenate([b_igate, b_fgate], axis=1)
    p["bif"] = jnp.pad(bif, ((0, 0), (0, 128 - 2 * HEADS))).reshape(DEPTH, 1, 128)
    same_group = jnp.eye(len(POOL_WINDOWS), dtype=bool)[None, :, None, :, None]
    wbd = jnp.where(same_group, w_pool[:, :, :, None, :], 0.0)
    p["wbd"] = wbd.reshape(DEPTH, POOL_WIDTH, POOL_WIDTH).astype(BF16)
    p["sp"] = s_pool.reshape(DEPTH, 1, POOL_WIDTH)
    p["gm"] = g_mlstm.reshape(DEPTH, 1, MLSTM_WIDTH)
    p["lng"] = gmlp_ln_g.reshape(DEPTH, 1, GMLP_WIDTH)
    p["lnb"] = gmlp_ln_b.reshape(DEPTH, 1, GMLP_WIDTH)
    p["ws"] = w_s.astype(BF16)
    p["bst"] = jnp.swapaxes(b_s, 1, 2)
    p["w00"] = jnp.repeat(w_s[:, :, 0, 0], GMLP_GDIM, axis=1).reshape(DEPTH, 1, GMLP_WIDTH)
    p["b0"] = jnp.repeat(b_s[:, :, 0], GMLP_GDIM, axis=1).reshape(DEPTH, 1, GMLP_WIDTH)
    p["wbp"] = w_br_pool.astype(BF16)
    p["wbm"] = w_br_mlstm.astype(BF16)
    p["wbg"] = w_br_gmlp.astype(BF16)
    p["wout"] = w_out.astype(BF16)
    p["gmem"] = g_mem.reshape(DEPTH, 1, D_MODEL)
    p["wq"] = w_mq.astype(BF16)
    p["wk"] = w_mk.astype(BF16)
    p["wv"] = w_mv.astype(BF16)
    p["wo"] = w_mo.astype(BF16)
    return p


def _prompt_layer(x, mem2d, norm_g, p, l, batch, seq):
    tm = 1024
    k32, v32, kb, vb = _memkv(mem2d, p["gmem"], p["wk"], p["wv"], l, 512)
    x = _ffn(x, norm_g, p["ffn_in"], p["ffn_out"], l, 0, 0, tm)
    zp, qkv, zo, ugv, zif = _inproj(x, norm_g, p["w1t"], l, tm, BF16)
    b3 = lambda a: a.reshape(batch, seq, a.shape[-1])
    hp = _pool_prompt(b3(zp), p["wbd"], p["sp"], l, 1024)
    hm, c_new, n_new, m_new = _mlstm_chunks(b3(qkv), b3(zif), b3(zo), p["bif"], p["gm"], l, 4)
    hg = _gmlp_prompt(b3(ugv), p["lng"], p["lnb"], p["ws"], p["bst"], l, 1024)
    f2 = lambda a: a.reshape(batch * seq, a.shape[-1])
    x = _merge(x, f2(hp), f2(hm), f2(hg), norm_g, p["wgt"], p["wbp"], p["wbm"], p["wbg"], p["wout"], l, tm)
    x = _attn_prompt(b3(x), kb.reshape(batch, MEM_TOKENS, D_MODEL), vb.reshape(batch, MEM_TOKENS, D_MODEL),
                     norm_g, p["wq"], p["wo"], l, 1024).reshape(batch * seq, D_MODEL)
    x = _ffn(x, norm_g, p["ffn_in"], p["ffn_out"], l, 3, 1, tm)
    pool_buf = b3(zp)[:, seq - POOL_BUF:, :]
    mk = k32.reshape(batch, MEM_TOKENS, MEM_HEADS, MEM_HDIM)
    mv = v32.reshape(batch, MEM_TOKENS, MEM_HEADS, MEM_HDIM)
    return x, pool_buf, c_new, n_new.reshape(batch, HEADS, HDIM), m_new.reshape(batch, HEADS), mk, mv


def _sample_layer(x, norm_g, p, l, pool_t, c_all, n_all, m_all, c_prev, k_all, v_all):
    batch = x.shape[0]
    tm = batch
    x = _ffn(x, norm_g, p["ffn_in"], p["ffn_out"], l, 0, 0, tm)
    zp, qkv, zo, ugv, zif = _inproj(x, norm_g, p["w1t"], l, tm, F32)
    hp, new_buf = _pool_step(zp, pool_t, p["wbd"], p["sp"], l)
    r3 = lambda a: a.reshape(batch, 1, a.shape[-1])
    hm, c_out, n_new, m_new = _mlstm_step(r3(qkv), r3(zif), r3(zo), p["bif"], p["gm"],
                                          c_all, n_all, m_all, c_prev, l, 16)
    hg, vn = _gmlp_step(ugv, p["lng"], p["lnb"], p["w00"], p["b0"], l)
    x = _merge(x, hp, hm.reshape(batch, MLSTM_WIDTH), hg, norm_g, p["wgt"], p["wbp"], p["wbm"], p["wbg"],
               p["wout"], l, tm)
    q = _qproj(x, norm_g, p["wq"], l)
    a = _attn_step(_halves_major(q.reshape(batch, 1, MEM_HEADS, MEM_HDIM)), k_all, v_all, l, 4)
    x = _oproj(x, _heads_major(a.reshape(batch, 2 * MEM_HEADS, 128)), norm_g, p["wo"], l)
    x = _ffn(x, norm_g, p["ffn_in"], p["ffn_out"], l, 3, 1, tm)
    return x, new_buf, c_out, n_new, m_new.reshape(batch, HEADS), vn.reshape(batch, 1, GMLP_WIDTH)


def kernel(x_prompt, x_sample, state_pool, state_mlstm_C, state_mlstm_n, state_mlstm_m, cache_mem_k, cache_mem_v, mem_prompt, norm_g, w_ff_in, w_ff_out, w_in, b_igate, b_fgate, w_pool, s_pool, g_mlstm, gmlp_ln_g, gmlp_ln_b, w_s, b_s, w_br_pool, w_br_mlstm, w_br_gmlp, w_out, g_mem, w_mq, w_mk, w_mv, w_mo):
    batch, seq, _ = x_prompt.shape
    dec_batch = x_sample.shape[0]
    assert x_sample.shape[1] == 1 and seq % CHUNK == 0 and PAST_LEN % CHUNK == 0 and PAST_LEN >= max(POOL_WINDOWS)
    p = _prep_weights(w_ff_in, w_ff_out, w_in, b_igate, b_fgate, w_pool, s_pool, g_mlstm, gmlp_ln_g, gmlp_ln_b,
                      w_s, b_s, w_br_pool, w_br_mlstm, w_br_gmlp, w_out, g_mem, w_mq, w_mk, w_mv, w_mo)
    xp = x_prompt.reshape(batch * seq, D_MODEL)
    xs = x_sample.reshape(dec_batch, D_MODEL)
    mem2d = mem_prompt.reshape(batch * MEM_TOKENS, D_MODEL)
    pool_t = jnp.swapaxes(state_pool, 1, 2)
    m_all = state_mlstm_m.reshape(DEPTH, dec_batch, 1, HEADS)
    k_dense = _halves_major(cache_mem_k)
    v_dense = _halves_major(cache_mem_v)
    outs =[[] for _ in range(10)]
    c_samp = None
    for l in range(DEPTH):
        xp, pb, c, n, m, mk, mv = _prompt_layer(xp, mem2d, norm_g, p, l, batch, seq)
        xs, pbs, c_samp, ns, ms, vr = _sample_layer(xs, norm_g, p, l, pool_t, state_mlstm_C, state_mlstm_n, m_all,
                                                    c_samp, k_dense, v_dense)
        for lst, val in zip(outs, (pb, pbs, c, n, m, ns, ms, mk, mv, vr)):
            lst.append(val)
    pb, pbs, c, n, m, ns, ms, mk, mv, vr = [jnp.stack(o) for o in outs]
    return (xp.reshape(batch, seq, D_MODEL), xs.reshape(dec_batch, 1, D_MODEL),
            pb, jnp.swapaxes(pbs, 1, 2), c, n, m, c_samp, ns, ms, mk, mv, vr)
```

```python
import functools

import jax
import jax.numpy as jnp
from jax import lax
from jax.experimental import pallas as pl
from jax.experimental.pallas import tpu as pltpu

F32 = jnp.float32
BF16 = jnp.bfloat16

D_MODEL = 1024
DEPTH = 4
PAST_LEN = 16384
POOL_WINDOWS = (2, 4, 8, 16)
POOL_WIDTH = 256
POOL_GDIM = 64
POOL_BUF = 15
HEADS = 4
MLSTM_WIDTH = 512
HDIM = 128
CHUNK = 128
GMLP_WIDTH = 256
GMLP_GROUPS = 4
GMLP_GDIM = 64
MEM_TOKENS = 256
MEM_HEADS = 4
MEM_HDIM = 256
D_FF = 2816
FF_CHUNK = 256
N_FF_CHUNKS = D_FF // FF_CHUNK
RMS_EPS = 1e-6
LN_EPS = 1e-5
OFF_Q = POOL_WIDTH
OFF_IF = OFF_Q + 4 * MLSTM_WIDTH
OFF_U = OFF_IF + 2 * HEADS
OFF_GATE = OFF_U + 2 * GMLP_WIDTH
W1_QKV = POOL_WIDTH
W1_O = W1_QKV + 3 * MLSTM_WIDTH
W1_UGV = W1_O + MLSTM_WIDTH
W1_IF = W1_UGV + 2 * GMLP_WIDTH
W1_ROWS = W1_IF + 128

VMEM_LIMIT_BYTES = 56 * 1024 * 1024


def _params(*sem):
    return pltpu.CompilerParams(dimension_semantics=sem, vmem_limit_bytes=VMEM_LIMIT_BYTES)


def _pick(tail, *idx):
    nt = len(tail)
    return pl.BlockSpec((None,) * len(idx) + tuple(tail), lambda *_: tuple(idx) + (0,) * nt,
                        pipeline_mode=pl.Buffered(1))


def _rms(x, g):
    return x * lax.rsqrt(jnp.mean(x * x, axis=-1, keepdims=True) + RMS_EPS) * g


def _dot(a, b):
    return jnp.dot(a, b, preferred_element_type=F32)


def _dot_nt(a, b):
    return lax.dot_general(a, b, (((1,), (1,)), ((), ())), preferred_element_type=F32)


def _dot_tn(a, b):
    return lax.dot_general(a, b, (((0,), (0,)), ((), ())), preferred_element_type=F32)


def _sigmoid(x):
    return 1.0 / (1.0 + jnp.exp(-x))


def _gelu_tanh(x):
    return 0.5 * x * (1.0 + jnp.tanh(0.7978845608028654 * (x + 0.044715 * (x * x * x))))


def _log_sigmoid(x):
    return -(jnp.maximum(-x, 0.0) + jnp.log(1.0 + jnp.exp(-jnp.abs(x))))


def _swiglu_half_step(x, g_ref, wi_ref, wo_ref, acc_ref):
    xn = _rms(x, g_ref[0:1, :]).astype(BF16)
    for c in range(N_FF_CHUNKS):
        cols = slice(c * FF_CHUNK, (c + 1) * FF_CHUNK)
        gate = _dot(xn, wi_ref[:, cols])
        up = _dot(xn, wi_ref[:, D_FF + c * FF_CHUNK:D_FF + (c + 1) * FF_CHUNK])
        act = (gate * _sigmoid(gate) * up).astype(BF16)
        part = _dot(act, wo_ref[cols, :])
        if c == 0:
            acc_ref[...] = part
        else:
            acc_ref[...] += part
    return x + 0.5 * _rms(acc_ref[...], g_ref[1:2, :])


def _ffn_body(x_ref, g_ref, wi_ref, wo_ref, o_ref, acc_ref):
    o_ref[...] = _swiglu_half_step(x_ref[...], g_ref, wi_ref, wo_ref, acc_ref)


def _oproj_ffn_body(x_ref, a_ref, ga_ref, wmo_ref, g_ref, wi_ref, wo_ref, o_ref, acc_ref):
    y = _dot(a_ref[...].astype(BF16), wmo_ref[...])
    x = x_ref[...] + _rms(y, ga_ref[1:2, :])
    o_ref[...] = _swiglu_half_step(x, g_ref, wi_ref, wo_ref, acc_ref)


def _oproj_ffn(x, a, norm_g, wmo, w_in, w_out, l):
    n = x.shape[0]
    full = pl.BlockSpec((n, D_MODEL), lambda i: (0, 0))
    return pl.pallas_call(
        _oproj_ffn_body,
        grid=(1,),
        in_specs=[full, full, _pick((2, D_MODEL), l, 2), _pick((D_MODEL, D_MODEL), l),
                  _pick((2, D_MODEL), l, 3), _pick((D_MODEL, 2 * D_FF), l, 1), _pick((D_FF, D_MODEL), l, 1)],
        out_specs=full,
        out_shape=jax.ShapeDtypeStruct((n, D_MODEL), F32),
        scratch_shapes=[pltpu.VMEM((n, D_MODEL), F32)],
        compiler_params=_params("arbitrary"),
        name="oproj_ffn",
    )(x, a, norm_g, wmo, norm_g, w_in, w_out)


def _ffn(x, norm_g, w_in, w_out, l, sub, half, tm):
    n = x.shape[0]
    return pl.pallas_call(
        _ffn_body,
        grid=(n // tm,),
        in_specs=[
            pl.BlockSpec((tm, D_MODEL), lambda i: (i, 0)),
            _pick((2, D_MODEL), l, sub),
            _pick((D_MODEL, 2 * D_FF), l, half),
            _pick((D_FF, D_MODEL), l, half),
        ],
        out_specs=pl.BlockSpec((tm, D_MODEL), lambda i: (i, 0)),
        out_shape=jax.ShapeDtypeStruct((n, D_MODEL), F32),
        scratch_shapes=[pltpu.VMEM((tm, D_MODEL), F32)],
        compiler_params=_params("arbitrary"),
        name="ffn",
    )(x, norm_g, w_in, w_out)


def _inproj_body(x_ref, g_ref, w_ref, zp_ref, qkv_ref, zo_ref, ugv_ref, zif_ref):
    xn = _rms(x_ref[...], g_ref[0:1, :]).astype(BF16)
    z = _dot_nt(xn, w_ref[...])
    zp_ref[...] = z[:, :W1_QKV]
    q = z[:, W1_QKV:W1_QKV + MLSTM_WIDTH]
    k = z[:, W1_QKV + MLSTM_WIDTH:W1_QKV + 2 * MLSTM_WIDTH] * (HDIM ** -0.5)
    v = z[:, W1_QKV + 2 * MLSTM_WIDTH:W1_O]
    qkv_ref[:, 0:MLSTM_WIDTH] = q.astype(qkv_ref.dtype)
    qkv_ref[:, MLSTM_WIDTH:2 * MLSTM_WIDTH] = k.astype(qkv_ref.dtype)
    qkv_ref[:, 2 * MLSTM_WIDTH:] = v.astype(qkv_ref.dtype)
    zo_ref[...] = z[:, W1_O:W1_UGV]
    ugv_ref[...] = z[:, W1_UGV:W1_IF]
    zif_ref[...] = z[:, W1_IF:]


def _inproj(x, norm_g, w1t, l, tm, qkv_dtype):
    n = x.shape[0]
    row = lambda w: pl.BlockSpec((tm, w), lambda i: (i, 0))
    return pl.pallas_call(
        _inproj_body,
        grid=(n // tm,),
        in_specs=[row(D_MODEL), _pick((2, D_MODEL), l, 1), _pick((W1_ROWS, D_MODEL), l)],
        out_specs=[row(POOL_WIDTH), row(3 * MLSTM_WIDTH), row(MLSTM_WIDTH), row(2 * GMLP_WIDTH), row(128)],
        out_shape=[
            jax.ShapeDtypeStruct((n, POOL_WIDTH), F32),
            jax.ShapeDtypeStruct((n, 3 * MLSTM_WIDTH), qkv_dtype),
            jax.ShapeDtypeStruct((n, MLSTM_WIDTH), F32),
            jax.ShapeDtypeStruct((n, 2 * GMLP_WIDTH), F32),
            jax.ShapeDtypeStruct((n, 128), F32),
        ],
        compiler_params=_params("arbitrary"),
        name="inproj",
    )(x, norm_g, w1t)


def _pool_select(sums, u, count_of):
    lane = lax.broadcasted_iota(jnp.int32, u.shape, u.ndim - 1)
    mean = None
    for gi, w in enumerate(POOL_WINDOWS):
        m = sums[w] * (1.0 / count_of(w))
        mean = m if mean is None else jnp.where(lane >= gi * POOL_GDIM, m, mean)
    return mean - u


POOL_HALO = 2 * max(POOL_WINDOWS)


def _pool_prompt_body(zp_ref, wbd_ref, sp_ref, o_ref, full_ref, p2_ref, p4_ref, p8_ref, *, tt):
    t = pl.program_id(1)
    hl = POOL_HALO
    n = hl + tt

    @pl.when(t == 0)
    def _():
        full_ref[0:hl, :] = jnp.zeros((hl, POOL_WIDTH), F32)

    @pl.when(t > 0)
    def _():
        full_ref[0:hl, :] = full_ref[tt:n, :]

    u = zp_ref[0]
    full_ref[hl:n, :] = u
    p2_ref[8:n, :] = full_ref[8:n, :] + full_ref[7:n - 1, :]
    p4_ref[16:n, :] = p2_ref[16:n, :] + p2_ref[14:n - 2, :]
    p8_ref[24:n, :] = p4_ref[24:n, :] + p4_ref[20:n - 4, :]
    sums = {2: p2_ref[hl:n, :], 4: p4_ref[hl:n, :], 8: p8_ref[hl:n, :],
            16: p8_ref[hl:n, :] + p8_ref[hl - 8:n - 8, :]}
    pos = t * tt + lax.broadcasted_iota(jnp.int32, (tt, 1), 0)
    diff = _pool_select(sums, u, lambda w: jnp.minimum(pos + 1, w).astype(F32))
    y = _dot(diff.astype(BF16), wbd_ref[...]) * sp_ref[...]
    o_ref[0] = y.astype(o_ref.dtype)


def _inproj_mix_body(x_ref, g_ref, w_ref, wbd_ref, sp_ref, lng_ref, lnb_ref, ws_ref, bst_ref,
                     qkv_ref, zo_ref, zif_ref, hp_ref, hg_ref, tail_ref,
                     full_ref, p2_ref, p4_ref, p8_ref, *, tm, tiles_per_seq):
    xn = _rms(x_ref[...], g_ref[0:1, :]).astype(BF16)
    z = _dot_nt(xn, w_ref[...])
    q = z[:, W1_QKV:W1_QKV + MLSTM_WIDTH]
    k = z[:, W1_QKV + MLSTM_WIDTH:W1_QKV + 2 * MLSTM_WIDTH] * (HDIM ** -0.5)
    v = z[:, W1_QKV + 2 * MLSTM_WIDTH:W1_O]
    qkv_ref[:, 0:MLSTM_WIDTH] = q.astype(qkv_ref.dtype)
    qkv_ref[:, MLSTM_WIDTH:2 * MLSTM_WIDTH] = k.astype(qkv_ref.dtype)
    qkv_ref[:, 2 * MLSTM_WIDTH:] = v.astype(qkv_ref.dtype)
    zo_ref[...] = z[:, W1_O:W1_UGV]
    zif_ref[...] = z[:, W1_IF:]

    t = lax.rem(pl.program_id(0), tiles_per_seq)
    hl = POOL_HALO
    n = hl + tm

    @pl.when(t == 0)
    def _():
        full_ref[0:hl, :] = jnp.zeros((hl, POOL_WIDTH), F32)

    @pl.when(t > 0)
    def _():
        full_ref[0:hl, :] = full_ref[tm:n, :]

    u = z[:, :W1_QKV]
    full_ref[hl:n, :] = u
    tail_ref[0] = u[tm - 16:, :]
    p2_ref[8:n, :] = full_ref[8:n, :] + full_ref[7:n - 1, :]
    p4_ref[16:n, :] = p2_ref[16:n, :] + p2_ref[14:n - 2, :]
    p8_ref[24:n, :] = p4_ref[24:n, :] + p4_ref[20:n - 4, :]
    sums = {2: p2_ref[hl:n, :], 4: p4_ref[hl:n, :], 8: p8_ref[hl:n, :],
            16: p8_ref[hl:n, :] + p8_ref[hl - 8:n - 8, :]}
    pos = t * tm + lax.broadcasted_iota(jnp.int32, (tm, 1), 0)
    diff = _pool_select(sums, u, lambda w: jnp.minimum(pos + 1, w).astype(F32))
    hp_ref[...] = (_dot(diff.astype(BF16), wbd_ref[...]) * sp_ref[...]).astype(hp_ref.dtype)

    row = lax.broadcasted_iota(jnp.int32, (CHUNK, CHUNK), 0)
    col = lax.broadcasted_iota(jnp.int32, (CHUNK, CHUNK), 1)
    causal = col <= row
    lane = lax.broadcasted_iota(jnp.int32, (CHUNK, GMLP_WIDTH), 1)
    ws = [jnp.where(causal, ws_ref[g], jnp.zeros((), ws_ref.dtype)) for g in range(GMLP_GROUPS)]
    for c in range(tm // CHUNK):
        rows = slice(c * CHUNK, (c + 1) * CHUNK)
        gu = _gelu_tanh(z[rows, W1_UGV:W1_UGV + GMLP_WIDTH])
        gv = _gelu_tanh(z[rows, W1_UGV + GMLP_WIDTH:W1_IF])
        vn = _layer_norm(gv, lng_ref[...], lnb_ref[...]).astype(BF16)
        mixed = None
        for g in range(GMLP_GROUPS):
            mg = _dot(ws[g], vn) + bst_ref[:, g:g + 1]
            mixed = mg if mixed is None else jnp.where(lane >= g * GMLP_GDIM, mg, mixed)
        hg_ref[rows, :] = (gu * mixed).astype(hg_ref.dtype)


def _inproj_mix(x, norm_g, p, l, tm, seq):
    n = x.shape[0]
    tiles_per_seq = seq // tm
    row = lambda w: pl.BlockSpec((tm, w), lambda i: (i, 0))
    return pl.pallas_call(
        functools.partial(_inproj_mix_body, tm=tm, tiles_per_seq=tiles_per_seq),
        grid=(n // tm,),
        in_specs=[row(D_MODEL), _pick((2, D_MODEL), l, 1), _pick((W1_ROWS, D_MODEL), l),
                  _pick((POOL_WIDTH, POOL_WIDTH), l), _pick((1, POOL_WIDTH), l),
                  _pick((1, GMLP_WIDTH), l), _pick((1, GMLP_WIDTH), l),
                  _pick((GMLP_GROUPS, CHUNK, CHUNK), l), _pick((CHUNK, GMLP_GROUPS), l)],
        out_specs=[row(3 * MLSTM_WIDTH), row(MLSTM_WIDTH), row(128), row(POOL_WIDTH), row(GMLP_WIDTH),
                   pl.BlockSpec((1, 16, POOL_WIDTH), lambda i: (i // tiles_per_seq, 0, 0))],
        out_shape=[
            jax.ShapeDtypeStruct((n, 3 * MLSTM_WIDTH), BF16),
            jax.ShapeDtypeStruct((n, MLSTM_WIDTH), F32),
            jax.ShapeDtypeStruct((n, 128), F32),
            jax.ShapeDtypeStruct((n, POOL_WIDTH), BF16),
            jax.ShapeDtypeStruct((n, GMLP_WIDTH), BF16),
            jax.ShapeDtypeStruct((n // seq, 16, POOL_WIDTH), F32),
        ],
        scratch_shapes=[pltpu.VMEM((tm + POOL_HALO, POOL_WIDTH), F32)] * 4,
        compiler_params=_params("arbitrary"),
        name="inproj_mix",
    )(x, norm_g, p["w1t"], p["wbd"], p["sp"], p["lng"], p["lnb"], p["ws"], p["bst"])


def _pool_prompt(zp, wbd, sp, l, tt):
    b, t, _ = zp.shape
    return pl.pallas_call(
        functools.partial(_pool_prompt_body, tt=tt),
        grid=(b, t // tt),
        in_specs=[
            pl.BlockSpec((1, tt, POOL_WIDTH), lambda i, j: (i, j, 0)),
            _pick((POOL_WIDTH, POOL_WIDTH), l),
            _pick((1, POOL_WIDTH), l),
        ],
        out_specs=pl.BlockSpec((1, tt, POOL_WIDTH), lambda i, j: (i, j, 0)),
        out_shape=jax.ShapeDtypeStruct((b, t, POOL_WIDTH), BF16),
        scratch_shapes=[pltpu.VMEM((tt + POOL_HALO, POOL_WIDTH), F32)] * 4,
        compiler_params=_params("arbitrary", "arbitrary"),
        name="pool_prompt",
    )(zp, wbd, sp)


def _pool_step_body(zp_ref, buf_ref, wbd_ref, sp_ref, o_ref, nbuf_ref):
    u = zp_ref[...]
    acc = u
    sums = {}
    for j in range(1, max(POOL_WINDOWS)):
        acc = acc + buf_ref[POOL_BUF - j]
        if j + 1 in POOL_WINDOWS:
            sums[j + 1] = acc
    diff = _pool_select(sums, u, lambda w: float(w))
    y = _dot(diff.astype(BF16), wbd_ref[...]) * sp_ref[...]
    o_ref[...] = y.astype(o_ref.dtype)
    nbuf_ref[0:POOL_BUF - 1] = buf_ref[1:POOL_BUF]
    nbuf_ref[POOL_BUF - 1] = u


def _pool_step(zp, pool_t, wbd, sp, l):
    b = zp.shape[0]
    full = lambda shape: pl.BlockSpec(shape, lambda i: (0,) * len(shape))
    return pl.pallas_call(
        _pool_step_body,
        grid=(1,),
        in_specs=[full((b, POOL_WIDTH)), _pick((POOL_BUF, b, POOL_WIDTH), l),
                  _pick((POOL_WIDTH, POOL_WIDTH), l), _pick((1, POOL_WIDTH), l)],
        out_specs=[full((b, POOL_WIDTH)), full((POOL_BUF, b, POOL_WIDTH))],
        out_shape=[jax.ShapeDtypeStruct((b, POOL_WIDTH), BF16),
                   jax.ShapeDtypeStruct((POOL_BUF, b, POOL_WIDTH), F32)],
        compiler_params=_params("arbitrary"),
        name="pool_step",
    )(zp, pool_t, wbd, sp)


def _bdot(a, b, ca, cb):
    return lax.dot_general(a, b, (((ca,), (cb,)), ((0,), (0,))), preferred_element_type=F32)


def _mlstm_chunk_body(qkv_ref, zif_ref, zo_ref, bif_ref, gm_ref, h_ref, c_ref, n_ref, m_ref, *, bb):
    @pl.when(pl.program_id(1) == 0)
    def _():
        c_ref[...] = jnp.zeros(c_ref.shape, F32)
        n_ref[...] = jnp.zeros(n_ref.shape, F32)
        m_ref[...] = jnp.zeros(m_ref.shape, F32)

    L = CHUNK
    G = bb * HEADS
    row = lax.broadcasted_iota(jnp.int32, (L, L), 0)
    col = lax.broadcasted_iota(jnp.int32, (L, L), 1)
    causal = col <= row
    tri = jnp.where(causal, 1.0, 0.0).astype(BF16)
    lane = lax.broadcasted_iota(jnp.int32, (L, 128), 1)
    is_forget = (lane >= HEADS) & (lane < 2 * HEADS)
    gls = []
    for bi in range(bb):
        g = zif_ref[bi] + bif_ref[...]
        gls.append(jnp.where(is_forget, _log_sigmoid(g), g))
    gl_all = jnp.concatenate(gls, axis=1)
    hi = gl_all.astype(BF16)
    r1 = gl_all - hi.astype(F32)
    mid = r1.astype(BF16)
    lo = (r1 - mid.astype(F32)).astype(BF16)
    cs_all = _dot(tri, hi) + _dot(tri, mid) + _dot(tri, lo)
    ig_col, b_col, ig_row, b_row, qs, ks, vs, zos = [], [], [], [], [], [], [], []
    for bi in range(bb):
        gl = gls[bi]
        cs = cs_all[:, bi * 128:(bi + 1) * 128]
        gl_t = gl.T
        cs_t = cs.T
        for h in range(HEADS):
            hs = slice(h * HDIM, (h + 1) * HDIM)
            ig_col.append(gl[:, h:h + 1])
            b_col.append(cs[:, HEADS + h:HEADS + h + 1])
            ig_row.append(gl_t[h:h + 1, :])
            b_row.append(cs_t[HEADS + h:HEADS + h + 1, :])
            qs.append(qkv_ref[bi, :, hs])
            ks.append(qkv_ref[bi, :, MLSTM_WIDTH + h * HDIM:MLSTM_WIDTH + (h + 1) * HDIM])
            vs.append(qkv_ref[bi, :, 2 * MLSTM_WIDTH + h * HDIM:2 * MLSTM_WIDTH + (h + 1) * HDIM])
            zos.append(zo_ref[bi, :, hs])
    ig_col, b_col, ig_row, b_row = (jnp.stack(a) for a in (ig_col, b_col, ig_row, b_row))
    q, k, v, zo = (jnp.stack(a) for a in (qs, ks, vs, zos))
    gm = jnp.stack([gm_ref[:, h * HDIM:(h + 1) * HDIM] for h in range(HEADS)] * bb)
    c_old = c_ref[...].reshape(G, HDIM, HDIM)
    n_old = n_ref[...].reshape(G, 1, HDIM)
    m_prev = m_ref[...].reshape(G, 1, 1)

    dmat = jnp.where(causal[None], b_col - b_row + ig_row, -jnp.inf)
    inter = b_col + m_prev
    m_t = jnp.maximum(inter, jnp.max(dmat, axis=-1, keepdims=True))
    w_intra = jnp.exp(dmat - m_t)
    w_inter = jnp.exp(inter - m_t)
    s = _bdot(q, k, 2, 2) * w_intra
    num = _bdot(s.astype(BF16), v, 2, 1) + w_inter * _bdot(q, c_old.astype(BF16), 2, 2)
    qn = jnp.sum(q.astype(F32) * n_old, axis=-1, keepdims=True)
    den = jnp.sum(s, axis=-1, keepdims=True) + w_inter * qn
    hh = num / jnp.maximum(jnp.abs(den), jnp.exp(-m_t))
    hh = hh * lax.rsqrt(jnp.mean(hh * hh, axis=-1, keepdims=True) + RMS_EPS)
    hh = (hh * gm * _sigmoid(zo)).astype(h_ref.dtype)
    for bi in range(bb):
        for h in range(HEADS):
            h_ref[bi, :, h * HDIM:(h + 1) * HDIM] = hh[bi * HEADS + h]
    m_new = m_t[:, L - 1:L, :]
    b_last = b_col[:, L - 1:L, :]
    w_state = jnp.exp(b_last - b_col + ig_col - m_new)
    decay = jnp.exp(b_last + m_prev - m_new)
    vw = (v.astype(F32) * w_state).astype(BF16)
    c_new = decay * c_old + _bdot(vw, k, 1, 1)
    n_new = decay * n_old + jnp.sum(k.astype(F32) * w_state, axis=1, keepdims=True)
    c_ref[...] = c_new.reshape(c_ref.shape)
    n_ref[...] = n_new.reshape(n_ref.shape)
    m_ref[...] = m_new.reshape(m_ref.shape)


def _mlstm_chunks(qkv, zif, zo, bif, gm, l, bb):
    b, t, _ = qkv.shape
    blk = lambda w: pl.BlockSpec((bb, CHUNK, w), lambda i, j: (i, j, 0))
    state = lambda r, c: pl.BlockSpec((bb, HEADS, r, c), lambda i, j: (i, 0, 0, 0))
    return pl.pallas_call(
        functools.partial(_mlstm_chunk_body, bb=bb),
        grid=(b // bb, t // CHUNK),
        in_specs=[blk(3 * MLSTM_WIDTH), blk(128), blk(MLSTM_WIDTH), _pick((1, 128), l), _pick((1, MLSTM_WIDTH), l)],
        out_specs=[blk(MLSTM_WIDTH), state(HDIM, HDIM), state(1, HDIM), state(1, 1)],
        out_shape=[
            jax.ShapeDtypeStruct((b, t, MLSTM_WIDTH), BF16),
            jax.ShapeDtypeStruct((b, HEADS, HDIM, HDIM), F32),
            jax.ShapeDtypeStruct((b, HEADS, 1, HDIM), F32),
            jax.ShapeDtypeStruct((b, HEADS, 1, 1), F32),
        ],
        compiler_params=_params("arbitrary", "arbitrary"),
        name="mlstm_chunks",
    )(qkv, zif, zo, bif, gm)


def _mlstm_step_body(*refs, aliased):
    qkv_ref, zif_ref, zo_ref, bif_ref, gm_ref, c0_ref, n0_ref, m0_ref = refs[:8]
    h_ref, c_ref, n_ref, m_ref = refs[9:] if aliased else refs[8:]
    g = zif_ref[...] + bif_ref[...]
    r = lax.broadcasted_iota(jnp.int32, (HDIM, HDIM), 0)
    c = lax.broadcasted_iota(jnp.int32, (HDIM, HDIM), 1)
    eye = (r == c)[None]
    for h in range(HEADS):
        hs = slice(h * HDIM, (h + 1) * HDIM)
        ig = g[:, :, h:h + 1]
        lf = _log_sigmoid(g[:, :, HEADS + h:HEADS + h + 1])
        m_prev = m0_ref[:, :, h:h + 1]
        q = qkv_ref[:, :, hs]
        k = qkv_ref[:, :, MLSTM_WIDTH + h * HDIM:MLSTM_WIDTH + (h + 1) * HDIM]
        v = qkv_ref[:, :, 2 * MLSTM_WIDTH + h * HDIM:2 * MLSTM_WIDTH + (h + 1) * HDIM]
        c_h = c0_ref[:, h]
        n_h = n0_ref[:, h:h + 1, :]
        inter = lf + m_prev
        m_t = jnp.maximum(inter, ig)
        w_intra = jnp.exp(ig - m_t)
        w_inter = jnp.exp(inter - m_t)
        s = jnp.sum(q * k, axis=-1, keepdims=True) * w_intra
        cq = jnp.sum(c_h * q, axis=-1, keepdims=True)
        v_col = jnp.sum(jnp.where(eye, v, 0.0), axis=-1, keepdims=True)
        num = s * v_col + w_inter * cq
        den = s + w_inter * jnp.sum(n_h * q, axis=-1, keepdims=True)
        hcol = num / jnp.maximum(jnp.abs(den), jnp.exp(-m_t))
        hcol = hcol * lax.rsqrt(jnp.mean(hcol * hcol, axis=1, keepdims=True) + RMS_EPS)
        hrow = jnp.sum(jnp.where(eye, hcol, 0.0), axis=1, keepdims=True)
        hrow = hrow * gm_ref[:, hs] * _sigmoid(zo_ref[:, :, hs])
        h_ref[:, :, hs] = hrow.astype(h_ref.dtype)
        w_state = jnp.exp(ig - m_t)
        decay = jnp.exp(lf + m_prev - m_t)
        c_ref[:, h] = decay * c_h + (w_state * v_col) * k
        n_ref[:, h:h + 1, :] = decay * n_h + w_state * k
        m_ref[:, :, h:h + 1] = m_t


def _mlstm_step(qkv, zif, zo, bif, gm, c_all, n_all, m_all, c_prev, l, bb):
    b = qkv.shape[0]
    blk = lambda w: pl.BlockSpec((bb, 1, w), lambda i: (i, 0, 0))
    st_c = pl.BlockSpec((None, bb, HEADS, HDIM, HDIM), lambda i: (l, i, 0, 0, 0))
    st_n = pl.BlockSpec((None, bb, HEADS, HDIM), lambda i: (l, i, 0, 0))
    st_m = pl.BlockSpec((None, bb, 1, HEADS), lambda i: (l, i, 0, 0))
    in_specs = [blk(3 * MLSTM_WIDTH), blk(128), blk(MLSTM_WIDTH), _pick((1, 128), l), _pick((1, MLSTM_WIDTH), l),
                st_c, st_n, st_m]
    args = [qkv, zif, zo, bif, gm, c_all, n_all, m_all]
    aliases = {}
    if c_prev is not None:
        in_specs.append(pl.BlockSpec(memory_space=pl.ANY))
        args.append(c_prev)
        aliases = {8: 1}
    return pl.pallas_call(
        functools.partial(_mlstm_step_body, aliased=c_prev is not None),
        grid=(b // bb,),
        in_specs=in_specs,
        out_specs=[blk(MLSTM_WIDTH), st_c, pl.BlockSpec((bb, HEADS, HDIM), lambda i: (i, 0, 0)),
                   pl.BlockSpec((bb, 1, HEADS), lambda i: (i, 0, 0))],
        out_shape=[
            jax.ShapeDtypeStruct((b, 1, MLSTM_WIDTH), BF16),
            jax.ShapeDtypeStruct(c_all.shape, F32),
            jax.ShapeDtypeStruct((b, HEADS, HDIM), F32),
            jax.ShapeDtypeStruct((b, 1, HEADS), F32),
        ],
        input_output_aliases=aliases,
        compiler_params=_params("arbitrary"),
        name="mlstm_step",
    )(*args)


def _layer_norm(v, g, b):
    vc = v - jnp.mean(v, axis=-1, keepdims=True)
    return vc * lax.rsqrt(jnp.mean(vc * vc, axis=-1, keepdims=True) + LN_EPS) * g + b


def _gmlp_prompt_body(ugv_ref, lng_ref, lnb_ref, ws_ref, bst_ref, o_ref, *, nchunk):
    row = lax.broadcasted_iota(jnp.int32, (CHUNK, CHUNK), 0)
    col = lax.broadcasted_iota(jnp.int32, (CHUNK, CHUNK), 1)
    causal = col <= row
    lane = lax.broadcasted_iota(jnp.int32, (CHUNK, GMLP_WIDTH), 1)
    ws = [jnp.where(causal, ws_ref[g], jnp.zeros((), ws_ref.dtype)) for g in range(GMLP_GROUPS)]
    for c in range(nchunk):
        rows = slice(c * CHUNK, (c + 1) * CHUNK)
        u = _gelu_tanh(ugv_ref[0, rows, 0:GMLP_WIDTH])
        v = _gelu_tanh(ugv_ref[0, rows, GMLP_WIDTH:])
        vn = _layer_norm(v, lng_ref[...], lnb_ref[...]).astype(BF16)
        mixed = None
        for g in range(GMLP_GROUPS):
            mg = _dot(ws[g], vn) + bst_ref[:, g:g + 1]
            mixed = mg if mixed is None else jnp.where(lane >= g * GMLP_GDIM, mg, mixed)
        o_ref[0, rows, :] = (u * mixed).astype(o_ref.dtype)


def _gmlp_prompt(ugv, lng, lnb, ws, bst, l, tt):
    b, t, _ = ugv.shape
    return pl.pallas_call(
        functools.partial(_gmlp_prompt_body, nchunk=tt // CHUNK),
        grid=(b, t // tt),
        in_specs=[pl.BlockSpec((1, tt, 2 * GMLP_WIDTH), lambda i, j: (i, j, 0)),
                  _pick((1, GMLP_WIDTH), l), _pick((1, GMLP_WIDTH), l),
                  _pick((GMLP_GROUPS, CHUNK, CHUNK), l), _pick((CHUNK, GMLP_GROUPS), l)],
        out_specs=pl.BlockSpec((1, tt, GMLP_WIDTH), lambda i, j: (i, j, 0)),
        out_shape=jax.ShapeDtypeStruct((b, t, GMLP_WIDTH), BF16),
        compiler_params=_params("arbitrary", "arbitrary"),
        name="gmlp_prompt",
    )(ugv, lng, lnb, ws, bst)


def _gmlp_step_body(ugv_ref, lng_ref, lnb_ref, w00_ref, b0_ref, o_ref, vn_ref):
    u = _gelu_tanh(ugv_ref[:, 0:GMLP_WIDTH])
    v = _gelu_tanh(ugv_ref[:, GMLP_WIDTH:])
    vn = _layer_norm(v, lng_ref[...], lnb_ref[...])
    vn_ref[...] = vn
    o_ref[...] = (u * (w00_ref[...] * vn + b0_ref[...])).astype(o_ref.dtype)


def _gmlp_step(ugv, lng, lnb, w00, b0, l):
    b = ugv.shape[0]
    full = lambda w: pl.BlockSpec((b, w), lambda i: (0, 0))
    vec = _pick((1, GMLP_WIDTH), l)
    return pl.pallas_call(
        _gmlp_step_body,
        grid=(1,),
        in_specs=[full(2 * GMLP_WIDTH), vec, vec, vec, vec],
        out_specs=[full(GMLP_WIDTH), full(GMLP_WIDTH)],
        out_shape=[jax.ShapeDtypeStruct((b, GMLP_WIDTH), BF16), jax.ShapeDtypeStruct((b, GMLP_WIDTH), F32)],
        compiler_params=_params("arbitrary"),
        name="gmlp_step",
    )(ugv, lng, lnb, w00, b0)


def _merge_body(x_ref, hp_ref, hm_ref, hg_ref, g_ref, wgt_ref, wbp_ref, wbm_ref, wbg_ref, wout_ref, o_ref):
    x = x_ref[...]
    xn = _rms(x, g_ref[0:1, :]).astype(BF16)
    merged = None
    for i, (h_ref, w_ref) in enumerate(((hp_ref, wbp_ref), (hm_ref, wbm_ref), (hg_ref, wbg_ref))):
        gate = _sigmoid(_dot_nt(xn, wgt_ref[i * D_MODEL:(i + 1) * D_MODEL, :]))
        term = gate * _dot(h_ref[...], w_ref[...])
        merged = term if merged is None else merged + term
    y = _dot(merged.astype(BF16), wout_ref[...])
    o_ref[...] = x + _rms(y, g_ref[1:2, :])


def _merge_q_body(x_ref, hp_ref, hm_ref, hg_ref, g_ref, wgt_ref, wbp_ref, wbm_ref, wbg_ref, wout_ref,
                  ga_ref, wq_ref, o_ref, q_ref):
    _merge_body(x_ref, hp_ref, hm_ref, hg_ref, g_ref, wgt_ref, wbp_ref, wbm_ref, wbg_ref, wout_ref, o_ref)
    xn = _rms(o_ref[...], ga_ref[0:1, :]).astype(BF16)
    q_ref[...] = _dot(xn, wq_ref[...])


def _merge(x, hp, hm, hg, norm_g, wgt, wbp, wbm, wbg, wout, l, tm, wq=None):
    n = x.shape[0]
    row = lambda w: pl.BlockSpec((tm, w), lambda i: (i, 0))
    in_specs = [row(D_MODEL), row(POOL_WIDTH), row(MLSTM_WIDTH), row(GMLP_WIDTH), _pick((2, D_MODEL), l, 1),
                _pick((3 * D_MODEL, D_MODEL), l), _pick((POOL_WIDTH, D_MODEL), l),
                _pick((MLSTM_WIDTH, D_MODEL), l), _pick((GMLP_WIDTH, D_MODEL), l), _pick((D_MODEL, D_MODEL), l)]
    args = [x, hp, hm, hg, norm_g, wgt, wbp, wbm, wbg, wout]
    out_shape = jax.ShapeDtypeStruct((n, D_MODEL), F32)
    if wq is None:
        return pl.pallas_call(
            _merge_body, grid=(n // tm,), in_specs=in_specs, out_specs=row(D_MODEL), out_shape=out_shape,
            compiler_params=_params("arbitrary"), name="merge",
        )(*args)
    return pl.pallas_call(
        _merge_q_body, grid=(n // tm,),
        in_specs=in_specs + [_pick((2, D_MODEL), l, 2), _pick((D_MODEL, D_MODEL), l)],
        out_specs=[row(D_MODEL), row(D_MODEL)], out_shape=[out_shape, out_shape],
        compiler_params=_params("arbitrary"), name="merge_q",
    )(*args, norm_g, wq)


def _memkv_body(mem_ref, g_ref, wk_ref, wv_ref, k_ref, v_ref, kb_ref, vb_ref):
    mn = _rms(mem_ref[...], g_ref[...]).astype(BF16)
    k = _dot(mn, wk_ref[...])
    v = _dot(mn, wv_ref[...])
    kb_ref[...] = k.astype(BF16)
    vb_ref[...] = v.astype(BF16)
    for h in range(MEM_HEADS):
        for half in range(2):
            cols = slice(h * MEM_HDIM + half * 128, h * MEM_HDIM + (half + 1) * 128)
            k_ref[:, half * MEM_HEADS + h, :] = k[:, cols]
            v_ref[:, half * MEM_HEADS + h, :] = v[:, cols]


def _memkv(mem, g, wk, wv, tm):
    n = mem.shape[0]
    depth = wk.shape[0]
    per_layer = lambda *tail: pl.BlockSpec((None,) + tail, lambda l, j: (l,) + (0,) * len(tail))
    rows = lambda *tail: pl.BlockSpec((None, tm) + tail, lambda l, j: (l, j) + (0,) * len(tail))
    return pl.pallas_call(
        _memkv_body,
        grid=(depth, n // tm),
        in_specs=[pl.BlockSpec((tm, D_MODEL), lambda l, j: (j, 0)), per_layer(1, D_MODEL),
                  per_layer(D_MODEL, D_MODEL), per_layer(D_MODEL, D_MODEL)],
        out_specs=[rows(2 * MEM_HEADS, 128), rows(2 * MEM_HEADS, 128), rows(D_MODEL), rows(D_MODEL)],
        out_shape=[jax.ShapeDtypeStruct((depth, n, 2 * MEM_HEADS, 128), F32),
                   jax.ShapeDtypeStruct((depth, n, 2 * MEM_HEADS, 128), F32),
                   jax.ShapeDtypeStruct((depth, n, D_MODEL), BF16), jax.ShapeDtypeStruct((depth, n, D_MODEL), BF16)],
        compiler_params=_params("arbitrary", "arbitrary"),
        name="memkv",
    )(mem, g, wk, wv)


def _attn_prompt_body(x_ref, k_ref, v_ref, g_ref, wq_ref, wo_ref, o_ref):
    x = x_ref[0]
    xn = _rms(x, g_ref[0:1, :]).astype(BF16)
    q = _dot(xn, wq_ref[...]).astype(BF16)
    outs = []
    for h in range(MEM_HEADS):
        hs = slice(h * MEM_HDIM, (h + 1) * MEM_HDIM)
        s = _dot_nt(q[:, hs], k_ref[:, hs]) * (MEM_HDIM ** -0.5)
        e = jnp.exp(s - jnp.max(s, axis=-1, keepdims=True))
        p = e / jnp.sum(e, axis=-1, keepdims=True)
        outs.append(_dot(p.astype(BF16), v_ref[:, hs]).astype(BF16))
    o = jnp.concatenate(outs, axis=-1)
    y = _dot(o, wo_ref[...])
    o_ref[0] = x + _rms(y, g_ref[1:2, :])


def _attn_prompt(x, kb, vb, norm_g, wq, wo, l, tq):
    b, t, _ = x.shape
    xs = pl.BlockSpec((1, tq, D_MODEL), lambda i, j: (i, j, 0))
    ms = pl.BlockSpec((None, MEM_TOKENS, D_MODEL), lambda i, j: (l, i, 0))
    return pl.pallas_call(
        _attn_prompt_body,
        grid=(b, t // tq),
        in_specs=[xs, ms, ms, _pick((2, D_MODEL), l, 2), _pick((D_MODEL, D_MODEL), l), _pick((D_MODEL, D_MODEL), l)],
        out_specs=xs,
        out_shape=jax.ShapeDtypeStruct((b, t, D_MODEL), F32),
        compiler_params=_params("arbitrary", "arbitrary"),
        name="attn_prompt",
    )(x, kb, vb, norm_g, wq, wo)


def _qproj_body(x_ref, g_ref, wq_ref, q_ref):
    xn = _rms(x_ref[...], g_ref[0:1, :]).astype(BF16)
    q_ref[...] = _dot(xn, wq_ref[...])


def _qproj(x, norm_g, wq, l):
    n = x.shape[0]
    full = pl.BlockSpec((n, D_MODEL), lambda i: (0, 0))
    return pl.pallas_call(
        _qproj_body,
        grid=(1,),
        in_specs=[full, _pick((2, D_MODEL), l, 2), _pick((D_MODEL, D_MODEL), l)],
        out_specs=full,
        out_shape=jax.ShapeDtypeStruct((n, D_MODEL), F32),
        compiler_params=_params("arbitrary"),
        name="qproj",
    )(x, norm_g, wq)


def _attn_step_body(q_ref, k_ref, v_ref, o_ref):
    q = q_ref[...] * (MEM_HDIM ** -0.5)
    part = jnp.sum(k_ref[...] * q, axis=-1, keepdims=True)
    s = part + pltpu.roll(part, MEM_HEADS, 2)
    e = jnp.exp(s - jnp.max(s, axis=1, keepdims=True))
    o_ref[...] = jnp.sum(e * v_ref[...], axis=1, keepdims=True) / jnp.sum(e, axis=1, keepdims=True)


def _attn_step(q, k_all, v_all, l, bb):
    b = q.shape[0]
    qs = pl.BlockSpec((bb, 1, 2 * MEM_HEADS, 128), lambda i: (i, 0, 0, 0))
    ms = pl.BlockSpec((None, bb, MEM_TOKENS, 2 * MEM_HEADS, 128), lambda i: (l, i, 0, 0, 0))
    return pl.pallas_call(
        _attn_step_body,
        grid=(b // bb,),
        in_specs=[qs, ms, ms],
        out_specs=qs,
        out_shape=jax.ShapeDtypeStruct((b, 1, 2 * MEM_HEADS, 128), F32),
        compiler_params=_params("arbitrary"),
        name="attn_step",
    )(q, k_all, v_all)


def _halves_major(a):
    lead = a.shape[:-2]
    return jnp.swapaxes(a.reshape(*lead, MEM_HEADS, 2, 128), -3, -2).reshape(*lead, 2 * MEM_HEADS, 128)


def _heads_major(a):
    lead = a.shape[:-2]
    return jnp.swapaxes(a.reshape(*lead, 2, MEM_HEADS, 128), -3, -2).reshape(*lead, D_MODEL)


def _oproj_body(x_ref, a_ref, g_ref, wo_ref, out_ref):
    y = _dot(a_ref[...].astype(BF16), wo_ref[...])
    out_ref[...] = x_ref[...] + _rms(y, g_ref[1:2, :])


def _oproj(x, a, norm_g, wo, l):
    n = x.shape[0]
    full = pl.BlockSpec((n, D_MODEL), lambda i: (0, 0))
    return pl.pallas_call(
        _oproj_body,
        grid=(1,),
        in_specs=[full, full, _pick((2, D_MODEL), l, 2), _pick((D_MODEL, D_MODEL), l)],
        out_specs=full,
        out_shape=jax.ShapeDtypeStruct((n, D_MODEL), F32),
        compiler_params=_params("arbitrary"),
        name="oproj",
    )(x, a, norm_g, wo)


def _prep_weights(w_ff_in, w_ff_out, w_in, b_igate, b_fgate, w_pool, s_pool, g_mlstm, gmlp_ln_g, gmlp_ln_b,
                  w_s, b_s, w_br_pool, w_br_mlstm, w_br_gmlp, w_out, g_mem, w_mq, w_mk, w_mv, w_mo):
    p = {}
    p["ffn_in"] = w_ff_in.astype(BF16)
    p["ffn_out"] = w_ff_out.astype(BF16)
    wt = jnp.swapaxes(w_in, 1, 2)
    w_if = jnp.pad(wt[:, OFF_IF:OFF_U], ((0, 0), (0, 128 - 2 * HEADS), (0, 0)))
    p["w1t"] = jnp.concatenate([wt[:, :OFF_IF], wt[:, OFF_U:OFF_GATE], w_if], axis=1).astype(BF16)
    p["wgt"] = wt[:, OFF_GATE:].astype(BF16)
    bif = jnp.concatenate([b_igate, b_fgate], axis=1)
    p["bif"] = jnp.pad(bif, ((0, 0), (0, 128 - 2 * HEADS))).reshape(DEPTH, 1, 128)
    same_group = jnp.eye(len(POOL_WINDOWS), dtype=bool)[None, :, None, :, None]
    wbd = jnp.where(same_group, w_pool[:, :, :, None, :], 0.0)
    p["wbd"] = wbd.reshape(DEPTH, POOL_WIDTH, POOL_WIDTH).astype(BF16)
    p["sp"] = s_pool.reshape(DEPTH, 1, POOL_WIDTH)
    p["gm"] = g_mlstm.reshape(DEPTH, 1, MLSTM_WIDTH)
    p["lng"] = gmlp_ln_g.reshape(DEPTH, 1, GMLP_WIDTH)
    p["lnb"] = gmlp_ln_b.reshape(DEPTH, 1, GMLP_WIDTH)
    p["ws"] = w_s.astype(BF16)
    p["bst"] = jnp.swapaxes(b_s, 1, 2)
    p["w00"] = jnp.repeat(w_s[:, :, 0, 0], GMLP_GDIM, axis=1).reshape(DEPTH, 1, GMLP_WIDTH)
    p["b0"] = jnp.repeat(b_s[:, :, 0], GMLP_GDIM, axis=1).reshape(DEPTH, 1, GMLP_WIDTH)
    p["wbp"] = w_br_pool.astype(BF16)
    p["wbm"] = w_br_mlstm.astype(BF16)
    p["wbg"] = w_br_gmlp.astype(BF16)
    p["wout"] = w_out.astype(BF16)
    p["gmem"] = g_mem.reshape(DEPTH, 1, D_MODEL)
    p["wq"] = w_mq.astype(BF16)
    p["wk"] = w_mk.astype(BF16)
    p["wv"] = w_mv.astype(BF16)
    p["wo"] = w_mo.astype(BF16)
    return p


def _prompt_layer(x, kb, vb, norm_g, p, l, batch, seq):
    tm = 1024
    x = _ffn(x, norm_g, p["ffn_in"], p["ffn_out"], l, 0, 0, tm)
    qkv, zo, zif, hp, hg, tail = _inproj_mix(x, norm_g, p, l, tm, seq)
    b3 = lambda a: a.reshape(batch, seq, a.shape[-1])
    hm, c_new, n_new, m_new = _mlstm_chunks(b3(qkv), b3(zif), b3(zo), p["bif"], p["gm"], l, 4)
    x = _merge(x, hp, hm.reshape(batch * seq, MLSTM_WIDTH), hg, norm_g, p["wgt"], p["wbp"], p["wbm"], p["wbg"],
               p["wout"], l, tm)
    x = _attn_prompt(b3(x), kb, vb, norm_g, p["wq"], p["wo"], l, 1024).reshape(batch * seq, D_MODEL)
    x = _ffn(x, norm_g, p["ffn_in"], p["ffn_out"], l, 3, 1, tm)
    return x, tail[:, 16 - POOL_BUF:, :], c_new, n_new.reshape(batch, HEADS, HDIM), m_new.reshape(batch, HEADS)


def _sample_layer(x, norm_g, p, l, pool_t, c_all, n_all, m_all, c_prev, k_all, v_all):
    batch = x.shape[0]
    tm = batch
    x = _ffn(x, norm_g, p["ffn_in"], p["ffn_out"], l, 0, 0, tm)
    zp, qkv, zo, ugv, zif = _inproj(x, norm_g, p["w1t"], l, tm, F32)
    hp, new_buf = _pool_step(zp, pool_t, p["wbd"], p["sp"], l)
    r3 = lambda a: a.reshape(batch, 1, a.shape[-1])
    hm, c_out, n_new, m_new = _mlstm_step(r3(qkv), r3(zif), r3(zo), p["bif"], p["gm"],
                                          c_all, n_all, m_all, c_prev, l, 16)
    hg, vn = _gmlp_step(ugv, p["lng"], p["lnb"], p["w00"], p["b0"], l)
    x, q = _merge(x, hp, hm.reshape(batch, MLSTM_WIDTH), hg, norm_g, p["wgt"], p["wbp"], p["wbm"], p["wbg"],
                  p["wout"], l, tm, wq=p["wq"])
    a = _attn_step(_halves_major(q.reshape(batch, 1, MEM_HEADS, MEM_HDIM)), k_all, v_all, l, 4)
    x = _oproj_ffn(x, _heads_major(a.reshape(batch, 2 * MEM_HEADS, 128)), norm_g, p["wo"],
                   p["ffn_in"], p["ffn_out"], l)
    return x, new_buf, c_out, n_new, m_new.reshape(batch, HEADS), vn.reshape(batch, 1, GMLP_WIDTH)


def kernel(x_prompt, x_sample, state_pool, state_mlstm_C, state_mlstm_n, state_mlstm_m, cache_mem_k, cache_mem_v, mem_prompt, norm_g, w_ff_in, w_ff_out, w_in, b_igate, b_fgate, w_pool, s_pool, g_mlstm, gmlp_ln_g, gmlp_ln_b, w_s, b_s, w_br_pool, w_br_mlstm, w_br_gmlp, w_out, g_mem, w_mq, w_mk, w_mv, w_mo):
    batch, seq, _ = x_prompt.shape
    dec_batch = x_sample.shape[0]
    assert x_sample.shape[1] == 1 and seq % CHUNK == 0 and PAST_LEN % CHUNK == 0 and PAST_LEN >= max(POOL_WINDOWS)
    p = _prep_weights(w_ff_in, w_ff_out, w_in, b_igate, b_fgate, w_pool, s_pool, g_mlstm, gmlp_ln_g, gmlp_ln_b,
                      w_s, b_s, w_br_pool, w_br_mlstm, w_br_gmlp, w_out, g_mem, w_mq, w_mk, w_mv, w_mo)
    xp = x_prompt.reshape(batch * seq, D_MODEL)
    xs = x_sample.reshape(dec_batch, D_MODEL)
    k32, v32, kb, vb = _memkv(mem_prompt.reshape(batch * MEM_TOKENS, D_MODEL), p["gmem"], p["wk"], p["wv"], 512)
    mk = _heads_major(k32).reshape(DEPTH, batch, MEM_TOKENS, MEM_HEADS, MEM_HDIM)
    mv = _heads_major(v32).reshape(DEPTH, batch, MEM_TOKENS, MEM_HEADS, MEM_HDIM)
    pool_t = jnp.swapaxes(state_pool, 1, 2)
    m_all = state_mlstm_m.reshape(DEPTH, dec_batch, 1, HEADS)
    k_dense = _halves_major(cache_mem_k)
    v_dense = _halves_major(cache_mem_v)
    outs = [[] for _ in range(8)]
    c_samp = jnp.zeros(state_mlstm_C.shape, F32)
    for l in range(DEPTH):
        xp, pb, c, n, m = _prompt_layer(xp, kb, vb, norm_g, p, l, batch, seq)
        xs, pbs, c_samp, ns, ms, vr = _sample_layer(xs, norm_g, p, l, pool_t, state_mlstm_C, state_mlstm_n, m_all,
                                                    c_samp, k_dense, v_dense)
        for lst, val in zip(outs, (pb, pbs, c, n, m, ns, ms, vr)):
            lst.append(val)
    pb, pbs, c, n, m, ns, ms, vr = [jnp.stack(o) for o in outs]
    return (xp.reshape(batch, seq, D_MODEL), xs.reshape(dec_batch, 1, D_MODEL),
            pb, jnp.swapaxes(pbs, 1, 2), c, n, m, c_samp, ns, ms, mk, mv, vr)
```

```python
import functools

import jax
import jax.numpy as jnp
from jax import lax
from jax.experimental import pallas as pl
from jax.experimental.pallas import tpu as pltpu

F32 = jnp.float32
BF16 = jnp.bfloat16

D_MODEL = 1024
DEPTH = 4
PAST_LEN = 16384
POOL_WINDOWS = (2, 4, 8, 16)
POOL_WIDTH = 256
POOL_GDIM = 64
POOL_BUF = 15
HEADS = 4
MLSTM_WIDTH = 512
HDIM = 128
CHUNK = 128
GMLP_WIDTH = 256
GMLP_GROUPS = 4
GMLP_GDIM = 64
MEM_TOKENS = 256
MEM_HEADS = 4
MEM_HDIM = 256
D_FF = 2816
FF_CHUNK = 256
N_FF_CHUNKS = D_FF // FF_CHUNK
RMS_EPS = 1e-6
LN_EPS = 1e-5
OFF_Q = POOL_WIDTH
OFF_IF = OFF_Q + 4 * MLSTM_WIDTH
OFF_U = OFF_IF + 2 * HEADS
OFF_GATE = OFF_U + 2 * GMLP_WIDTH
W1_QKV = POOL_WIDTH
W1_O = W1_QKV + 3 * MLSTM_WIDTH
W1_UGV = W1_O + MLSTM_WIDTH
W1_IF = W1_UGV + 2 * GMLP_WIDTH
W1_ROWS = W1_IF + 128

VMEM_LIMIT_BYTES = 56 * 1024 * 1024


def _params(*sem):
    return pltpu.CompilerParams(dimension_semantics=sem, vmem_limit_bytes=VMEM_LIMIT_BYTES)


def _pick(tail, *idx):
    nt = len(tail)
    return pl.BlockSpec((None,) * len(idx) + tuple(tail), lambda *_: tuple(idx) + (0,) * nt,
                        pipeline_mode=pl.Buffered(1))


def _rms(x, g):
    return x * lax.rsqrt(jnp.mean(x * x, axis=-1, keepdims=True) + RMS_EPS) * g


def _dot(a, b):
    return jnp.dot(a, b, preferred_element_type=F32)


def _dot_nt(a, b):
    return lax.dot_general(a, b, (((1,), (1,)), ((), ())), preferred_element_type=F32)


def _dot_tn(a, b):
    return lax.dot_general(a, b, (((0,), (0,)), ((), ())), preferred_element_type=F32)


def _sigmoid(x):
    return 1.0 / (1.0 + jnp.exp(-x))


def _gelu_tanh(x):
    return 0.5 * x * (1.0 + jnp.tanh(0.7978845608028654 * (x + 0.044715 * (x * x * x))))


def _log_sigmoid(x):
    return -(jnp.maximum(-x, 0.0) + jnp.log(1.0 + jnp.exp(-jnp.abs(x))))


def _swiglu_half_step(x, g_ref, wi_ref, wo_ref, acc_ref):
    xn = _rms(x, g_ref[0:1, :]).astype(BF16)
    for c in range(N_FF_CHUNKS):
        cols = slice(c * FF_CHUNK, (c + 1) * FF_CHUNK)
        gate = _dot(xn, wi_ref[:, cols])
        up = _dot(xn, wi_ref[:, D_FF + c * FF_CHUNK:D_FF + (c + 1) * FF_CHUNK])
        act = (gate * _sigmoid(gate) * up).astype(BF16)
        part = _dot(act, wo_ref[cols, :])
        if c == 0:
            acc_ref[...] = part
        else:
            acc_ref[...] += part
    return x + 0.5 * _rms(acc_ref[...], g_ref[1:2, :])


def _ffn_body(x_ref, g_ref, wi_ref, wo_ref, o_ref, acc_ref):
    o_ref[...] = _swiglu_half_step(x_ref[...], g_ref, wi_ref, wo_ref, acc_ref)


def _oproj_ffn_body(x_ref, a_ref, ga_ref, wmo_ref, g_ref, wi_ref, wo_ref, o_ref, acc_ref):
    y = _dot(a_ref[...].astype(BF16), wmo_ref[...])
    x = x_ref[...] + _rms(y, ga_ref[1:2, :])
    o_ref[...] = _swiglu_half_step(x, g_ref, wi_ref, wo_ref, acc_ref)


def _oproj_ffn(x, a, norm_g, wmo, w_in, w_out, l):
    n = x.shape[0]
    full = pl.BlockSpec((n, D_MODEL), lambda i: (0, 0))
    return pl.pallas_call(
        _oproj_ffn_body,
        grid=(1,),
        in_specs=[full, full, _pick((2, D_MODEL), l, 2), _pick((D_MODEL, D_MODEL), l),
                  _pick((2, D_MODEL), l, 3), _pick((D_MODEL, 2 * D_FF), l, 1), _pick((D_FF, D_MODEL), l, 1)],
        out_specs=full,
        out_shape=jax.ShapeDtypeStruct((n, D_MODEL), F32),
        scratch_shapes=[pltpu.VMEM((n, D_MODEL), F32)],
        compiler_params=_params("arbitrary"),
        name="oproj_ffn",
    )(x, a, norm_g, wmo, norm_g, w_in, w_out)


def _ffn(x, norm_g, w_in, w_out, l, sub, half, tm):
    n = x.shape[0]
    return pl.pallas_call(
        _ffn_body,
        grid=(n // tm,),
        in_specs=[
            pl.BlockSpec((tm, D_MODEL), lambda i: (i, 0)),
            _pick((2, D_MODEL), l, sub),
            _pick((D_MODEL, 2 * D_FF), l, half),
            _pick((D_FF, D_MODEL), l, half),
        ],
        out_specs=pl.BlockSpec((tm, D_MODEL), lambda i: (i, 0)),
        out_shape=jax.ShapeDtypeStruct((n, D_MODEL), F32),
        scratch_shapes=[pltpu.VMEM((tm, D_MODEL), F32)],
        compiler_params=_params("arbitrary"),
        name="ffn",
    )(x, norm_g, w_in, w_out)


def _inproj_body(x_ref, g_ref, w_ref, zp_ref, qkv_ref, zo_ref, ugv_ref, zif_ref):
    xn = _rms(x_ref[...], g_ref[0:1, :]).astype(BF16)
    z = _dot_nt(xn, w_ref[...])
    zp_ref[...] = z[:, :W1_QKV]
    q = z[:, W1_QKV:W1_QKV + MLSTM_WIDTH]
    k = z[:, W1_QKV + MLSTM_WIDTH:W1_QKV + 2 * MLSTM_WIDTH] * (HDIM ** -0.5)
    v = z[:, W1_QKV + 2 * MLSTM_WIDTH:W1_O]
    qkv_ref[:, 0:MLSTM_WIDTH] = q.astype(qkv_ref.dtype)
    qkv_ref[:, MLSTM_WIDTH:2 * MLSTM_WIDTH] = k.astype(qkv_ref.dtype)
    qkv_ref[:, 2 * MLSTM_WIDTH:] = v.astype(qkv_ref.dtype)
    zo_ref[...] = z[:, W1_O:W1_UGV]
    ugv_ref[...] = z[:, W1_UGV:W1_IF]
    zif_ref[...] = z[:, W1_IF:]


def _inproj(x, norm_g, w1t, l, tm, qkv_dtype):
    n = x.shape[0]
    row = lambda w: pl.BlockSpec((tm, w), lambda i: (i, 0))
    return pl.pallas_call(
        _inproj_body,
        grid=(n // tm,),
        in_specs=[row(D_MODEL), _pick((2, D_MODEL), l, 1), _pick((W1_ROWS, D_MODEL), l)],
        out_specs=[row(POOL_WIDTH), row(3 * MLSTM_WIDTH), row(MLSTM_WIDTH), row(2 * GMLP_WIDTH), row(128)],
        out_shape=[
            jax.ShapeDtypeStruct((n, POOL_WIDTH), F32),
            jax.ShapeDtypeStruct((n, 3 * MLSTM_WIDTH), qkv_dtype),
            jax.ShapeDtypeStruct((n, MLSTM_WIDTH), F32),
            jax.ShapeDtypeStruct((n, 2 * GMLP_WIDTH), F32),
            jax.ShapeDtypeStruct((n, 128), F32),
        ],
        compiler_params=_params("arbitrary"),
        name="inproj",
    )(x, norm_g, w1t)


def _pool_select(sums, u, count_of):
    lane = lax.broadcasted_iota(jnp.int32, u.shape, u.ndim - 1)
    mean = None
    for gi, w in enumerate(POOL_WINDOWS):
        m = sums[w] * (1.0 / count_of(w))
        mean = m if mean is None else jnp.where(lane >= gi * POOL_GDIM, m, mean)
    return mean - u


POOL_HALO = 2 * max(POOL_WINDOWS)


def _pool_prompt_body(zp_ref, wbd_ref, sp_ref, o_ref, full_ref, p2_ref, p4_ref, p8_ref, *, tt):
    t = pl.program_id(1)
    hl = POOL_HALO
    n = hl + tt

    @pl.when(t == 0)
    def _():
        full_ref[0:hl, :] = jnp.zeros((hl, POOL_WIDTH), F32)

    @pl.when(t > 0)
    def _():
        full_ref[0:hl, :] = full_ref[tt:n, :]

    u = zp_ref[0]
    full_ref[hl:n, :] = u
    p2_ref[8:n, :] = full_ref[8:n, :] + full_ref[7:n - 1, :]
    p4_ref[16:n, :] = p2_ref[16:n, :] + p2_ref[14:n - 2, :]
    p8_ref[24:n, :] = p4_ref[24:n, :] + p4_ref[20:n - 4, :]
    sums = {2: p2_ref[hl:n, :], 4: p4_ref[hl:n, :], 8: p8_ref[hl:n, :],
            16: p8_ref[hl:n, :] + p8_ref[hl - 8:n - 8, :]}
    pos = t * tt + lax.broadcasted_iota(jnp.int32, (tt, 1), 0)
    diff = _pool_select(sums, u, lambda w: jnp.minimum(pos + 1, w).astype(F32))
    y = _dot(diff.astype(BF16), wbd_ref[...]) * sp_ref[...]
    o_ref[0] = y.astype(o_ref.dtype)


def _inproj_mix_body(x_ref, g_ref, w_ref, wbd_ref, sp_ref, lng_ref, lnb_ref, ws_ref, bst_ref,
                     qkv_ref, zo_ref, zif_ref, hp_ref, hg_ref, tail_ref,
                     full_ref, p2_ref, p4_ref, p8_ref, *, tm, tiles_per_seq):
    xn = _rms(x_ref[...], g_ref[0:1, :]).astype(BF16)
    z = _dot_nt(xn, w_ref[...])
    q = z[:, W1_QKV:W1_QKV + MLSTM_WIDTH]
    k = z[:, W1_QKV + MLSTM_WIDTH:W1_QKV + 2 * MLSTM_WIDTH] * (HDIM ** -0.5)
    v = z[:, W1_QKV + 2 * MLSTM_WIDTH:W1_O]
    qkv_ref[:, 0:MLSTM_WIDTH] = q.astype(qkv_ref.dtype)
    qkv_ref[:, MLSTM_WIDTH:2 * MLSTM_WIDTH] = k.astype(qkv_ref.dtype)
    qkv_ref[:, 2 * MLSTM_WIDTH:] = v.astype(qkv_ref.dtype)
    zo_ref[...] = z[:, W1_O:W1_UGV]
    zif_ref[...] = z[:, W1_IF:]

    t = lax.rem(pl.program_id(0), tiles_per_seq)
    hl = POOL_HALO
    n = hl + tm

    @pl.when(t == 0)
    def _():
        full_ref[0:hl, :] = jnp.zeros((hl, POOL_WIDTH), F32)

    @pl.when(t > 0)
    def _():
        full_ref[0:hl, :] = full_ref[tm:n, :]

    u = z[:, :W1_QKV]
    full_ref[hl:n, :] = u
    tail_ref[0] = u[tm - 16:, :]
    p2_ref[8:n, :] = full_ref[8:n, :] + full_ref[7:n - 1, :]
    p4_ref[16:n, :] = p2_ref[16:n, :] + p2_ref[14:n - 2, :]
    p8_ref[24:n, :] = p4_ref[24:n, :] + p4_ref[20:n - 4, :]
    sums = {2: p2_ref[hl:n, :], 4: p4_ref[hl:n, :], 8: p8_ref[hl:n, :],
            16: p8_ref[hl:n, :] + p8_ref[hl - 8:n - 8, :]}
    pos = t * tm + lax.broadcasted_iota(jnp.int32, (tm, 1), 0)
    diff = _pool_select(sums, u, lambda w: jnp.minimum(pos + 1, w).astype(F32))
    hp_ref[...] = (_dot(diff.astype(BF16), wbd_ref[...]) * sp_ref[...]).astype(hp_ref.dtype)

    row = lax.broadcasted_iota(jnp.int32, (CHUNK, CHUNK), 0)
    col = lax.broadcasted_iota(jnp.int32, (CHUNK, CHUNK), 1)
    causal = col <= row
    lane = lax.broadcasted_iota(jnp.int32, (CHUNK, GMLP_WIDTH), 1)
    ws = [jnp.where(causal, ws_ref[g], jnp.zeros((), ws_ref.dtype)) for g in range(GMLP_GROUPS)]
    for c in range(tm // CHUNK):
        rows = slice(c * CHUNK, (c + 1) * CHUNK)
        gu = _gelu_tanh(z[rows, W1_UGV:W1_UGV + GMLP_WIDTH])
        gv = _gelu_tanh(z[rows, W1_UGV + GMLP_WIDTH:W1_IF])
        vn = _layer_norm(gv, lng_ref[...], lnb_ref[...]).astype(BF16)
        mixed = None
        for g in range(GMLP_GROUPS):
            mg = _dot(ws[g], vn) + bst_ref[:, g:g + 1]
            mixed = mg if mixed is None else jnp.where(lane >= g * GMLP_GDIM, mg, mixed)
        hg_ref[rows, :] = (gu * mixed).astype(hg_ref.dtype)


def _inproj_mix(x, norm_g, p, l, tm, seq):
    n = x.shape[0]
    tiles_per_seq = seq // tm
    row = lambda w: pl.BlockSpec((tm, w), lambda i: (i, 0))
    return pl.pallas_call(
        functools.partial(_inproj_mix_body, tm=tm, tiles_per_seq=tiles_per_seq),
        grid=(n // tm,),
        in_specs=[row(D_MODEL), _pick((2, D_MODEL), l, 1), _pick((W1_ROWS, D_MODEL), l),
                  _pick((POOL_WIDTH, POOL_WIDTH), l), _pick((1, POOL_WIDTH), l),
                  _pick((1, GMLP_WIDTH), l), _pick((1, GMLP_WIDTH), l),
                  _pick((GMLP_GROUPS, CHUNK, CHUNK), l), _pick((CHUNK, GMLP_GROUPS), l)],
        out_specs=[row(3 * MLSTM_WIDTH), row(MLSTM_WIDTH), row(128), row(POOL_WIDTH), row(GMLP_WIDTH),
                   pl.BlockSpec((1, 16, POOL_WIDTH), lambda i: (i // tiles_per_seq, 0, 0))],
        out_shape=[
            jax.ShapeDtypeStruct((n, 3 * MLSTM_WIDTH), BF16),
            jax.ShapeDtypeStruct((n, MLSTM_WIDTH), F32),
            jax.ShapeDtypeStruct((n, 128), F32),
            jax.ShapeDtypeStruct((n, POOL_WIDTH), BF16),
            jax.ShapeDtypeStruct((n, GMLP_WIDTH), BF16),
            jax.ShapeDtypeStruct((n // seq, 16, POOL_WIDTH), F32),
        ],
        scratch_shapes=[pltpu.VMEM((tm + POOL_HALO, POOL_WIDTH), F32)] * 4,
        compiler_params=_params("arbitrary"),
        name="inproj_mix",
    )(x, norm_g, p["w1t"], p["wbd"], p["sp"], p["lng"], p["lnb"], p["ws"], p["bst"])


def _pool_prompt(zp, wbd, sp, l, tt):
    b, t, _ = zp.shape
    return pl.pallas_call(
        functools.partial(_pool_prompt_body, tt=tt),
        grid=(b, t // tt),
        in_specs=[
            pl.BlockSpec((1, tt, POOL_WIDTH), lambda i, j: (i, j, 0)),
            _pick((POOL_WIDTH, POOL_WIDTH), l),
            _pick((1, POOL_WIDTH), l),
        ],
        out_specs=pl.BlockSpec((1, tt, POOL_WIDTH), lambda i, j: (i, j, 0)),
        out_shape=jax.ShapeDtypeStruct((b, t, POOL_WIDTH), BF16),
        scratch_shapes=[pltpu.VMEM((tt + POOL_HALO, POOL_WIDTH), F32)] * 4,
        compiler_params=_params("arbitrary", "arbitrary"),
        name="pool_prompt",
    )(zp, wbd, sp)


def _pool_step_body(zp_ref, buf_ref, wbd_ref, sp_ref, o_ref, nbuf_ref):
    u = zp_ref[...]
    acc = u
    sums = {}
    for j in range(1, max(POOL_WINDOWS)):
        acc = acc + buf_ref[POOL_BUF - j]
        if j + 1 in POOL_WINDOWS:
            sums[j + 1] = acc
    diff = _pool_select(sums, u, lambda w: float(w))
    y = _dot(diff.astype(BF16), wbd_ref[...]) * sp_ref[...]
    o_ref[...] = y.astype(o_ref.dtype)
    nbuf_ref[0:POOL_BUF - 1] = buf_ref[1:POOL_BUF]
    nbuf_ref[POOL_BUF - 1] = u


def _pool_step(zp, pool_t, wbd, sp, l):
    b = zp.shape[0]
    full = lambda shape: pl.BlockSpec(shape, lambda i: (0,) * len(shape))
    return pl.pallas_call(
        _pool_step_body,
        grid=(1,),
        in_specs=[full((b, POOL_WIDTH)), _pick((POOL_BUF, b, POOL_WIDTH), l),
                  _pick((POOL_WIDTH, POOL_WIDTH), l), _pick((1, POOL_WIDTH), l)],
        out_specs=[full((b, POOL_WIDTH)), full((POOL_BUF, b, POOL_WIDTH))],
        out_shape=[jax.ShapeDtypeStruct((b, POOL_WIDTH), BF16),
                   jax.ShapeDtypeStruct((POOL_BUF, b, POOL_WIDTH), F32)],
        compiler_params=_params("arbitrary"),
        name="pool_step",
    )(zp, pool_t, wbd, sp)


def _bdot(a, b, ca, cb):
    return lax.dot_general(a, b, (((ca,), (cb,)), ((0,), (0,))), preferred_element_type=F32)


def _mlstm_chunk_body(qkv_ref, zif_ref, zo_ref, bif_ref, gm_ref, h_ref, c_ref, n_ref, m_ref, *, bb):
    @pl.when(pl.program_id(1) == 0)
    def _():
        c_ref[...] = jnp.zeros(c_ref.shape, F32)
        n_ref[...] = jnp.zeros(n_ref.shape, F32)
        m_ref[...] = jnp.zeros(m_ref.shape, F32)

    L = CHUNK
    G = bb * HEADS
    row = lax.broadcasted_iota(jnp.int32, (L, L), 0)
    col = lax.broadcasted_iota(jnp.int32, (L, L), 1)
    causal = col <= row
    tri = jnp.where(causal, 1.0, 0.0).astype(BF16)
    lane = lax.broadcasted_iota(jnp.int32, (L, 128), 1)
    is_forget = (lane >= HEADS) & (lane < 2 * HEADS)
    gls = []
    for bi in range(bb):
        g = zif_ref[bi] + bif_ref[...]
        gls.append(jnp.where(is_forget, _log_sigmoid(g), g))
    gl_all = jnp.concatenate(gls, axis=1)
    hi = gl_all.astype(BF16)
    r1 = gl_all - hi.astype(F32)
    mid = r1.astype(BF16)
    lo = (r1 - mid.astype(F32)).astype(BF16)
    cs_all = _dot(tri, hi) + _dot(tri, mid) + _dot(tri, lo)
    ig_col, b_col, ig_row, b_row, qs, ks, vs, zos = [], [], [], [], [], [], [], []
    for bi in range(bb):
        gl = gls[bi]
        cs = cs_all[:, bi * 128:(bi + 1) * 128]
        gl_t = gl.T
        cs_t = cs.T
        for h in range(HEADS):
            hs = slice(h * HDIM, (h + 1) * HDIM)
            ig_col.append(gl[:, h:h + 1])
            b_col.append(cs[:, HEADS + h:HEADS + h + 1])
            ig_row.append(gl_t[h:h + 1, :])
            b_row.append(cs_t[HEADS + h:HEADS + h + 1, :])
            qs.append(qkv_ref[bi, :, hs])
            ks.append(qkv_ref[bi, :, MLSTM_WIDTH + h * HDIM:MLSTM_WIDTH + (h + 1) * HDIM])
            vs.append(qkv_ref[bi, :, 2 * MLSTM_WIDTH + h * HDIM:2 * MLSTM_WIDTH + (h + 1) * HDIM])
            zos.append(zo_ref[bi, :, hs])
    ig_col, b_col, ig_row, b_row = (jnp.stack(a) for a in (ig_col, b_col, ig_row, b_row))
    q, k, v, zo = (jnp.stack(a) for a in (qs, ks, vs, zos))
    gm = jnp.stack([gm_ref[:, h * HDIM:(h + 1) * HDIM] for h in range(HEADS)] * bb)
    c_old = c_ref[...].reshape(G, HDIM, HDIM)
    n_old = n_ref[...].reshape(G, 1, HDIM)
    m_prev = m_ref[...].reshape(G, 1, 1)

    dmat = jnp.where(causal[None], b_col - b_row + ig_row, -jnp.inf)
    inter = b_col + m_prev
    m_t = jnp.maximum(inter, jnp.max(dmat, axis=-1, keepdims=True))
    w_intra = jnp.exp(dmat - m_t)
    w_inter = jnp.exp(inter - m_t)
    s = _bdot(q, k, 2, 2) * w_intra
    num = _bdot(s.astype(BF16), v, 2, 1) + w_inter * _bdot(q, c_old.astype(BF16), 2, 2)
    qn = jnp.sum(q.astype(F32) * n_old, axis=-1, keepdims=True)
    den = jnp.sum(s, axis=-1, keepdims=True) + w_inter * qn
    hh = num / jnp.maximum(jnp.abs(den), jnp.exp(-m_t))
    hh = hh * lax.rsqrt(jnp.mean(hh * hh, axis=-1, keepdims=True) + RMS_EPS)
    hh = (hh * gm * _sigmoid(zo)).astype(h_ref.dtype)
    for bi in range(bb):
        for h in range(HEADS):
            h_ref[bi, :, h * HDIM:(h + 1) * HDIM] = hh[bi * HEADS + h]
    m_new = m_t[:, L - 1:L, :]
    b_last = b_col[:, L - 1:L, :]
    w_state = jnp.exp(b_last - b_col + ig_col - m_new)
    decay = jnp.exp(b_last + m_prev - m_new)
    vw = (v.astype(F32) * w_state).astype(BF16)
    c_new = decay * c_old + _bdot(vw, k, 1, 1)
    n_new = decay * n_old + jnp.sum(k.astype(F32) * w_state, axis=1, keepdims=True)
    c_ref[...] = c_new.reshape(c_ref.shape)
    n_ref[...] = n_new.reshape(n_ref.shape)
    m_ref[...] = m_new.reshape(m_ref.shape)


def _mlstm_chunks(qkv, zif, zo, bif, gm, l, bb):
    b, t, _ = qkv.shape
    blk = lambda w: pl.BlockSpec((bb, CHUNK, w), lambda i, j: (i, j, 0))
    state = lambda r, c: pl.BlockSpec((bb, HEADS, r, c), lambda i, j: (i, 0, 0, 0))
    return pl.pallas_call(
        functools.partial(_mlstm_chunk_body, bb=bb),
        grid=(b // bb, t // CHUNK),
        in_specs=[blk(3 * MLSTM_WIDTH), blk(128), blk(MLSTM_WIDTH), _pick((1, 128), l), _pick((1, MLSTM_WIDTH), l)],
        out_specs=[blk(MLSTM_WIDTH), state(HDIM, HDIM), state(1, HDIM), state(1, 1)],
        out_shape=[
            jax.ShapeDtypeStruct((b, t, MLSTM_WIDTH), BF16),
            jax.ShapeDtypeStruct((b, HEADS, HDIM, HDIM), F32),
            jax.ShapeDtypeStruct((b, HEADS, 1, HDIM), F32),
            jax.ShapeDtypeStruct((b, HEADS, 1, 1), F32),
        ],
        compiler_params=_params("arbitrary", "arbitrary"),
        name="mlstm_chunks",
    )(qkv, zif, zo, bif, gm)


def _mlstm_step_body(*refs, aliased):
    qkv_ref, zif_ref, zo_ref, bif_ref, gm_ref, c0_ref, n0_ref, m0_ref = refs[:8]
    h_ref, c_ref, n_ref, m_ref = refs[9:] if aliased else refs[8:]
    g = zif_ref[...] + bif_ref[...]
    r = lax.broadcasted_iota(jnp.int32, (HDIM, HDIM), 0)
    c = lax.broadcasted_iota(jnp.int32, (HDIM, HDIM), 1)
    eye = (r == c)[None]
    for h in range(HEADS):
        hs = slice(h * HDIM, (h + 1) * HDIM)
        ig = g[:, :, h:h + 1]
        lf = _log_sigmoid(g[:, :, HEADS + h:HEADS + h + 1])
        m_prev = m0_ref[:, :, h:h + 1]
        q = qkv_ref[:, :, hs]
        k = qkv_ref[:, :, MLSTM_WIDTH + h * HDIM:MLSTM_WIDTH + (h + 1) * HDIM]
        v = qkv_ref[:, :, 2 * MLSTM_WIDTH + h * HDIM:2 * MLSTM_WIDTH + (h + 1) * HDIM]
        c_h = c0_ref[:, h]
        n_h = n0_ref[:, h:h + 1, :]
        inter = lf + m_prev
        m_t = jnp.maximum(inter, ig)
        w_intra = jnp.exp(ig - m_t)
        w_inter = jnp.exp(inter - m_t)
        s = jnp.sum(q * k, axis=-1, keepdims=True) * w_intra
        cq = jnp.sum(c_h * q, axis=-1, keepdims=True)
        v_col = jnp.sum(jnp.where(eye, v, 0.0), axis=-1, keepdims=True)
        num = s * v_col + w_inter * cq
        den = s + w_inter * jnp.sum(n_h * q, axis=-1, keepdims=True)
        hcol = num / jnp.maximum(jnp.abs(den), jnp.exp(-m_t))
        hcol = hcol * lax.rsqrt(jnp.mean(hcol * hcol, axis=1, keepdims=True) + RMS_EPS)
        hrow = jnp.sum(jnp.where(eye, hcol, 0.0), axis=1, keepdims=True)
        hrow = hrow * gm_ref[:, hs] * _sigmoid(zo_ref[:, :, hs])
        h_ref[:, :, hs] = hrow.astype(h_ref.dtype)
        w_state = jnp.exp(ig - m_t)
        decay = jnp.exp(lf + m_prev - m_t)
        c_ref[:, h] = decay * c_h + (w_state * v_col) * k
        n_ref[:, h:h + 1, :] = decay * n_h + w_state * k
        m_ref[:, :, h:h + 1] = m_t


def _mlstm_step(qkv, zif, zo, bif, gm, c_all, n_all, m_all, c_prev, l, bb):
    b = qkv.shape[0]
    blk = lambda w: pl.BlockSpec((bb, 1, w), lambda i: (i, 0, 0))
    st_c = pl.BlockSpec((None, bb, HEADS, HDIM, HDIM), lambda i: (l, i, 0, 0, 0))
    st_n = pl.BlockSpec((None, bb, HEADS, HDIM), lambda i: (l, i, 0, 0))
    st_m = pl.BlockSpec((None, bb, 1, HEADS), lambda i: (l, i, 0, 0))
    in_specs = [blk(3 * MLSTM_WIDTH), blk(128), blk(MLSTM_WIDTH), _pick((1, 128), l), _pick((1, MLSTM_WIDTH), l),
                st_c, st_n, st_m]
    args = [qkv, zif, zo, bif, gm, c_all, n_all, m_all]
    aliases = {}
    if c_prev is not None:
        in_specs.append(pl.BlockSpec(memory_space=pl.ANY))
        args.append(c_prev)
        aliases = {8: 1}
    return pl.pallas_call(
        functools.partial(_mlstm_step_body, aliased=c_prev is not None),
        grid=(b // bb,),
        in_specs=in_specs,
        out_specs=[blk(MLSTM_WIDTH), st_c, pl.BlockSpec((bb, HEADS, HDIM), lambda i: (i, 0, 0)),
                   pl.BlockSpec((bb, 1, HEADS), lambda i: (i, 0, 0))],
        out_shape=[
            jax.ShapeDtypeStruct((b, 1, MLSTM_WIDTH), BF16),
            jax.ShapeDtypeStruct(c_all.shape, F32),
            jax.ShapeDtypeStruct((b, HEADS, HDIM), F32),
            jax.ShapeDtypeStruct((b, 1, HEADS), F32),
        ],
        input_output_aliases=aliases,
        compiler_params=_params("arbitrary"),
        name="mlstm_step",
    )(*args)


def _layer_norm(v, g, b):
    vc = v - jnp.mean(v, axis=-1, keepdims=True)
    return vc * lax.rsqrt(jnp.mean(vc * vc, axis=-1, keepdims=True) + LN_EPS) * g + b


def _gmlp_prompt_body(ugv_ref, lng_ref, lnb_ref, ws_ref, bst_ref, o_ref, *, nchunk):
    row = lax.broadcasted_iota(jnp.int32, (CHUNK, CHUNK), 0)
    col = lax.broadcasted_iota(jnp.int32, (CHUNK, CHUNK), 1)
    causal = col <= row
    lane = lax.broadcasted_iota(jnp.int32, (CHUNK, GMLP_WIDTH), 1)
    ws = [jnp.where(causal, ws_ref[g], jnp.zeros((), ws_ref.dtype)) for g in range(GMLP_GROUPS)]
    for c in range(nchunk):
        rows = slice(c * CHUNK, (c + 1) * CHUNK)
        u = _gelu_tanh(ugv_ref[0, rows, 0:GMLP_WIDTH])
        v = _gelu_tanh(ugv_ref[0, rows, GMLP_WIDTH:])
        vn = _layer_norm(v, lng_ref[...], lnb_ref[...]).astype(BF16)
        mixed = None
        for g in range(GMLP_GROUPS):
            mg = _dot(ws[g], vn) + bst_ref[:, g:g + 1]
            mixed = mg if mixed is None else jnp.where(lane >= g * GMLP_GDIM, mg, mixed)
        o_ref[0, rows, :] = (u * mixed).astype(o_ref.dtype)


def _gmlp_prompt(ugv, lng, lnb, ws, bst, l, tt):
    b, t, _ = ugv.shape
    return pl.pallas_call(
        functools.partial(_gmlp_prompt_body, nchunk=tt // CHUNK),
        grid=(b, t // tt),
        in_specs=[pl.BlockSpec((1, tt, 2 * GMLP_WIDTH), lambda i, j: (i, j, 0)),
                  _pick((1, GMLP_WIDTH), l), _pick((1, GMLP_WIDTH), l),
                  _pick((GMLP_GROUPS, CHUNK, CHUNK), l), _pick((CHUNK, GMLP_GROUPS), l)],
        out_specs=pl.BlockSpec((1, tt, GMLP_WIDTH), lambda i, j: (i, j, 0)),
        out_shape=jax.ShapeDtypeStruct((b, t, GMLP_WIDTH), BF16),
        compiler_params=_params("arbitrary", "arbitrary"),
        name="gmlp_prompt",
    )(ugv, lng, lnb, ws, bst)


def _gmlp_step_body(ugv_ref, lng_ref, lnb_ref, w00_ref, b0_ref, o_ref, vn_ref):
    u = _gelu_tanh(ugv_ref[:, 0:GMLP_WIDTH])
    v = _gelu_tanh(ugv_ref[:, GMLP_WIDTH:])
    vn = _layer_norm(v, lng_ref[...], lnb_ref[...])
    vn_ref[...] = vn
    o_ref[...] = (u * (w00_ref[...] * vn + b0_ref[...])).astype(o_ref.dtype)


def _gmlp_step(ugv, lng, lnb, w00, b0, l):
    b = ugv.shape[0]
    full = lambda w: pl.BlockSpec((b, w), lambda i: (0, 0))
    vec = _pick((1, GMLP_WIDTH), l)
    return pl.pallas_call(
        _gmlp_step_body,
        grid=(1,),
        in_specs=[full(2 * GMLP_WIDTH), vec, vec, vec, vec],
        out_specs=[full(GMLP_WIDTH), full(GMLP_WIDTH)],
        out_shape=[jax.ShapeDtypeStruct((b, GMLP_WIDTH), BF16), jax.ShapeDtypeStruct((b, GMLP_WIDTH), F32)],
        compiler_params=_params("arbitrary"),
        name="gmlp_step",
    )(ugv, lng, lnb, w00, b0)


def _merge_body(x_ref, hp_ref, hm_ref, hg_ref, g_ref, wgt_ref, wbp_ref, wbm_ref, wbg_ref, wout_ref, o_ref):
    x = x_ref[...]
    xn = _rms(x, g_ref[0:1, :]).astype(BF16)
    merged = None
    for i, (h_ref, w_ref) in enumerate(((hp_ref, wbp_ref), (hm_ref, wbm_ref), (hg_ref, wbg_ref))):
        gate = _sigmoid(_dot_nt(xn, wgt_ref[i * D_MODEL:(i + 1) * D_MODEL, :]))
        term = gate * _dot(h_ref[...], w_ref[...])
        merged = term if merged is None else merged + term
    y = _dot(merged.astype(BF16), wout_ref[...])
    o_ref[...] = x + _rms(y, g_ref[1:2, :])


def _merge_q_body(x_ref, hp_ref, hm_ref, hg_ref, g_ref, wgt_ref, wbp_ref, wbm_ref, wbg_ref, wout_ref,
                  ga_ref, wq_ref, o_ref, q_ref):
    _merge_body(x_ref, hp_ref, hm_ref, hg_ref, g_ref, wgt_ref, wbp_ref, wbm_ref, wbg_ref, wout_ref, o_ref)
    xn = _rms(o_ref[...], ga_ref[0:1, :]).astype(BF16)
    q_ref[...] = _dot(xn, wq_ref[...])


def _merge(x, hp, hm, hg, norm_g, wgt, wbp, wbm, wbg, wout, l, tm, wq=None):
    n = x.shape[0]
    row = lambda w: pl.BlockSpec((tm, w), lambda i: (i, 0))
    in_specs = [row(D_MODEL), row(POOL_WIDTH), row(MLSTM_WIDTH), row(GMLP_WIDTH), _pick((2, D_MODEL), l, 1),
                _pick((3 * D_MODEL, D_MODEL), l), _pick((POOL_WIDTH, D_MODEL), l),
                _pick((MLSTM_WIDTH, D_MODEL), l), _pick((GMLP_WIDTH, D_MODEL), l), _pick((D_MODEL, D_MODEL), l)]
    args = [x, hp, hm, hg, norm_g, wgt, wbp, wbm, wbg, wout]
    out_shape = jax.ShapeDtypeStruct((n, D_MODEL), F32)
    if wq is None:
        return pl.pallas_call(
            _merge_body, grid=(n // tm,), in_specs=in_specs, out_specs=row(D_MODEL), out_shape=out_shape,
            compiler_params=_params("arbitrary"), name="merge",
        )(*args)
    return pl.pallas_call(
        _merge_q_body, grid=(n // tm,),
        in_specs=in_specs + [_pick((2, D_MODEL), l, 2), _pick((D_MODEL, D_MODEL), l)],
        out_specs=[row(D_MODEL), row(D_MODEL)], out_shape=[out_shape, out_shape],
        compiler_params=_params("arbitrary"), name="merge_q",
    )(*args, norm_g, wq)


def _memkv_body(mem_ref, g_ref, wk_ref, wv_ref, k_ref, v_ref, kb_ref, vb_ref):
    mn = _rms(mem_ref[...], g_ref[...]).astype(BF16)
    k = _dot(mn, wk_ref[...])
    v = _dot(mn, wv_ref[...])
    kb_ref[...] = k.astype(BF16)
    vb_ref[...] = v.astype(BF16)
    for h in range(MEM_HEADS):
        for half in range(2):
            cols = slice(h * MEM_HDIM + half * 128, h * MEM_HDIM + (half + 1) * 128)
            k_ref[:, half * MEM_HEADS + h, :] = k[:, cols]
            v_ref[:, half * MEM_HEADS + h, :] = v[:, cols]


def _memkv(mem, g, wk, wv, tm):
    n = mem.shape[0]
    depth = wk.shape[0]
    per_layer = lambda *tail: pl.BlockSpec((None,) + tail, lambda l, j: (l,) + (0,) * len(tail))
    rows = lambda *tail: pl.BlockSpec((None, tm) + tail, lambda l, j: (l, j) + (0,) * len(tail))
    return pl.pallas_call(
        _memkv_body,
        grid=(depth, n // tm),
        in_specs=[pl.BlockSpec((tm, D_MODEL), lambda l, j: (j, 0)), per_layer(1, D_MODEL),
                  per_layer(D_MODEL, D_MODEL), per_layer(D_MODEL, D_MODEL)],
        out_specs=[rows(2 * MEM_HEADS, 128), rows(2 * MEM_HEADS, 128), rows(D_MODEL), rows(D_MODEL)],
        out_shape=[jax.ShapeDtypeStruct((depth, n, 2 * MEM_HEADS, 128), F32),
                   jax.ShapeDtypeStruct((depth, n, 2 * MEM_HEADS, 128), F32),
                   jax.ShapeDtypeStruct((depth, n, D_MODEL), BF16), jax.ShapeDtypeStruct((depth, n, D_MODEL), BF16)],
        compiler_params=_params("arbitrary", "arbitrary"),
        name="memkv",
    )(mem, g, wk, wv)


def _attn_prompt_body(x_ref, k_ref, v_ref, g_ref, wq_ref, wo_ref, o_ref):
    x = x_ref[0]
    xn = _rms(x, g_ref[0:1, :]).astype(BF16)
    q = _dot(xn, wq_ref[...]).astype(BF16)
    outs = []
    for h in range(MEM_HEADS):
        hs = slice(h * MEM_HDIM, (h + 1) * MEM_HDIM)
        s = _dot_nt(q[:, hs], k_ref[:, hs]) * (MEM_HDIM ** -0.5)
        e = jnp.exp(s - jnp.max(s, axis=-1, keepdims=True))
        p = e / jnp.sum(e, axis=-1, keepdims=True)
        outs.append(_dot(p.astype(BF16), v_ref[:, hs]).astype(BF16))
    o = jnp.concatenate(outs, axis=-1)
    y = _dot(o, wo_ref[...])
    o_ref[0] = x + _rms(y, g_ref[1:2, :])


def _attn_prompt(x, kb, vb, norm_g, wq, wo, l, tq):
    b, t, _ = x.shape
    xs = pl.BlockSpec((1, tq, D_MODEL), lambda i, j: (i, j, 0))
    ms = pl.BlockSpec((None, MEM_TOKENS, D_MODEL), lambda i, j: (l, i, 0))
    return pl.pallas_call(
        _attn_prompt_body,
        grid=(b, t // tq),
        in_specs=[xs, ms, ms, _pick((2, D_MODEL), l, 2), _pick((D_MODEL, D_MODEL), l), _pick((D_MODEL, D_MODEL), l)],
        out_specs=xs,
        out_shape=jax.ShapeDtypeStruct((b, t, D_MODEL), F32),
        compiler_params=_params("arbitrary", "arbitrary"),
        name="attn_prompt",
    )(x, kb, vb, norm_g, wq, wo)


def _qproj_body(x_ref, g_ref, wq_ref, q_ref):
    xn = _rms(x_ref[...], g_ref[0:1, :]).astype(BF16)
    q_ref[...] = _dot(xn, wq_ref[...])


def _qproj(x, norm_g, wq, l):
    n = x.shape[0]
    full = pl.BlockSpec((n, D_MODEL), lambda i: (0, 0))
    return pl.pallas_call(
        _qproj_body,
        grid=(1,),
        in_specs=[full, _pick((2, D_MODEL), l, 2), _pick((D_MODEL, D_MODEL), l)],
        out_specs=full,
        out_shape=jax.ShapeDtypeStruct((n, D_MODEL), F32),
        compiler_params=_params("arbitrary"),
        name="qproj",
    )(x, norm_g, wq)


def _attn_step_body(q_ref, k_ref, v_ref, o_ref):
    q = q_ref[...] * (MEM_HDIM ** -0.5)
    part = jnp.sum(k_ref[...] * q, axis=-1, keepdims=True)
    s = part + pltpu.roll(part, MEM_HEADS, 2)
    e = jnp.exp(s - jnp.max(s, axis=1, keepdims=True))
    o_ref[...] = jnp.sum(e * v_ref[...], axis=1, keepdims=True) / jnp.sum(e, axis=1, keepdims=True)


def _attn_step(q, k_all, v_all, l, bb):
    b = q.shape[0]
    qs = pl.BlockSpec((bb, 1, 2 * MEM_HEADS, 128), lambda i: (i, 0, 0, 0))
    ms = pl.BlockSpec((None, bb, MEM_TOKENS, 2 * MEM_HEADS, 128), lambda i: (l, i, 0, 0, 0))
    return pl.pallas_call(
        _attn_step_body,
        grid=(b // bb,),
        in_specs=[qs, ms, ms],
        out_specs=qs,
        out_shape=jax.ShapeDtypeStruct((b, 1, 2 * MEM_HEADS, 128), F32),
        compiler_params=_params("arbitrary"),
        name="attn_step",
    )(q, k_all, v_all)


def _ffn_attn_body(x_ref, g_ref, wi_ref, wo_ref, q_ref, k_ref, v_ref, o_ref, a_ref, acc_ref):
    _attn_step_body(q_ref, k_ref, v_ref, a_ref)
    o_ref[...] = _swiglu_half_step(x_ref[...], g_ref, wi_ref, wo_ref, acc_ref)


def _ffn_with_attn(x, norm_g, w_in, w_out, l, sub, half, q, k_all, v_all):
    n = x.shape[0]
    b = q.shape[0]
    steps = 32
    tm, bb = n // steps, b // steps
    qs = pl.BlockSpec((bb, 1, 2 * MEM_HEADS, 128), lambda i: (i, 0, 0, 0))
    ms = pl.BlockSpec((None, bb, MEM_TOKENS, 2 * MEM_HEADS, 128), lambda i: (l, i, 0, 0, 0))
    return pl.pallas_call(
        _ffn_attn_body,
        grid=(steps,),
        in_specs=[pl.BlockSpec((tm, D_MODEL), lambda i: (i, 0)), _pick((2, D_MODEL), l, sub),
                  _pick((D_MODEL, 2 * D_FF), l, half), _pick((D_FF, D_MODEL), l, half), qs, ms, ms],
        out_specs=[pl.BlockSpec((tm, D_MODEL), lambda i: (i, 0)), qs],
        out_shape=[jax.ShapeDtypeStruct((n, D_MODEL), F32),
                   jax.ShapeDtypeStruct((b, 1, 2 * MEM_HEADS, 128), F32)],
        scratch_shapes=[pltpu.VMEM((tm, D_MODEL), F32)],
        compiler_params=_params("arbitrary"),
        name="ffn_attn",
    )(x, norm_g, w_in, w_out, q, k_all, v_all)


def _halves_major(a):
    lead = a.shape[:-2]
    return jnp.swapaxes(a.reshape(*lead, MEM_HEADS, 2, 128), -3, -2).reshape(*lead, 2 * MEM_HEADS, 128)


def _heads_major(a):
    lead = a.shape[:-2]
    return jnp.swapaxes(a.reshape(*lead, 2, MEM_HEADS, 128), -3, -2).reshape(*lead, D_MODEL)


def _oproj_body(x_ref, a_ref, g_ref, wo_ref, out_ref):
    y = _dot(a_ref[...].astype(BF16), wo_ref[...])
    out_ref[...] = x_ref[...] + _rms(y, g_ref[1:2, :])


def _oproj(x, a, norm_g, wo, l):
    n = x.shape[0]
    full = pl.BlockSpec((n, D_MODEL), lambda i: (0, 0))
    return pl.pallas_call(
        _oproj_body,
        grid=(1,),
        in_specs=[full, full, _pick((2, D_MODEL), l, 2), _pick((D_MODEL, D_MODEL), l)],
        out_specs=full,
        out_shape=jax.ShapeDtypeStruct((n, D_MODEL), F32),
        compiler_params=_params("arbitrary"),
        name="oproj",
    )(x, a, norm_g, wo)


def _prep_weights(w_ff_in, w_ff_out, w_in, b_igate, b_fgate, w_pool, s_pool, g_mlstm, gmlp_ln_g, gmlp_ln_b,
                  w_s, b_s, w_br_pool, w_br_mlstm, w_br_gmlp, w_out, g_mem, w_mq, w_mk, w_mv, w_mo):
    p = {}
    p["ffn_in"] = w_ff_in.astype(BF16)
    p["ffn_out"] = w_ff_out.astype(BF16)
    wt = jnp.swapaxes(w_in, 1, 2)
    w_if = jnp.pad(wt[:, OFF_IF:OFF_U], ((0, 0), (0, 128 - 2 * HEADS), (0, 0)))
    p["w1t"] = jnp.concatenate([wt[:, :OFF_IF], wt[:, OFF_U:OFF_GATE], w_if], axis=1).astype(BF16)
    p["wgt"] = wt[:, OFF_GATE:].astype(BF16)
    bif = jnp.concatenate([b_igate, b_fgate], axis=1)
    p["bif"] = jnp.pad(bif, ((0, 0), (0, 128 - 2 * HEADS))).reshape(DEPTH, 1, 128)
    same_group = jnp.eye(len(POOL_WINDOWS), dtype=bool)[None, :, None, :, None]
    wbd = jnp.where(same_group, w_pool[:, :, :, None, :], 0.0)
    p["wbd"] = wbd.reshape(DEPTH, POOL_WIDTH, POOL_WIDTH).astype(BF16)
    p["sp"] = s_pool.reshape(DEPTH, 1, POOL_WIDTH)
    p["gm"] = g_mlstm.reshape(DEPTH, 1, MLSTM_WIDTH)
    p["lng"] = gmlp_ln_g.reshape(DEPTH, 1, GMLP_WIDTH)
    p["lnb"] = gmlp_ln_b.reshape(DEPTH, 1, GMLP_WIDTH)
    p["ws"] = w_s.astype(BF16)
    p["bst"] = jnp.swapaxes(b_s, 1, 2)
    p["w00"] = jnp.repeat(w_s[:, :, 0, 0], GMLP_GDIM, axis=1).reshape(DEPTH, 1, GMLP_WIDTH)
    p["b0"] = jnp.repeat(b_s[:, :, 0], GMLP_GDIM, axis=1).reshape(DEPTH, 1, GMLP_WIDTH)
    p["wbp"] = w_br_pool.astype(BF16)
    p["wbm"] = w_br_mlstm.astype(BF16)
    p["wbg"] = w_br_gmlp.astype(BF16)
    p["wout"] = w_out.astype(BF16)
    p["gmem"] = g_mem.reshape(DEPTH, 1, D_MODEL)
    p["wq"] = w_mq.astype(BF16)
    p["wk"] = w_mk.astype(BF16)
    p["wv"] = w_mv.astype(BF16)
    p["wo"] = w_mo.astype(BF16)
    return p


def _ffn_mlstm_body(x_ref, g_ref, wi_ref, wo_ref, *rest):
    o_ref, acc_ref = rest[9], rest[14]
    _mlstm_step_body(*rest[0:9], *rest[10:14], aliased=True)
    o_ref[...] = _swiglu_half_step(x_ref[...], g_ref, wi_ref, wo_ref, acc_ref)


def _ffn_with_mlstm(x, norm_g, w_in, w_out, l, sub, half, qkv, zif, zo, bif, gm, c_all, n_all, m_all, c_prev):
    n = x.shape[0]
    b = qkv.shape[0]
    steps = 32
    tm, bb = n // steps, b // steps
    blk = lambda w: pl.BlockSpec((bb, 1, w), lambda i: (i, 0, 0))
    st_c = pl.BlockSpec((None, bb, HEADS, HDIM, HDIM), lambda i: (l, i, 0, 0, 0))
    st_n = pl.BlockSpec((None, bb, HEADS, HDIM), lambda i: (l, i, 0, 0))
    st_m = pl.BlockSpec((None, bb, 1, HEADS), lambda i: (l, i, 0, 0))
    row = pl.BlockSpec((tm, D_MODEL), lambda i: (i, 0))
    return pl.pallas_call(
        _ffn_mlstm_body,
        grid=(steps,),
        in_specs=[row, _pick((2, D_MODEL), l, sub), _pick((D_MODEL, 2 * D_FF), l, half), _pick((D_FF, D_MODEL), l, half),
                  blk(3 * MLSTM_WIDTH), blk(128), blk(MLSTM_WIDTH), _pick((1, 128), l), _pick((1, MLSTM_WIDTH), l),
                  st_c, st_n, st_m, pl.BlockSpec(memory_space=pl.ANY)],
        out_specs=[row, blk(MLSTM_WIDTH), st_c, pl.BlockSpec((bb, HEADS, HDIM), lambda i: (i, 0, 0)),
                   pl.BlockSpec((bb, 1, HEADS), lambda i: (i, 0, 0))],
        out_shape=[
            jax.ShapeDtypeStruct((n, D_MODEL), F32),
            jax.ShapeDtypeStruct((b, 1, MLSTM_WIDTH), BF16),
            jax.ShapeDtypeStruct(c_all.shape, F32),
            jax.ShapeDtypeStruct((b, HEADS, HDIM), F32),
            jax.ShapeDtypeStruct((b, 1, HEADS), F32),
        ],
        input_output_aliases={12: 2},
        scratch_shapes=[pltpu.VMEM((tm, D_MODEL), F32)],
        compiler_params=_params("arbitrary"),
        name="ffn_mlstm",
    )(x, norm_g, w_in, w_out, qkv, zif, zo, bif, gm, c_all, n_all, m_all, c_prev)


def _prompt_layer(x, kb, vb, norm_g, p, l, batch, seq):
    tm = 1024
    qkv, zo, zif, hp, hg, tail = _inproj_mix(x, norm_g, p, l, tm, seq)
    b3 = lambda a: a.reshape(batch, seq, a.shape[-1])
    hm, c_new, n_new, m_new = _mlstm_chunks(b3(qkv), b3(zif), b3(zo), p["bif"], p["gm"], l, 4)
    x = _merge(x, hp, hm.reshape(batch * seq, MLSTM_WIDTH), hg, norm_g, p["wgt"], p["wbp"], p["wbm"], p["wbg"],
               p["wout"], l, tm)
    x = _attn_prompt(b3(x), kb, vb, norm_g, p["wq"], p["wo"], l, 1024).reshape(batch * seq, D_MODEL)
    return x, tail[:, 16 - POOL_BUF:, :], c_new, n_new.reshape(batch, HEADS, HDIM), m_new.reshape(batch, HEADS)


def _sample_mix(x, zp, ugv, hm, norm_g, p, l, pool_t):
    batch = x.shape[0]
    hp, new_buf = _pool_step(zp, pool_t, p["wbd"], p["sp"], l)
    hg, vn = _gmlp_step(ugv, p["lng"], p["lnb"], p["w00"], p["b0"], l)
    x, q = _merge(x, hp, hm.reshape(batch, MLSTM_WIDTH), hg, norm_g, p["wgt"], p["wbp"], p["wbm"], p["wbg"],
                  p["wout"], l, batch, wq=p["wq"])
    q8 = _halves_major(q.reshape(batch, 1, MEM_HEADS, MEM_HDIM))
    return x, q8, new_buf, vn.reshape(batch, 1, GMLP_WIDTH)


def kernel(x_prompt, x_sample, state_pool, state_mlstm_C, state_mlstm_n, state_mlstm_m, cache_mem_k, cache_mem_v, mem_prompt, norm_g, w_ff_in, w_ff_out, w_in, b_igate, b_fgate, w_pool, s_pool, g_mlstm, gmlp_ln_g, gmlp_ln_b, w_s, b_s, w_br_pool, w_br_mlstm, w_br_gmlp, w_out, g_mem, w_mq, w_mk, w_mv, w_mo):
    batch, seq, _ = x_prompt.shape
    dec_batch = x_sample.shape[0]
    assert x_sample.shape[1] == 1 and seq % CHUNK == 0 and PAST_LEN % CHUNK == 0 and PAST_LEN >= max(POOL_WINDOWS)
    p = _prep_weights(w_ff_in, w_ff_out, w_in, b_igate, b_fgate, w_pool, s_pool, g_mlstm, gmlp_ln_g, gmlp_ln_b,
                      w_s, b_s, w_br_pool, w_br_mlstm, w_br_gmlp, w_out, g_mem, w_mq, w_mk, w_mv, w_mo)
    xp = x_prompt.reshape(batch * seq, D_MODEL)
    xs = x_sample.reshape(dec_batch, D_MODEL)
    k32, v32, kb, vb = _memkv(mem_prompt.reshape(batch * MEM_TOKENS, D_MODEL), p["gmem"], p["wk"], p["wv"], 512)
    mk = _heads_major(k32).reshape(DEPTH, batch, MEM_TOKENS, MEM_HEADS, MEM_HDIM)
    mv = _heads_major(v32).reshape(DEPTH, batch, MEM_TOKENS, MEM_HEADS, MEM_HDIM)
    pool_t = jnp.swapaxes(state_pool, 1, 2)
    m_all = state_mlstm_m.reshape(DEPTH, dec_batch, 1, HEADS)
    k_dense = _halves_major(cache_mem_k)
    v_dense = _halves_major(cache_mem_v)
    outs = [[] for _ in range(8)]
    c_samp = jnp.zeros(state_mlstm_C.shape, F32)
    for l in range(DEPTH):
        xs = _ffn(xs, norm_g, p["ffn_in"], p["ffn_out"], l, 0, 0, dec_batch)
        zp, qkv, zo, ugv, zif = _inproj(xs, norm_g, p["w1t"], l, dec_batch, F32)
        r3 = lambda a: a.reshape(dec_batch, 1, a.shape[-1])
        xp, hm, c_samp, ns, ms = _ffn_with_mlstm(xp, norm_g, p["ffn_in"], p["ffn_out"], l, 0, 0, r3(qkv), r3(zif),
                                                 r3(zo), p["bif"], p["gm"], state_mlstm_C, state_mlstm_n, m_all,
                                                 c_samp)
        ms = ms.reshape(dec_batch, HEADS)
        xp, pb, c, n, m = _prompt_layer(xp, kb, vb, norm_g, p, l, batch, seq)
        xs, q8, pbs, vr = _sample_mix(xs, zp, ugv, hm, norm_g, p, l, pool_t)
        xp, a8 = _ffn_with_attn(xp, norm_g, p["ffn_in"], p["ffn_out"], l, 3, 1, q8, k_dense, v_dense)
        xs = _oproj_ffn(xs, _heads_major(a8.reshape(dec_batch, 2 * MEM_HEADS, 128)), norm_g, p["wo"],
                        p["ffn_in"], p["ffn_out"], l)
        for lst, val in zip(outs, (pb, pbs, c, n, m, ns, ms, vr)):
            lst.append(val)
    pb, pbs, c, n, m, ns, ms, vr = [jnp.stack(o) for o in outs]
    return (xp.reshape(batch, seq, D_MODEL), xs.reshape(dec_batch, 1, D_MODEL),
            pb, jnp.swapaxes(pbs, 1, 2), c, n, m, c_samp, ns, ms, mk, mv, vr)
```

```python
import functools

import jax
import jax.numpy as jnp
from jax import lax
from jax.experimental import pallas as pl
from jax.experimental.pallas import tpu as pltpu

F32 = jnp.float32
BF16 = jnp.bfloat16

D_MODEL = 1024
DEPTH = 4
PAST_LEN = 16384
POOL_WINDOWS = (2, 4, 8, 16)
POOL_WIDTH = 256
POOL_GDIM = 64
POOL_BUF = 15
HEADS = 4
MLSTM_WIDTH = 512
HDIM = 128
CHUNK = 128
GMLP_WIDTH = 256
GMLP_GROUPS = 4
GMLP_GDIM = 64
MEM_TOKENS = 256
MEM_HEADS = 4
MEM_HDIM = 256
D_FF = 2816
FF_CHUNK = 256
MERGE_CHUNK = 256
N_FF_CHUNKS = D_FF // FF_CHUNK
RMS_EPS = 1e-6
LN_EPS = 1e-5
OFF_Q = POOL_WIDTH
OFF_IF = OFF_Q + 4 * MLSTM_WIDTH
OFF_U = OFF_IF + 2 * HEADS
OFF_GATE = OFF_U + 2 * GMLP_WIDTH
W1_QKV = POOL_WIDTH
W1_O = W1_QKV + 3 * MLSTM_WIDTH
W1_UGV = W1_O + MLSTM_WIDTH
W1_IF = W1_UGV + 2 * GMLP_WIDTH
W1_ROWS = W1_IF + 128

VMEM_LIMIT_BYTES = 56 * 1024 * 1024


def _params(*sem):
    return pltpu.CompilerParams(dimension_semantics=sem, vmem_limit_bytes=VMEM_LIMIT_BYTES)


def _pick(tail, *idx):
    nt = len(tail)
    return pl.BlockSpec((None,) * len(idx) + tuple(tail), lambda *_: tuple(idx) + (0,) * nt,
                        pipeline_mode=pl.Buffered(1))


def _rms(x, g):
    return x * lax.rsqrt(jnp.mean(x * x, axis=-1, keepdims=True) + RMS_EPS) * g


def _dot(a, b):
    return jnp.dot(a, b, preferred_element_type=F32)


def _dot_nt(a, b):
    return lax.dot_general(a, b, (((1,), (1,)), ((), ())), preferred_element_type=F32)


def _sigmoid(x):
    return 1.0 / (1.0 + jnp.exp(-x))


def _gelu_tanh(x):
    return 0.5 * x * (1.0 + jnp.tanh(0.7978845608028654 * (x + 0.044715 * (x * x * x))))


def _log_sigmoid(x):
    return -(jnp.maximum(-x, 0.0) + jnp.log(1.0 + jnp.exp(-jnp.abs(x))))


def _swiglu_half_step(x, g_ref, wi_ref, wo_ref, acc_ref):
    xn = _rms(x, g_ref[0:1, :]).astype(BF16)
    for c in range(N_FF_CHUNKS):
        cols = slice(c * FF_CHUNK, (c + 1) * FF_CHUNK)
        gate = _dot(xn, wi_ref[:, cols])
        up = _dot(xn, wi_ref[:, D_FF + c * FF_CHUNK:D_FF + (c + 1) * FF_CHUNK])
        act = (gate * _sigmoid(gate) * up).astype(BF16)
        part = _dot(act, wo_ref[cols, :])
        if c == 0:
            acc_ref[...] = part
        else:
            acc_ref[...] += part
    return x + 0.5 * _rms(acc_ref[...], g_ref[1:2, :])


def _ffn_body(x_ref, g_ref, wi_ref, wo_ref, o_ref, acc_ref):
    o_ref[...] = _swiglu_half_step(x_ref[...], g_ref, wi_ref, wo_ref, acc_ref)


def _oproj_ffn_body(x_ref, a_ref, ga_ref, wmo_ref, g_ref, wi_ref, wo_ref, o_ref, acc_ref):
    y = _dot(a_ref[...].astype(BF16), wmo_ref[...])
    x = x_ref[...] + _rms(y, ga_ref[1:2, :])
    o_ref[...] = _swiglu_half_step(x, g_ref, wi_ref, wo_ref, acc_ref)


def _oproj_ffn(x, a, norm_g, wmo, w_in, w_out, l):
    n = x.shape[0]
    full = pl.BlockSpec((n, D_MODEL), lambda i: (0, 0))
    return pl.pallas_call(
        _oproj_ffn_body,
        grid=(1,),
        in_specs=[full, full, _pick((2, D_MODEL), l, 2), _pick((D_MODEL, D_MODEL), l),
                  _pick((2, D_MODEL), l, 3), _pick((D_MODEL, 2 * D_FF), l, 1), _pick((D_FF, D_MODEL), l, 1)],
        out_specs=full,
        out_shape=jax.ShapeDtypeStruct((n, D_MODEL), F32),
        scratch_shapes=[pltpu.VMEM((n, D_MODEL), F32)],
        compiler_params=_params("arbitrary"),
        name="oproj_ffn",
    )(x, a, norm_g, wmo, norm_g, w_in, w_out)


def _ffn(x, norm_g, w_in, w_out, l, sub, half, tm):
    n = x.shape[0]
    return pl.pallas_call(
        _ffn_body,
        grid=(n // tm,),
        in_specs=[
            pl.BlockSpec((tm, D_MODEL), lambda i: (i, 0)),
            _pick((2, D_MODEL), l, sub),
            _pick((D_MODEL, 2 * D_FF), l, half),
            _pick((D_FF, D_MODEL), l, half),
        ],
        out_specs=pl.BlockSpec((tm, D_MODEL), lambda i: (i, 0)),
        out_shape=jax.ShapeDtypeStruct((n, D_MODEL), F32),
        scratch_shapes=[pltpu.VMEM((tm, D_MODEL), F32)],
        compiler_params=_params("arbitrary"),
        name="ffn",
    )(x, norm_g, w_in, w_out)


def _inproj_body(x_ref, g_ref, w_ref, zp_ref, qkv_ref, zo_ref, ugv_ref, zif_ref):
    xn = _rms(x_ref[...], g_ref[0:1, :]).astype(BF16)
    z = _dot_nt(xn, w_ref[...])
    zp_ref[...] = z[:, :W1_QKV]
    q = z[:, W1_QKV:W1_QKV + MLSTM_WIDTH]
    k = z[:, W1_QKV + MLSTM_WIDTH:W1_QKV + 2 * MLSTM_WIDTH] * (HDIM ** -0.5)
    v = z[:, W1_QKV + 2 * MLSTM_WIDTH:W1_O]
    qkv_ref[:, 0:MLSTM_WIDTH] = q.astype(qkv_ref.dtype)
    qkv_ref[:, MLSTM_WIDTH:2 * MLSTM_WIDTH] = k.astype(qkv_ref.dtype)
    qkv_ref[:, 2 * MLSTM_WIDTH:] = v.astype(qkv_ref.dtype)
    zo_ref[...] = z[:, W1_O:W1_UGV]
    ugv_ref[...] = z[:, W1_UGV:W1_IF]
    zif_ref[...] = z[:, W1_IF:]


def _inproj(x, norm_g, w1t, l, tm, qkv_dtype):
    n = x.shape[0]
    row = lambda w: pl.BlockSpec((tm, w), lambda i: (i, 0))
    return pl.pallas_call(
        _inproj_body,
        grid=(n // tm,),
        in_specs=[row(D_MODEL), _pick((2, D_MODEL), l, 1), _pick((W1_ROWS, D_MODEL), l)],
        out_specs=[row(POOL_WIDTH), row(3 * MLSTM_WIDTH), row(MLSTM_WIDTH), row(2 * GMLP_WIDTH), row(128)],
        out_shape=[
            jax.ShapeDtypeStruct((n, POOL_WIDTH), F32),
            jax.ShapeDtypeStruct((n, 3 * MLSTM_WIDTH), qkv_dtype),
            jax.ShapeDtypeStruct((n, MLSTM_WIDTH), F32),
            jax.ShapeDtypeStruct((n, 2 * GMLP_WIDTH), F32),
            jax.ShapeDtypeStruct((n, 128), F32),
        ],
        compiler_params=_params("arbitrary"),
        name="inproj",
    )(x, norm_g, w1t)


def _pool_select(sums, u, count_of):
    lane = lax.broadcasted_iota(jnp.int32, u.shape, u.ndim - 1)
    mean = None
    for gi, w in enumerate(POOL_WINDOWS):
        m = sums[w] * (1.0 / count_of(w))
        mean = m if mean is None else jnp.where(lane >= gi * POOL_GDIM, m, mean)
    return mean - u


POOL_HALO = 2 * max(POOL_WINDOWS)


def _inproj_mix_body(x_ref, g_ref, w_ref, wbd_ref, sp_ref, lng_ref, lnb_ref, ws_ref, bst_ref,
                     qkv_ref, zo_ref, zif_ref, hp_ref, hg_ref, tail_ref,
                     full_ref, p2_ref, p4_ref, p8_ref, *, tm, tiles_per_seq):
    xn = _rms(x_ref[...], g_ref[0:1, :]).astype(BF16)
    proj = lambda lo, width: _dot_nt(xn, w_ref[lo:lo + width, :])
    u = proj(0, POOL_WIDTH)
    z_u = proj(W1_UGV, GMLP_WIDTH)
    z_gv = proj(W1_UGV + GMLP_WIDTH, GMLP_WIDTH)
    for c in range(3 * MLSTM_WIDTH // 256):
        part = proj(W1_QKV + c * 256, 256)
        if MLSTM_WIDTH <= c * 256 < 2 * MLSTM_WIDTH:
            part = part * (HDIM ** -0.5)
        qkv_ref[:, c * 256:(c + 1) * 256] = part.astype(qkv_ref.dtype)
    for c in range(MLSTM_WIDTH // 256):
        zo_ref[:, c * 256:(c + 1) * 256] = proj(W1_O + c * 256, 256)
    zif_ref[...] = proj(W1_IF, 128)

    t = lax.rem(pl.program_id(0), tiles_per_seq)
    hl = POOL_HALO
    n = hl + tm

    @pl.when(t == 0)
    def _():
        full_ref[0:hl, :] = jnp.zeros((hl, POOL_WIDTH), F32)

    @pl.when(t > 0)
    def _():
        full_ref[0:hl, :] = full_ref[tm:n, :]

    full_ref[hl:n, :] = u
    tail_ref[0] = u[tm - 16:, :]
    p2_ref[8:n, :] = full_ref[8:n, :] + full_ref[7:n - 1, :]
    p4_ref[16:n, :] = p2_ref[16:n, :] + p2_ref[14:n - 2, :]
    p8_ref[24:n, :] = p4_ref[24:n, :] + p4_ref[20:n - 4, :]
    sums = {2: p2_ref[hl:n, :], 4: p4_ref[hl:n, :], 8: p8_ref[hl:n, :],
            16: p8_ref[hl:n, :] + p8_ref[hl - 8:n - 8, :]}
    pos = t * tm + lax.broadcasted_iota(jnp.int32, (tm, 1), 0)
    diff = _pool_select(sums, u, lambda w: jnp.minimum(pos + 1, w).astype(F32))
    hp_ref[...] = (_dot(diff.astype(BF16), wbd_ref[...]) * sp_ref[...]).astype(hp_ref.dtype)

    row = lax.broadcasted_iota(jnp.int32, (CHUNK, CHUNK), 0)
    col = lax.broadcasted_iota(jnp.int32, (CHUNK, CHUNK), 1)
    causal = col <= row
    lane = lax.broadcasted_iota(jnp.int32, (CHUNK, GMLP_WIDTH), 1)
    ws = [jnp.where(causal, ws_ref[g], jnp.zeros((), ws_ref.dtype)) for g in range(GMLP_GROUPS)]
    for c in range(tm // CHUNK):
        rows = slice(c * CHUNK, (c + 1) * CHUNK)
        gu = _gelu_tanh(z_u[rows, :])
        gv = _gelu_tanh(z_gv[rows, :])
        vn = _layer_norm(gv, lng_ref[...], lnb_ref[...]).astype(BF16)
        mixed = None
        for g in range(GMLP_GROUPS):
            mg = _dot(ws[g], vn) + bst_ref[:, g:g + 1]
            mixed = mg if mixed is None else jnp.where(lane >= g * GMLP_GDIM, mg, mixed)
        hg_ref[rows, :] = (gu * mixed).astype(hg_ref.dtype)


def _inproj_mix(x, norm_g, p, l, tm, seq):
    n = x.shape[0]
    tiles_per_seq = seq // tm
    row = lambda w: pl.BlockSpec((tm, w), lambda i: (i, 0))
    return pl.pallas_call(
        functools.partial(_inproj_mix_body, tm=tm, tiles_per_seq=tiles_per_seq),
        grid=(n // tm,),
        in_specs=[row(D_MODEL), _pick((2, D_MODEL), l, 1), _pick((W1_ROWS, D_MODEL), l),
                  _pick((POOL_WIDTH, POOL_WIDTH), l), _pick((1, POOL_WIDTH), l),
                  _pick((1, GMLP_WIDTH), l), _pick((1, GMLP_WIDTH), l),
                  _pick((GMLP_GROUPS, CHUNK, CHUNK), l), _pick((CHUNK, GMLP_GROUPS), l)],
        out_specs=[row(3 * MLSTM_WIDTH), row(MLSTM_WIDTH), row(128), row(POOL_WIDTH), row(GMLP_WIDTH),
                   pl.BlockSpec((1, 16, POOL_WIDTH), lambda i: (i // tiles_per_seq, 0, 0))],
        out_shape=[
            jax.ShapeDtypeStruct((n, 3 * MLSTM_WIDTH), BF16),
            jax.ShapeDtypeStruct((n, MLSTM_WIDTH), F32),
            jax.ShapeDtypeStruct((n, 128), F32),
            jax.ShapeDtypeStruct((n, POOL_WIDTH), BF16),
            jax.ShapeDtypeStruct((n, GMLP_WIDTH), BF16),
            jax.ShapeDtypeStruct((n // seq, 16, POOL_WIDTH), F32),
        ],
        scratch_shapes=[pltpu.VMEM((tm + POOL_HALO, POOL_WIDTH), F32)] * 4,
        compiler_params=_params("arbitrary"),
        name="inproj_mix",
    )(x, norm_g, p["w1t"], p["wbd"], p["sp"], p["lng"], p["lnb"], p["ws"], p["bst"])


def _pool_step_body(zp_ref, buf_ref, wbd_ref, sp_ref, o_ref, nbuf_ref):
    u = zp_ref[...]
    acc = u
    sums = {}
    for j in range(1, max(POOL_WINDOWS)):
        acc = acc + buf_ref[POOL_BUF - j]
        if j + 1 in POOL_WINDOWS:
            sums[j + 1] = acc
    diff = _pool_select(sums, u, lambda w: float(w))
    y = _dot(diff.astype(BF16), wbd_ref[...]) * sp_ref[...]
    o_ref[...] = y.astype(o_ref.dtype)
    nbuf_ref[0:POOL_BUF - 1] = buf_ref[1:POOL_BUF]
    nbuf_ref[POOL_BUF - 1] = u


def _pool_step(zp, pool_t, wbd, sp, l):
    b = zp.shape[0]
    full = lambda shape: pl.BlockSpec(shape, lambda i: (0,) * len(shape))
    return pl.pallas_call(
        _pool_step_body,
        grid=(1,),
        in_specs=[full((b, POOL_WIDTH)), _pick((POOL_BUF, b, POOL_WIDTH), l),
                  _pick((POOL_WIDTH, POOL_WIDTH), l), _pick((1, POOL_WIDTH), l)],
        out_specs=[full((b, POOL_WIDTH)), full((POOL_BUF, b, POOL_WIDTH))],
        out_shape=[jax.ShapeDtypeStruct((b, POOL_WIDTH), BF16),
                   jax.ShapeDtypeStruct((POOL_BUF, b, POOL_WIDTH), F32)],
        compiler_params=_params("arbitrary"),
        name="pool_step",
    )(zp, pool_t, wbd, sp)


def _bdot(a, b, ca, cb):
    return lax.dot_general(a, b, (((ca,), (cb,)), ((0,), (0,))), preferred_element_type=F32)


def _mlstm_chunk_body(qkv_ref, zif_ref, zo_ref, bif_ref, gm_ref, h_ref, c_ref, n_ref, m_ref, *, bb):
    @pl.when(pl.program_id(1) == 0)
    def _():
        c_ref[...] = jnp.zeros(c_ref.shape, F32)
        n_ref[...] = jnp.zeros(n_ref.shape, F32)
        m_ref[...] = jnp.zeros(m_ref.shape, F32)

    L = CHUNK
    G = bb * HEADS
    row = lax.broadcasted_iota(jnp.int32, (L, L), 0)
    col = lax.broadcasted_iota(jnp.int32, (L, L), 1)
    causal = col <= row
    tri = jnp.where(causal, 1.0, 0.0).astype(BF16)
    lane = lax.broadcasted_iota(jnp.int32, (L, 128), 1)
    is_forget = (lane >= HEADS) & (lane < 2 * HEADS)
    gls = []
    for bi in range(bb):
        g = zif_ref[bi] + bif_ref[...]
        gls.append(jnp.where(is_forget, _log_sigmoid(g), g))
    gl_all = jnp.concatenate(gls, axis=1)
    hi = gl_all.astype(BF16)
    r1 = gl_all - hi.astype(F32)
    mid = r1.astype(BF16)
    lo = (r1 - mid.astype(F32)).astype(BF16)
    cs_all = _dot(tri, hi) + _dot(tri, mid) + _dot(tri, lo)
    ig_col, b_col, ig_row, b_row, qs, ks, vs, zos = [], [], [], [], [], [], [], []
    for bi in range(bb):
        gl = gls[bi]
        cs = cs_all[:, bi * 128:(bi + 1) * 128]
        gl_t = gl.T
        cs_t = cs.T
        for h in range(HEADS):
            hs = slice(h * HDIM, (h + 1) * HDIM)
            ig_col.append(gl[:, h:h + 1])
            b_col.append(cs[:, HEADS + h:HEADS + h + 1])
            ig_row.append(gl_t[h:h + 1, :])
            b_row.append(cs_t[HEADS + h:HEADS + h + 1, :])
            qs.append(qkv_ref[bi, :, hs])
            ks.append(qkv_ref[bi, :, MLSTM_WIDTH + h * HDIM:MLSTM_WIDTH + (h + 1) * HDIM])
            vs.append(qkv_ref[bi, :, 2 * MLSTM_WIDTH + h * HDIM:2 * MLSTM_WIDTH + (h + 1) * HDIM])
            zos.append(zo_ref[bi, :, hs])
    ig_col, b_col, ig_row, b_row = (jnp.stack(a) for a in (ig_col, b_col, ig_row, b_row))
    q, k, v, zo = (jnp.stack(a) for a in (qs, ks, vs, zos))
    gm = jnp.stack([gm_ref[:, h * HDIM:(h + 1) * HDIM] for h in range(HEADS)] * bb)
    c_old = c_ref[...].reshape(G, HDIM, HDIM)
    n_old = n_ref[...].reshape(G, 1, HDIM)
    m_prev = m_ref[...].reshape(G, 1, 1)

    dmat = jnp.where(causal[None], b_col - b_row + ig_row, -jnp.inf)
    inter = b_col + m_prev
    m_t = jnp.maximum(inter, jnp.max(dmat, axis=-1, keepdims=True))
    w_intra = jnp.exp(dmat - m_t)
    w_inter = jnp.exp(inter - m_t)
    s = _bdot(q, k, 2, 2) * w_intra
    num = _bdot(s.astype(BF16), v, 2, 1) + w_inter * _bdot(q, c_old.astype(BF16), 2, 2)
    qn = jnp.sum(q.astype(F32) * n_old, axis=-1, keepdims=True)
    den = jnp.sum(s, axis=-1, keepdims=True) + w_inter * qn
    hh = num / jnp.maximum(jnp.abs(den), jnp.exp(-m_t))
    hh = hh * lax.rsqrt(jnp.mean(hh * hh, axis=-1, keepdims=True) + RMS_EPS)
    hh = (hh * gm * _sigmoid(zo)).astype(h_ref.dtype)
    for bi in range(bb):
        for h in range(HEADS):
            h_ref[bi, :, h * HDIM:(h + 1) * HDIM] = hh[bi * HEADS + h]
    m_new = m_t[:, L - 1:L, :]
    b_last = b_col[:, L - 1:L, :]
    w_state = jnp.exp(b_last - b_col + ig_col - m_new)
    decay = jnp.exp(b_last + m_prev - m_new)
    vw = (v.astype(F32) * w_state).astype(BF16)
    c_new = decay * c_old + _bdot(vw, k, 1, 1)
    n_new = decay * n_old + jnp.sum(k.astype(F32) * w_state, axis=1, keepdims=True)
    c_ref[...] = c_new.reshape(c_ref.shape)
    n_ref[...] = n_new.reshape(n_ref.shape)
    m_ref[...] = m_new.reshape(m_ref.shape)


def _mlstm_chunks(qkv, zif, zo, bif, gm, l, bb):
    b, t, _ = qkv.shape
    blk = lambda w: pl.BlockSpec((bb, CHUNK, w), lambda i, j: (i, j, 0))
    state = lambda r, c: pl.BlockSpec((bb, HEADS, r, c), lambda i, j: (i, 0, 0, 0))
    return pl.pallas_call(
        functools.partial(_mlstm_chunk_body, bb=bb),
        grid=(b // bb, t // CHUNK),
        in_specs=[blk(3 * MLSTM_WIDTH), blk(128), blk(MLSTM_WIDTH), _pick((1, 128), l), _pick((1, MLSTM_WIDTH), l)],
        out_specs=[blk(MLSTM_WIDTH), state(HDIM, HDIM), state(1, HDIM), state(1, 1)],
        out_shape=[
            jax.ShapeDtypeStruct((b, t, MLSTM_WIDTH), BF16),
            jax.ShapeDtypeStruct((b, HEADS, HDIM, HDIM), F32),
            jax.ShapeDtypeStruct((b, HEADS, 1, HDIM), F32),
            jax.ShapeDtypeStruct((b, HEADS, 1, 1), F32),
        ],
        compiler_params=_params("arbitrary", "arbitrary"),
        name="mlstm_chunks",
    )(qkv, zif, zo, bif, gm)


def _mlstm_step_body(qkv_ref, zif_ref, zo_ref, bif_ref, gm_ref, c0_ref, n0_ref, m0_ref, h_ref, c_ref, n_ref, m_ref):
    g = zif_ref[...] + bif_ref[...]
    r = lax.broadcasted_iota(jnp.int32, (HDIM, HDIM), 0)
    c = lax.broadcasted_iota(jnp.int32, (HDIM, HDIM), 1)
    eye = (r == c)[None]
    for h in range(HEADS):
        hs = slice(h * HDIM, (h + 1) * HDIM)
        ig = g[:, :, h:h + 1]
        lf = _log_sigmoid(g[:, :, HEADS + h:HEADS + h + 1])
        m_prev = m0_ref[:, :, h:h + 1]
        q = qkv_ref[:, :, hs]
        k = qkv_ref[:, :, MLSTM_WIDTH + h * HDIM:MLSTM_WIDTH + (h + 1) * HDIM]
        v = qkv_ref[:, :, 2 * MLSTM_WIDTH + h * HDIM:2 * MLSTM_WIDTH + (h + 1) * HDIM]
        c_h = c0_ref[:, h]
        n_h = n0_ref[:, h:h + 1, :]
        inter = lf + m_prev
        m_t = jnp.maximum(inter, ig)
        w_intra = jnp.exp(ig - m_t)
        w_inter = jnp.exp(inter - m_t)
        s = jnp.sum(q * k, axis=-1, keepdims=True) * w_intra
        cq = jnp.sum(c_h * q, axis=-1, keepdims=True)
        v_col = jnp.sum(jnp.where(eye, v, 0.0), axis=-1, keepdims=True)
        num = s * v_col + w_inter * cq
        den = s + w_inter * jnp.sum(n_h * q, axis=-1, keepdims=True)
        hcol = num / jnp.maximum(jnp.abs(den), jnp.exp(-m_t))
        hcol = hcol * lax.rsqrt(jnp.mean(hcol * hcol, axis=1, keepdims=True) + RMS_EPS)
        hrow = jnp.sum(jnp.where(eye, hcol, 0.0), axis=1, keepdims=True)
        hrow = hrow * gm_ref[:, hs] * _sigmoid(zo_ref[:, :, hs])
        h_ref[:, :, hs] = hrow.astype(h_ref.dtype)
        w_state = jnp.exp(ig - m_t)
        decay = jnp.exp(lf + m_prev - m_t)
        c_ref[:, h] = decay * c_h + (w_state * v_col) * k
        n_ref[:, h:h + 1, :] = decay * n_h + w_state * k
        m_ref[:, :, h:h + 1] = m_t


def _layer_norm(v, g, b):
    vc = v - jnp.mean(v, axis=-1, keepdims=True)
    return vc * lax.rsqrt(jnp.mean(vc * vc, axis=-1, keepdims=True) + LN_EPS) * g + b


def _gmlp_step_body(ugv_ref, lng_ref, lnb_ref, w00_ref, b0_ref, o_ref, vn_ref):
    u = _gelu_tanh(ugv_ref[:, 0:GMLP_WIDTH])
    v = _gelu_tanh(ugv_ref[:, GMLP_WIDTH:])
    vn = _layer_norm(v, lng_ref[...], lnb_ref[...])
    vn_ref[...] = vn
    o_ref[...] = (u * (w00_ref[...] * vn + b0_ref[...])).astype(o_ref.dtype)


def _gmlp_step(ugv, lng, lnb, w00, b0, l):
    b = ugv.shape[0]
    full = lambda w: pl.BlockSpec((b, w), lambda i: (0, 0))
    vec = _pick((1, GMLP_WIDTH), l)
    return pl.pallas_call(
        _gmlp_step_body,
        grid=(1,),
        in_specs=[full(2 * GMLP_WIDTH), vec, vec, vec, vec],
        out_specs=[full(GMLP_WIDTH), full(GMLP_WIDTH)],
        out_shape=[jax.ShapeDtypeStruct((b, GMLP_WIDTH), BF16), jax.ShapeDtypeStruct((b, GMLP_WIDTH), F32)],
        compiler_params=_params("arbitrary"),
        name="gmlp_step",
    )(ugv, lng, lnb, w00, b0)


def _merge_body(x_ref, hp_ref, hm_ref, hg_ref, g_ref, wgt_ref, wbp_ref, wbm_ref, wbg_ref, wout_ref, o_ref):
    x = x_ref[...]
    xn = _rms(x, g_ref[0:1, :]).astype(BF16)
    y = None
    for j in range(D_MODEL // MERGE_CHUNK):
        cols = slice(j * MERGE_CHUNK, (j + 1) * MERGE_CHUNK)
        merged = None
        for i, (h_ref, w_ref) in enumerate(((hp_ref, wbp_ref), (hm_ref, wbm_ref), (hg_ref, wbg_ref))):
            gate_rows = slice(i * D_MODEL + j * MERGE_CHUNK, i * D_MODEL + (j + 1) * MERGE_CHUNK)
            gate = _sigmoid(_dot_nt(xn, wgt_ref[gate_rows, :]))
            term = gate * _dot(h_ref[...], w_ref[:, cols])
            merged = term if merged is None else merged + term
        part = _dot(merged.astype(BF16), wout_ref[cols, :])
        y = part if y is None else y + part
    o_ref[...] = x + _rms(y, g_ref[1:2, :])


def _merge_q_body(x_ref, hp_ref, hm_ref, hg_ref, g_ref, wgt_ref, wbp_ref, wbm_ref, wbg_ref, wout_ref,
                  ga_ref, wq_ref, o_ref, q_ref):
    _merge_body(x_ref, hp_ref, hm_ref, hg_ref, g_ref, wgt_ref, wbp_ref, wbm_ref, wbg_ref, wout_ref, o_ref)
    xn = _rms(o_ref[...], ga_ref[0:1, :]).astype(BF16)
    q_ref[...] = _dot(xn, wq_ref[...])


def _merge(x, hp, hm, hg, norm_g, wgt, wbp, wbm, wbg, wout, l, tm, wq=None):
    n = x.shape[0]
    row = lambda w: pl.BlockSpec((tm, w), lambda i: (i, 0))
    in_specs = [row(D_MODEL), row(POOL_WIDTH), row(MLSTM_WIDTH), row(GMLP_WIDTH), _pick((2, D_MODEL), l, 1),
                _pick((3 * D_MODEL, D_MODEL), l), _pick((POOL_WIDTH, D_MODEL), l),
                _pick((MLSTM_WIDTH, D_MODEL), l), _pick((GMLP_WIDTH, D_MODEL), l), _pick((D_MODEL, D_MODEL), l)]
    args = [x, hp, hm, hg, norm_g, wgt, wbp, wbm, wbg, wout]
    out_shape = jax.ShapeDtypeStruct((n, D_MODEL), F32)
    if wq is None:
        return pl.pallas_call(
            _merge_body, grid=(n // tm,), in_specs=in_specs, out_specs=row(D_MODEL), out_shape=out_shape,
            compiler_params=_params("arbitrary"), name="merge",
        )(*args)
    return pl.pallas_call(
        _merge_q_body, grid=(n // tm,),
        in_specs=in_specs + [_pick((2, D_MODEL), l, 2), _pick((D_MODEL, D_MODEL), l)],
        out_specs=[row(D_MODEL), row(D_MODEL)], out_shape=[out_shape, out_shape],
        compiler_params=_params("arbitrary"), name="merge_q",
    )(*args, norm_g, wq)


def _memkv_body(mem_ref, g_ref, wk_ref, wv_ref, k_ref, v_ref, kb_ref, vb_ref):
    mn = _rms(mem_ref[...], g_ref[...]).astype(BF16)
    k = _dot(mn, wk_ref[...])
    v = _dot(mn, wv_ref[...])
    kb_ref[...] = k.astype(BF16)
    vb_ref[...] = v.astype(BF16)
    for h in range(MEM_HEADS):
        for half in range(2):
            cols = slice(h * MEM_HDIM + half * 128, h * MEM_HDIM + (half + 1) * 128)
            k_ref[:, half * MEM_HEADS + h, :] = k[:, cols]
            v_ref[:, half * MEM_HEADS + h, :] = v[:, cols]


def _memkv(mem, g, wk, wv, tm):
    n = mem.shape[0]
    depth = wk.shape[0]
    per_layer = lambda *tail: pl.BlockSpec((None,) + tail, lambda l, j: (l,) + (0,) * len(tail))
    rows = lambda *tail: pl.BlockSpec((None, tm) + tail, lambda l, j: (l, j) + (0,) * len(tail))
    return pl.pallas_call(
        _memkv_body,
        grid=(depth, n // tm),
        in_specs=[pl.BlockSpec((tm, D_MODEL), lambda l, j: (j, 0)), per_layer(1, D_MODEL),
                  per_layer(D_MODEL, D_MODEL), per_layer(D_MODEL, D_MODEL)],
        out_specs=[rows(2 * MEM_HEADS, 128), rows(2 * MEM_HEADS, 128), rows(D_MODEL), rows(D_MODEL)],
        out_shape=[jax.ShapeDtypeStruct((depth, n, 2 * MEM_HEADS, 128), F32),
                   jax.ShapeDtypeStruct((depth, n, 2 * MEM_HEADS, 128), F32),
                   jax.ShapeDtypeStruct((depth, n, D_MODEL), BF16), jax.ShapeDtypeStruct((depth, n, D_MODEL), BF16)],
        compiler_params=_params("arbitrary", "arbitrary"),
        name="memkv",
    )(mem, g, wk, wv)


def _attn_prompt_body(x_ref, k_ref, v_ref, g_ref, wq_ref, wo_ref, o_ref):
    x = x_ref[0]
    xn = _rms(x, g_ref[0:1, :]).astype(BF16)
    q = _dot(xn, wq_ref[...]).astype(BF16)
    outs = []
    for h in range(MEM_HEADS):
        hs = slice(h * MEM_HDIM, (h + 1) * MEM_HDIM)
        s = _dot_nt(q[:, hs], k_ref[:, hs]) * (MEM_HDIM ** -0.5)
        e = jnp.exp(s - jnp.max(s, axis=-1, keepdims=True))
        p = e / jnp.sum(e, axis=-1, keepdims=True)
        outs.append(_dot(p.astype(BF16), v_ref[:, hs]).astype(BF16))
    o = jnp.concatenate(outs, axis=-1)
    y = _dot(o, wo_ref[...])
    o_ref[0] = x + _rms(y, g_ref[1:2, :])


def _attn_prompt(x, kb, vb, norm_g, wq, wo, l, tq):
    b, t, _ = x.shape
    xs = pl.BlockSpec((1, tq, D_MODEL), lambda i, j: (i, j, 0))
    ms = pl.BlockSpec((None, MEM_TOKENS, D_MODEL), lambda i, j: (l, i, 0))
    return pl.pallas_call(
        _attn_prompt_body,
        grid=(b, t // tq),
        in_specs=[xs, ms, ms, _pick((2, D_MODEL), l, 2), _pick((D_MODEL, D_MODEL), l), _pick((D_MODEL, D_MODEL), l)],
        out_specs=xs,
        out_shape=jax.ShapeDtypeStruct((b, t, D_MODEL), F32),
        compiler_params=_params("arbitrary", "arbitrary"),
        name="attn_prompt",
    )(x, kb, vb, norm_g, wq, wo)


def _attn_step_body(q_ref, k_ref, v_ref, o_ref):
    q = q_ref[...] * (MEM_HDIM ** -0.5)
    part = jnp.sum(k_ref[...] * q, axis=-1, keepdims=True)
    s = part + pltpu.roll(part, MEM_HEADS, 2)
    e = jnp.exp(s - jnp.max(s, axis=1, keepdims=True))
    o_ref[...] = jnp.sum(e * v_ref[...], axis=1, keepdims=True) / jnp.sum(e, axis=1, keepdims=True)


def _ffn_attn_body(x_ref, g_ref, wi_ref, wo_ref, q_ref, k_ref, v_ref, o_ref, a_ref, acc_ref):
    _attn_step_body(q_ref, k_ref, v_ref, a_ref)
    o_ref[...] = _swiglu_half_step(x_ref[...], g_ref, wi_ref, wo_ref, acc_ref)


def _ffn_with_attn(x, norm_g, w_in, w_out, l, sub, half, q, k_all, v_all):
    n = x.shape[0]
    b = q.shape[0]
    steps = 32
    tm, bb = n // steps, b // steps
    qs = pl.BlockSpec((bb, 1, 2 * MEM_HEADS, 128), lambda i: (i, 0, 0, 0))
    ms = pl.BlockSpec((None, bb, MEM_TOKENS, 2 * MEM_HEADS, 128), lambda i: (l, i, 0, 0, 0))
    return pl.pallas_call(
        _ffn_attn_body,
        grid=(steps,),
        in_specs=[pl.BlockSpec((tm, D_MODEL), lambda i: (i, 0)), _pick((2, D_MODEL), l, sub),
                  _pick((D_MODEL, 2 * D_FF), l, half), _pick((D_FF, D_MODEL), l, half), qs, ms, ms],
        out_specs=[pl.BlockSpec((tm, D_MODEL), lambda i: (i, 0)), qs],
        out_shape=[jax.ShapeDtypeStruct((n, D_MODEL), F32),
                   jax.ShapeDtypeStruct((b, 1, 2 * MEM_HEADS, 128), F32)],
        scratch_shapes=[pltpu.VMEM((tm, D_MODEL), F32)],
        compiler_params=_params("arbitrary"),
        name="ffn_attn",
    )(x, norm_g, w_in, w_out, q, k_all, v_all)


def _halves_major(a):
    lead = a.shape[:-2]
    return jnp.swapaxes(a.reshape(*lead, MEM_HEADS, 2, 128), -3, -2).reshape(*lead, 2 * MEM_HEADS, 128)


def _heads_major(a):
    lead = a.shape[:-2]
    return jnp.swapaxes(a.reshape(*lead, 2, MEM_HEADS, 128), -3, -2).reshape(*lead, D_MODEL)


def _prep_weights(w_ff_in, w_ff_out, w_in, b_igate, b_fgate, w_pool, s_pool, g_mlstm, gmlp_ln_g, gmlp_ln_b,
                  w_s, b_s, w_br_pool, w_br_mlstm, w_br_gmlp, w_out, g_mem, w_mq, w_mk, w_mv, w_mo):
    p = {}
    p["ffn_in"] = w_ff_in.astype(BF16)
    p["ffn_out"] = w_ff_out.astype(BF16)
    wt = jnp.swapaxes(w_in, 1, 2)
    w_if = jnp.pad(wt[:, OFF_IF:OFF_U], ((0, 0), (0, 128 - 2 * HEADS), (0, 0)))
    p["w1t"] = jnp.concatenate([wt[:, :OFF_IF], wt[:, OFF_U:OFF_GATE], w_if], axis=1).astype(BF16)
    p["wgt"] = wt[:, OFF_GATE:].astype(BF16)
    bif = jnp.concatenate([b_igate, b_fgate], axis=1)
    p["bif"] = jnp.pad(bif, ((0, 0), (0, 128 - 2 * HEADS))).reshape(DEPTH, 1, 128)
    same_group = jnp.eye(len(POOL_WINDOWS), dtype=bool)[None, :, None, :, None]
    wbd = jnp.where(same_group, w_pool[:, :, :, None, :], 0.0)
    p["wbd"] = wbd.reshape(DEPTH, POOL_WIDTH, POOL_WIDTH).astype(BF16)
    p["sp"] = s_pool.reshape(DEPTH, 1, POOL_WIDTH)
    p["gm"] = g_mlstm.reshape(DEPTH, 1, MLSTM_WIDTH)
    p["lng"] = gmlp_ln_g.reshape(DEPTH, 1, GMLP_WIDTH)
    p["lnb"] = gmlp_ln_b.reshape(DEPTH, 1, GMLP_WIDTH)
    p["ws"] = w_s.astype(BF16)
    p["bst"] = jnp.swapaxes(b_s, 1, 2)
    p["w00"] = jnp.repeat(w_s[:, :, 0, 0], GMLP_GDIM, axis=1).reshape(DEPTH, 1, GMLP_WIDTH)
    p["b0"] = jnp.repeat(b_s[:, :, 0], GMLP_GDIM, axis=1).reshape(DEPTH, 1, GMLP_WIDTH)
    p["wbp"] = w_br_pool.astype(BF16)
    p["wbm"] = w_br_mlstm.astype(BF16)
    p["wbg"] = w_br_gmlp.astype(BF16)
    p["wout"] = w_out.astype(BF16)
    p["gmem"] = g_mem.reshape(DEPTH, 1, D_MODEL)
    p["wq"] = w_mq.astype(BF16)
    p["wk"] = w_mk.astype(BF16)
    p["wv"] = w_mv.astype(BF16)
    p["wo"] = w_mo.astype(BF16)
    return p


def _ffn_mlstm_body(x_ref, g_ref, wi_ref, wo_ref, *rest):
    o_ref, acc_ref = rest[9], rest[14]
    _mlstm_step_body(*rest[0:8], *rest[10:14])
    o_ref[...] = _swiglu_half_step(x_ref[...], g_ref, wi_ref, wo_ref, acc_ref)


def _ffn_with_mlstm(x, norm_g, w_in, w_out, l, sub, half, qkv, zif, zo, bif, gm, c_all, n_all, m_all, c_prev):
    n = x.shape[0]
    b = qkv.shape[0]
    steps = 32
    tm, bb = n // steps, b // steps
    blk = lambda w: pl.BlockSpec((bb, 1, w), lambda i: (i, 0, 0))
    st_c = pl.BlockSpec((None, bb, HEADS, HDIM, HDIM), lambda i: (l, i, 0, 0, 0))
    st_n = pl.BlockSpec((None, bb, HEADS, HDIM), lambda i: (l, i, 0, 0))
    st_m = pl.BlockSpec((None, bb, 1, HEADS), lambda i: (l, i, 0, 0))
    row = pl.BlockSpec((tm, D_MODEL), lambda i: (i, 0))
    return pl.pallas_call(
        _ffn_mlstm_body,
        grid=(steps,),
        in_specs=[row, _pick((2, D_MODEL), l, sub), _pick((D_MODEL, 2 * D_FF), l, half), _pick((D_FF, D_MODEL), l, half),
                  blk(3 * MLSTM_WIDTH), blk(128), blk(MLSTM_WIDTH), _pick((1, 128), l), _pick((1, MLSTM_WIDTH), l),
                  st_c, st_n, st_m, pl.BlockSpec(memory_space=pl.ANY)],
        out_specs=[row, blk(MLSTM_WIDTH), st_c, pl.BlockSpec((bb, HEADS, HDIM), lambda i: (i, 0, 0)),
                   pl.BlockSpec((bb, 1, HEADS), lambda i: (i, 0, 0))],
        out_shape=[
            jax.ShapeDtypeStruct((n, D_MODEL), F32),
            jax.ShapeDtypeStruct((b, 1, MLSTM_WIDTH), BF16),
            jax.ShapeDtypeStruct(c_all.shape, F32),
            jax.ShapeDtypeStruct((b, HEADS, HDIM), F32),
            jax.ShapeDtypeStruct((b, 1, HEADS), F32),
        ],
        input_output_aliases={12: 2},
        scratch_shapes=[pltpu.VMEM((tm, D_MODEL), F32)],
        compiler_params=_params("arbitrary"),
        name="ffn_mlstm",
    )(x, norm_g, w_in, w_out, qkv, zif, zo, bif, gm, c_all, n_all, m_all, c_prev)


def _prompt_layer(x, kb, vb, norm_g, p, l, batch, seq):
    tm = 1024
    qkv, zo, zif, hp, hg, tail = _inproj_mix(x, norm_g, p, l, tm, seq)
    b3 = lambda a: a.reshape(batch, seq, a.shape[-1])
    hm, c_new, n_new, m_new = _mlstm_chunks(b3(qkv), b3(zif), b3(zo), p["bif"], p["gm"], l, 8)
    x = _merge(x, hp, hm.reshape(batch * seq, MLSTM_WIDTH), hg, norm_g, p["wgt"], p["wbp"], p["wbm"], p["wbg"],
               p["wout"], l, tm)
    x = _attn_prompt(b3(x), kb, vb, norm_g, p["wq"], p["wo"], l, 1024).reshape(batch * seq, D_MODEL)
    return x, tail[:, 16 - POOL_BUF:, :], c_new, n_new.reshape(batch, HEADS, HDIM), m_new.reshape(batch, HEADS)


def _sample_mix(x, zp, ugv, hm, norm_g, p, l, pool_t):
    batch = x.shape[0]
    hp, new_buf = _pool_step(zp, pool_t, p["wbd"], p["sp"], l)
    hg, vn = _gmlp_step(ugv, p["lng"], p["lnb"], p["w00"], p["b0"], l)
    x, q = _merge(x, hp, hm.reshape(batch, MLSTM_WIDTH), hg, norm_g, p["wgt"], p["wbp"], p["wbm"], p["wbg"],
                  p["wout"], l, batch, wq=p["wq"])
    q8 = _halves_major(q.reshape(batch, 1, MEM_HEADS, MEM_HDIM))
    return x, q8, new_buf, vn.reshape(batch, 1, GMLP_WIDTH)


def kernel(x_prompt, x_sample, state_pool, state_mlstm_C, state_mlstm_n, state_mlstm_m, cache_mem_k, cache_mem_v, mem_prompt, norm_g, w_ff_in, w_ff_out, w_in, b_igate, b_fgate, w_pool, s_pool, g_mlstm, gmlp_ln_g, gmlp_ln_b, w_s, b_s, w_br_pool, w_br_mlstm, w_br_gmlp, w_out, g_mem, w_mq, w_mk, w_mv, w_mo):
    batch, seq, _ = x_prompt.shape
    dec_batch = x_sample.shape[0]
    assert x_sample.shape[1] == 1 and seq % CHUNK == 0 and PAST_LEN % CHUNK == 0 and PAST_LEN >= max(POOL_WINDOWS)
    p = _prep_weights(w_ff_in, w_ff_out, w_in, b_igate, b_fgate, w_pool, s_pool, g_mlstm, gmlp_ln_g, gmlp_ln_b,
                      w_s, b_s, w_br_pool, w_br_mlstm, w_br_gmlp, w_out, g_mem, w_mq, w_mk, w_mv, w_mo)
    xp = x_prompt.reshape(batch * seq, D_MODEL)
    xs = x_sample.reshape(dec_batch, D_MODEL)
    k32, v32, kb, vb = _memkv(mem_prompt.reshape(batch * MEM_TOKENS, D_MODEL), p["gmem"], p["wk"], p["wv"], 512)
    mk = _heads_major(k32).reshape(DEPTH, batch, MEM_TOKENS, MEM_HEADS, MEM_HDIM)
    mv = _heads_major(v32).reshape(DEPTH, batch, MEM_TOKENS, MEM_HEADS, MEM_HDIM)
    pool_t = jnp.swapaxes(state_pool, 1, 2)
    m_all = state_mlstm_m.reshape(DEPTH, dec_batch, 1, HEADS)
    k_dense = _halves_major(cache_mem_k)
    v_dense = _halves_major(cache_mem_v)
    outs = [[] for _ in range(8)]
    c_samp = jnp.zeros(state_mlstm_C.shape, F32)
    for l in range(DEPTH):
        xs = _ffn(xs, norm_g, p["ffn_in"], p["ffn_out"], l, 0, 0, dec_batch)
        zp, qkv, zo, ugv, zif = _inproj(xs, norm_g, p["w1t"], l, dec_batch, F32)
        r3 = lambda a: a.reshape(dec_batch, 1, a.shape[-1])
        xp, hm, c_samp, ns, ms = _ffn_with_mlstm(xp, norm_g, p["ffn_in"], p["ffn_out"], l, 0, 0, r3(qkv), r3(zif),
                                                 r3(zo), p["bif"], p["gm"], state_mlstm_C, state_mlstm_n, m_all,
                                                 c_samp)
        ms = ms.reshape(dec_batch, HEADS)
        xp, pb, c, n, m = _prompt_layer(xp, kb, vb, norm_g, p, l, batch, seq)
        xs, q8, pbs, vr = _sample_mix(xs, zp, ugv, hm, norm_g, p, l, pool_t)
        xp, a8 = _ffn_with_attn(xp, norm_g, p["ffn_in"], p["ffn_out"], l, 3, 1, q8, k_dense, v_dense)
        xs = _oproj_ffn(xs, _heads_major(a8.reshape(dec_batch, 2 * MEM_HEADS, 128)), norm_g, p["wo"],
                        p["ffn_in"], p["ffn_out"], l)
        for lst, val in zip(outs, (pb, pbs, c, n, m, ns, ms, vr)):
            lst.append(val)
    pb, pbs, c, n, m, ns, ms, vr = [jnp.stack(o) for o in outs]
    return (xp.reshape(batch, seq, D_MODEL), xs.reshape(dec_batch, 1, D_MODEL),
            pb, jnp.swapaxes(pbs, 1, 2), c, n, m, c_samp, ns, ms, mk, mv, vr)
```

```python
import functools

import jax
import jax.numpy as jnp
from jax import lax
from jax.experimental import pallas as pl
from jax.experimental.pallas import tpu as pltpu

F32 = jnp.float32
BF16 = jnp.bfloat16

D_MODEL = 1024
DEPTH = 4
PAST_LEN = 16384
POOL_WINDOWS = (2, 4, 8, 16)
POOL_WIDTH = 256
POOL_GDIM = 64
POOL_BUF = 15
HEADS = 4
MLSTM_WIDTH = 512
HDIM = 128
CHUNK = 128
GMLP_WIDTH = 256
GMLP_GROUPS = 4
GMLP_GDIM = 64
MEM_TOKENS = 256
MEM_HEADS = 4
MEM_HDIM = 256
D_FF = 2816
FF_CHUNK = 256
MERGE_CHUNK = 256
N_FF_CHUNKS = D_FF // FF_CHUNK
RMS_EPS = 1e-6
LN_EPS = 1e-5
OFF_Q = POOL_WIDTH
OFF_IF = OFF_Q + 4 * MLSTM_WIDTH
OFF_U = OFF_IF + 2 * HEADS
OFF_GATE = OFF_U + 2 * GMLP_WIDTH
W1_QKV = POOL_WIDTH
W1_O = W1_QKV + 3 * MLSTM_WIDTH
W1_UGV = W1_O + MLSTM_WIDTH
W1_IF = W1_UGV + 2 * GMLP_WIDTH
W1_ROWS = W1_IF + 128

VMEM_LIMIT_BYTES = 56 * 1024 * 1024


def _params(*sem):
    return pltpu.CompilerParams(dimension_semantics=sem, vmem_limit_bytes=VMEM_LIMIT_BYTES)


def _pick(tail, *idx):
    nt = len(tail)
    return pl.BlockSpec((None,) * len(idx) + tuple(tail), lambda *_: tuple(idx) + (0,) * nt,
                        pipeline_mode=pl.Buffered(1))


def _rms(x, g):
    return x * lax.rsqrt(jnp.mean(x * x, axis=-1, keepdims=True) + RMS_EPS) * g


def _dot(a, b):
    return jnp.dot(a, b, preferred_element_type=F32)


def _dot_nt(a, b):
    return lax.dot_general(a, b, (((1,), (1,)), ((), ())), preferred_element_type=F32)


def _sigmoid(x):
    return 1.0 / (1.0 + jnp.exp(-x))


def _gelu_tanh(x):
    return 0.5 * x * (1.0 + jnp.tanh(0.7978845608028654 * (x + 0.044715 * (x * x * x))))


def _log_sigmoid(x):
    return -(jnp.maximum(-x, 0.0) + jnp.log(1.0 + jnp.exp(-jnp.abs(x))))


def _swiglu_half_step(x, g_ref, wi_ref, wo_ref, acc_ref):
    xn = _rms(x, g_ref[0:1, :]).astype(BF16)
    for c in range(N_FF_CHUNKS):
        cols = slice(c * FF_CHUNK, (c + 1) * FF_CHUNK)
        gate = _dot(xn, wi_ref[:, cols])
        up = _dot(xn, wi_ref[:, D_FF + c * FF_CHUNK:D_FF + (c + 1) * FF_CHUNK])
        act = (gate * _sigmoid(gate) * up).astype(BF16)
        part = _dot(act, wo_ref[cols, :])
        if c == 0:
            acc_ref[...] = part
        else:
            acc_ref[...] += part
    return x + 0.5 * _rms(acc_ref[...], g_ref[1:2, :])


def _oproj_ffn_body(x_ref, a_ref, ga_ref, wmo_ref, g_ref, wi_ref, wo_ref, o_ref, acc_ref):
    y = _dot(a_ref[...].astype(BF16), wmo_ref[...])
    x = x_ref[...] + _rms(y, ga_ref[1:2, :])
    o_ref[...] = _swiglu_half_step(x, g_ref, wi_ref, wo_ref, acc_ref)


def _oproj_ffn(x, a, norm_g, wmo, w_in, w_out, l):
    n = x.shape[0]
    full = pl.BlockSpec((n, D_MODEL), lambda i: (0, 0))
    return pl.pallas_call(
        _oproj_ffn_body,
        grid=(1,),
        in_specs=[full, full, _pick((2, D_MODEL), l, 2), _pick((D_MODEL, D_MODEL), l),
                  _pick((2, D_MODEL), l, 3), _pick((D_MODEL, 2 * D_FF), l, 1), _pick((D_FF, D_MODEL), l, 1)],
        out_specs=full,
        out_shape=jax.ShapeDtypeStruct((n, D_MODEL), F32),
        scratch_shapes=[pltpu.VMEM((n, D_MODEL), F32)],
        compiler_params=_params("arbitrary"),
        name="oproj_ffn",
    )(x, a, norm_g, wmo, norm_g, w_in, w_out)


def _inproj_body(x_ref, g_ref, w_ref, zp_ref, qkv_ref, zo_ref, ugv_ref, zif_ref):
    xn = _rms(x_ref[...], g_ref[0:1, :]).astype(BF16)
    z = _dot_nt(xn, w_ref[...])
    zp_ref[...] = z[:, :W1_QKV]
    q = z[:, W1_QKV:W1_QKV + MLSTM_WIDTH]
    k = z[:, W1_QKV + MLSTM_WIDTH:W1_QKV + 2 * MLSTM_WIDTH] * (HDIM ** -0.5)
    v = z[:, W1_QKV + 2 * MLSTM_WIDTH:W1_O]
    qkv_ref[:, 0:MLSTM_WIDTH] = q.astype(qkv_ref.dtype)
    qkv_ref[:, MLSTM_WIDTH:2 * MLSTM_WIDTH] = k.astype(qkv_ref.dtype)
    qkv_ref[:, 2 * MLSTM_WIDTH:] = v.astype(qkv_ref.dtype)
    zo_ref[...] = z[:, W1_O:W1_UGV]
    ugv_ref[...] = z[:, W1_UGV:W1_IF]
    zif_ref[...] = z[:, W1_IF:]


def _ffn_inproj_body(x_ref, g_ref, wi_ref, wo_ref, g1_ref, w1_ref,
                     o_ref, zp_ref, qkv_ref, zo_ref, ugv_ref, zif_ref, acc_ref):
    o_ref[...] = _swiglu_half_step(x_ref[...], g_ref, wi_ref, wo_ref, acc_ref)
    _inproj_body(o_ref, g1_ref, w1_ref, zp_ref, qkv_ref, zo_ref, ugv_ref, zif_ref)


def _ffn_inproj(x, norm_g, w_in, w_out, w1t, l):
    n = x.shape[0]
    full = lambda w: pl.BlockSpec((n, w), lambda i: (0, 0))
    widths = (D_MODEL, POOL_WIDTH, 3 * MLSTM_WIDTH, MLSTM_WIDTH, 2 * GMLP_WIDTH, 128)
    return pl.pallas_call(
        _ffn_inproj_body,
        grid=(1,),
        in_specs=[full(D_MODEL), _pick((2, D_MODEL), l, 0), _pick((D_MODEL, 2 * D_FF), l, 0),
                  _pick((D_FF, D_MODEL), l, 0), _pick((2, D_MODEL), l, 1), _pick((W1_ROWS, D_MODEL), l)],
        out_specs=[full(w) for w in widths],
        out_shape=[jax.ShapeDtypeStruct((n, w), F32) for w in widths],
        scratch_shapes=[pltpu.VMEM((n, D_MODEL), F32)],
        compiler_params=_params("arbitrary"),
        name="ffn_inproj",
    )(x, norm_g, w_in, w_out, norm_g, w1t)


def _pool_select(sums, u, count_of):
    lane = lax.broadcasted_iota(jnp.int32, u.shape, u.ndim - 1)
    mean = None
    for gi, w in enumerate(POOL_WINDOWS):
        m = sums[w] * (1.0 / count_of(w))
        mean = m if mean is None else jnp.where(lane >= gi * POOL_GDIM, m, mean)
    return mean - u


POOL_HALO = 2 * max(POOL_WINDOWS)


def _inproj_mix_body(x_ref, g_ref, w_ref, wbd_ref, sp_ref, lng_ref, lnb_ref, ws_ref, bst_ref,
                     qkv_ref, zo_ref, zif_ref, hp_ref, hg_ref, tail_ref,
                     full_ref, p2_ref, p4_ref, p8_ref, *, tm, tiles_per_seq):
    xn = _rms(x_ref[...], g_ref[0:1, :]).astype(BF16)
    proj = lambda lo, width: _dot_nt(xn, w_ref[lo:lo + width, :])
    u = proj(0, POOL_WIDTH)
    z_u = proj(W1_UGV, GMLP_WIDTH)
    z_gv = proj(W1_UGV + GMLP_WIDTH, GMLP_WIDTH)
    for c in range(3 * MLSTM_WIDTH // 256):
        part = proj(W1_QKV + c * 256, 256)
        if MLSTM_WIDTH <= c * 256 < 2 * MLSTM_WIDTH:
            part = part * (HDIM ** -0.5)
        qkv_ref[:, c * 256:(c + 1) * 256] = part.astype(qkv_ref.dtype)
    for c in range(MLSTM_WIDTH // 256):
        zo_ref[:, c * 256:(c + 1) * 256] = proj(W1_O + c * 256, 256)
    zif_ref[...] = proj(W1_IF, 128)

    t = lax.rem(pl.program_id(0), tiles_per_seq)
    hl = POOL_HALO
    n = hl + tm

    @pl.when(t == 0)
    def _():
        full_ref[0:hl, :] = jnp.zeros((hl, POOL_WIDTH), F32)

    @pl.when(t > 0)
    def _():
        full_ref[0:hl, :] = full_ref[tm:n, :]

    full_ref[hl:n, :] = u
    tail_ref[0] = u[tm - 16:, :]
    p2_ref[8:n, :] = full_ref[8:n, :] + full_ref[7:n - 1, :]
    p4_ref[16:n, :] = p2_ref[16:n, :] + p2_ref[14:n - 2, :]
    p8_ref[24:n, :] = p4_ref[24:n, :] + p4_ref[20:n - 4, :]
    sums = {2: p2_ref[hl:n, :], 4: p4_ref[hl:n, :], 8: p8_ref[hl:n, :],
            16: p8_ref[hl:n, :] + p8_ref[hl - 8:n - 8, :]}
    pos = t * tm + lax.broadcasted_iota(jnp.int32, (tm, 1), 0)
    diff = _pool_select(sums, u, lambda w: jnp.minimum(pos + 1, w).astype(F32))
    hp_ref[...] = (_dot(diff.astype(BF16), wbd_ref[...]) * sp_ref[...]).astype(hp_ref.dtype)

    row = lax.broadcasted_iota(jnp.int32, (CHUNK, CHUNK), 0)
    col = lax.broadcasted_iota(jnp.int32, (CHUNK, CHUNK), 1)
    causal = col <= row
    lane = lax.broadcasted_iota(jnp.int32, (CHUNK, GMLP_WIDTH), 1)
    ws = [jnp.where(causal, ws_ref[g], jnp.zeros((), ws_ref.dtype)) for g in range(GMLP_GROUPS)]
    for c in range(tm // CHUNK):
        rows = slice(c * CHUNK, (c + 1) * CHUNK)
        gu = _gelu_tanh(z_u[rows, :])
        gv = _gelu_tanh(z_gv[rows, :])
        vn = _layer_norm(gv, lng_ref[...], lnb_ref[...]).astype(BF16)
        mixed = None
        for g in range(GMLP_GROUPS):
            mg = _dot(ws[g], vn) + bst_ref[:, g:g + 1]
            mixed = mg if mixed is None else jnp.where(lane >= g * GMLP_GDIM, mg, mixed)
        hg_ref[rows, :] = (gu * mixed).astype(hg_ref.dtype)


def _inproj_mix(x, norm_g, p, l, tm, seq):
    n = x.shape[0]
    tiles_per_seq = seq // tm
    row = lambda w: pl.BlockSpec((tm, w), lambda i: (i, 0))
    return pl.pallas_call(
        functools.partial(_inproj_mix_body, tm=tm, tiles_per_seq=tiles_per_seq),
        grid=(n // tm,),
        in_specs=[row(D_MODEL), _pick((2, D_MODEL), l, 1), _pick((W1_ROWS, D_MODEL), l),
                  _pick((POOL_WIDTH, POOL_WIDTH), l), _pick((1, POOL_WIDTH), l),
                  _pick((1, GMLP_WIDTH), l), _pick((1, GMLP_WIDTH), l),
                  _pick((GMLP_GROUPS, CHUNK, CHUNK), l), _pick((CHUNK, GMLP_GROUPS), l)],
        out_specs=[row(3 * MLSTM_WIDTH), row(MLSTM_WIDTH), row(128), row(POOL_WIDTH), row(GMLP_WIDTH),
                   pl.BlockSpec((1, 16, POOL_WIDTH), lambda i: (i // tiles_per_seq, 0, 0))],
        out_shape=[
            jax.ShapeDtypeStruct((n, 3 * MLSTM_WIDTH), BF16),
            jax.ShapeDtypeStruct((n, MLSTM_WIDTH), F32),
            jax.ShapeDtypeStruct((n, 128), F32),
            jax.ShapeDtypeStruct((n, POOL_WIDTH), BF16),
            jax.ShapeDtypeStruct((n, GMLP_WIDTH), BF16),
            jax.ShapeDtypeStruct((n // seq, 16, POOL_WIDTH), F32),
        ],
        scratch_shapes=[pltpu.VMEM((tm + POOL_HALO, POOL_WIDTH), F32)] * 4,
        compiler_params=_params("arbitrary"),
        name="inproj_mix",
    )(x, norm_g, p["w1t"], p["wbd"], p["sp"], p["lng"], p["lnb"], p["ws"], p["bst"])


def _pool_step_body(zp_ref, buf_ref, wbd_ref, sp_ref, o_ref, nbuf_ref):
    u = zp_ref[...]
    acc = u
    sums = {}
    for j in range(1, max(POOL_WINDOWS)):
        acc = acc + buf_ref[POOL_BUF - j]
        if j + 1 in POOL_WINDOWS:
            sums[j + 1] = acc
    diff = _pool_select(sums, u, lambda w: float(w))
    y = _dot(diff.astype(BF16), wbd_ref[...]) * sp_ref[...]
    o_ref[...] = y.astype(o_ref.dtype)
    nbuf_ref[0:POOL_BUF - 1] = buf_ref[1:POOL_BUF]
    nbuf_ref[POOL_BUF - 1] = u


def _bdot(a, b, ca, cb):
    return lax.dot_general(a, b, (((ca,), (cb,)), ((0,), (0,))), preferred_element_type=F32)


def _mlstm_chunk_body(qkv_ref, zif_ref, zo_ref, bif_ref, gm_ref, h_ref, c_ref, n_ref, m_ref, *, bb):
    @pl.when(pl.program_id(1) == 0)
    def _():
        c_ref[...] = jnp.zeros(c_ref.shape, F32)
        n_ref[...] = jnp.zeros(n_ref.shape, F32)
        m_ref[...] = jnp.zeros(m_ref.shape, F32)

    L = CHUNK
    G = bb * HEADS
    row = lax.broadcasted_iota(jnp.int32, (L, L), 0)
    col = lax.broadcasted_iota(jnp.int32, (L, L), 1)
    causal = col <= row
    tri = jnp.where(causal, 1.0, 0.0).astype(BF16)
    lane = lax.broadcasted_iota(jnp.int32, (L, 128), 1)
    is_forget = (lane >= HEADS) & (lane < 2 * HEADS)
    gls = []
    for bi in range(bb):
        g = zif_ref[bi] + bif_ref[...]
        gls.append(jnp.where(is_forget, _log_sigmoid(g), g))
    gl_all = jnp.concatenate(gls, axis=1)
    hi = gl_all.astype(BF16)
    r1 = gl_all - hi.astype(F32)
    mid = r1.astype(BF16)
    lo = (r1 - mid.astype(F32)).astype(BF16)
    cs_all = _dot(tri, hi) + _dot(tri, mid) + _dot(tri, lo)
    ig_col, b_col, ig_row, b_row, qs, ks, vs, zos = [], [], [], [], [], [], [], []
    for bi in range(bb):
        gl = gls[bi]
        cs = cs_all[:, bi * 128:(bi + 1) * 128]
        gl_t = gl.T
        cs_t = cs.T
        for h in range(HEADS):
            hs = slice(h * HDIM, (h + 1) * HDIM)
            ig_col.append(gl[:, h:h + 1])
            b_col.append(cs[:, HEADS + h:HEADS + h + 1])
            ig_row.append(gl_t[h:h + 1, :])
            b_row.append(cs_t[HEADS + h:HEADS + h + 1, :])
            qs.append(qkv_ref[bi, :, hs])
            ks.append(qkv_ref[bi, :, MLSTM_WIDTH + h * HDIM:MLSTM_WIDTH + (h + 1) * HDIM])
            vs.append(qkv_ref[bi, :, 2 * MLSTM_WIDTH + h * HDIM:2 * MLSTM_WIDTH + (h + 1) * HDIM])
            zos.append(zo_ref[bi, :, hs])
    ig_col, b_col, ig_row, b_row = (jnp.stack(a) for a in (ig_col, b_col, ig_row, b_row))
    q, k, v, zo = (jnp.stack(a) for a in (qs, ks, vs, zos))
    gm = jnp.stack([gm_ref[:, h * HDIM:(h + 1) * HDIM] for h in range(HEADS)] * bb)
    c_old = c_ref[...].reshape(G, HDIM, HDIM)
    n_old = n_ref[...].reshape(G, 1, HDIM)
    m_prev = m_ref[...].reshape(G, 1, 1)

    dmat = jnp.where(causal[None], b_col - b_row + ig_row, -jnp.inf)
    inter = b_col + m_prev
    m_t = jnp.maximum(inter, jnp.max(dmat, axis=-1, keepdims=True))
    w_intra = jnp.exp(dmat - m_t)
    w_inter = jnp.exp(inter - m_t)
    s = _bdot(q, k, 2, 2) * w_intra
    num = _bdot(s.astype(BF16), v, 2, 1) + w_inter * _bdot(q, c_old.astype(BF16), 2, 2)
    qn = jnp.sum(q.astype(F32) * n_old, axis=-1, keepdims=True)
    den = jnp.sum(s, axis=-1, keepdims=True) + w_inter * qn
    hh = num / jnp.maximum(jnp.abs(den), jnp.exp(-m_t))
    hh = hh * lax.rsqrt(jnp.mean(hh * hh, axis=-1, keepdims=True) + RMS_EPS)
    hh = (hh * gm * _sigmoid(zo)).astype(h_ref.dtype)
    for bi in range(bb):
        for h in range(HEADS):
            h_ref[bi, :, h * HDIM:(h + 1) * HDIM] = hh[bi * HEADS + h]
    m_new = m_t[:, L - 1:L, :]
    b_last = b_col[:, L - 1:L, :]
    w_state = jnp.exp(b_last - b_col + ig_col - m_new)
    decay = jnp.exp(b_last + m_prev - m_new)
    vw = (v.astype(F32) * w_state).astype(BF16)
    c_new = decay * c_old + _bdot(vw, k, 1, 1)
    n_new = decay * n_old + jnp.sum(k.astype(F32) * w_state, axis=1, keepdims=True)
    c_ref[...] = c_new.reshape(c_ref.shape)
    n_ref[...] = n_new.reshape(n_ref.shape)
    m_ref[...] = m_new.reshape(m_ref.shape)


def _mlstm_chunks(qkv, zif, zo, bif, gm, l, bb):
    b, t, _ = qkv.shape
    blk = lambda w: pl.BlockSpec((bb, CHUNK, w), lambda i, j: (i, j, 0))
    state = lambda r, c: pl.BlockSpec((bb, HEADS, r, c), lambda i, j: (i, 0, 0, 0))
    return pl.pallas_call(
        functools.partial(_mlstm_chunk_body, bb=bb),
        grid=(b // bb, t // CHUNK),
        in_specs=[blk(3 * MLSTM_WIDTH), blk(128), blk(MLSTM_WIDTH), _pick((1, 128), l), _pick((1, MLSTM_WIDTH), l)],
        out_specs=[blk(MLSTM_WIDTH), state(HDIM, HDIM), state(1, HDIM), state(1, 1)],
        out_shape=[
            jax.ShapeDtypeStruct((b, t, MLSTM_WIDTH), BF16),
            jax.ShapeDtypeStruct((b, HEADS, HDIM, HDIM), F32),
            jax.ShapeDtypeStruct((b, HEADS, 1, HDIM), F32),
            jax.ShapeDtypeStruct((b, HEADS, 1, 1), F32),
        ],
        compiler_params=_params("arbitrary", "arbitrary"),
        name="mlstm_chunks",
    )(qkv, zif, zo, bif, gm)


def _mlstm_step_body(qkv_ref, zif_ref, zo_ref, bif_ref, gm_ref, c0_ref, n0_ref, m0_ref, h_ref, c_ref, n_ref, m_ref):
    g = zif_ref[...] + bif_ref[...]
    r = lax.broadcasted_iota(jnp.int32, (HDIM, HDIM), 0)
    c = lax.broadcasted_iota(jnp.int32, (HDIM, HDIM), 1)
    eye = (r == c)[None]
    for h in range(HEADS):
        hs = slice(h * HDIM, (h + 1) * HDIM)
        ig = g[:, :, h:h + 1]
        lf = _log_sigmoid(g[:, :, HEADS + h:HEADS + h + 1])
        m_prev = m0_ref[:, :, h:h + 1]
        q = qkv_ref[:, :, hs]
        k = qkv_ref[:, :, MLSTM_WIDTH + h * HDIM:MLSTM_WIDTH + (h + 1) * HDIM]
        v = qkv_ref[:, :, 2 * MLSTM_WIDTH + h * HDIM:2 * MLSTM_WIDTH + (h + 1) * HDIM]
        c_h = c0_ref[:, h]
        n_h = n0_ref[:, h:h + 1, :]
        inter = lf + m_prev
        m_t = jnp.maximum(inter, ig)
        w_intra = jnp.exp(ig - m_t)
        w_inter = jnp.exp(inter - m_t)
        s = jnp.sum(q * k, axis=-1, keepdims=True) * w_intra
        cq = jnp.sum(c_h * q, axis=-1, keepdims=True)
        v_col = jnp.sum(jnp.where(eye, v, 0.0), axis=-1, keepdims=True)
        num = s * v_col + w_inter * cq
        den = s + w_inter * jnp.sum(n_h * q, axis=-1, keepdims=True)
        hcol = num / jnp.maximum(jnp.abs(den), jnp.exp(-m_t))
        hcol = hcol * lax.rsqrt(jnp.mean(hcol * hcol, axis=1, keepdims=True) + RMS_EPS)
        hrow = jnp.sum(jnp.where(eye, hcol, 0.0), axis=1, keepdims=True)
        hrow = hrow * gm_ref[:, hs] * _sigmoid(zo_ref[:, :, hs])
        h_ref[:, :, hs] = hrow.astype(h_ref.dtype)
        w_state = jnp.exp(ig - m_t)
        decay = jnp.exp(lf + m_prev - m_t)
        c_ref[:, h] = decay * c_h + (w_state * v_col) * k
        n_ref[:, h:h + 1, :] = decay * n_h + w_state * k
        m_ref[:, :, h:h + 1] = m_t


def _layer_norm(v, g, b):
    vc = v - jnp.mean(v, axis=-1, keepdims=True)
    return vc * lax.rsqrt(jnp.mean(vc * vc, axis=-1, keepdims=True) + LN_EPS) * g + b


def _gmlp_step_body(ugv_ref, lng_ref, lnb_ref, w00_ref, b0_ref, o_ref, vn_ref):
    u = _gelu_tanh(ugv_ref[:, 0:GMLP_WIDTH])
    v = _gelu_tanh(ugv_ref[:, GMLP_WIDTH:])
    vn = _layer_norm(v, lng_ref[...], lnb_ref[...])
    vn_ref[...] = vn
    o_ref[...] = (u * (w00_ref[...] * vn + b0_ref[...])).astype(o_ref.dtype)


def _merge_body(x_ref, hp_ref, hm_ref, hg_ref, g_ref, wgt_ref, wbp_ref, wbm_ref, wbg_ref, wout_ref, o_ref):
    x = x_ref[...]
    xn = _rms(x, g_ref[0:1, :]).astype(BF16)
    y = None
    for j in range(D_MODEL // MERGE_CHUNK):
        cols = slice(j * MERGE_CHUNK, (j + 1) * MERGE_CHUNK)
        merged = None
        for i, (h_ref, w_ref) in enumerate(((hp_ref, wbp_ref), (hm_ref, wbm_ref), (hg_ref, wbg_ref))):
            gate_rows = slice(i * D_MODEL + j * MERGE_CHUNK, i * D_MODEL + (j + 1) * MERGE_CHUNK)
            gate = _sigmoid(_dot_nt(xn, wgt_ref[gate_rows, :]))
            term = gate * _dot(h_ref[...], w_ref[:, cols])
            merged = term if merged is None else merged + term
        part = _dot(merged.astype(BF16), wout_ref[cols, :])
        y = part if y is None else y + part
    o_ref[...] = x + _rms(y, g_ref[1:2, :])


def _merge_q_body(x_ref, hp_ref, hm_ref, hg_ref, g_ref, wgt_ref, wbp_ref, wbm_ref, wbg_ref, wout_ref,
                  ga_ref, wq_ref, o_ref, q_ref):
    _merge_body(x_ref, hp_ref, hm_ref, hg_ref, g_ref, wgt_ref, wbp_ref, wbm_ref, wbg_ref, wout_ref, o_ref)
    xn = _rms(o_ref[...], ga_ref[0:1, :]).astype(BF16)
    q_ref[...] = _dot(xn, wq_ref[...])


def _merge_weight_specs(l):
    return [_pick((2, D_MODEL), l, 1), _pick((3 * D_MODEL, D_MODEL), l), _pick((POOL_WIDTH, D_MODEL), l),
            _pick((MLSTM_WIDTH, D_MODEL), l), _pick((GMLP_WIDTH, D_MODEL), l), _pick((D_MODEL, D_MODEL), l)]


def _merge(x, hp, hm, hg, norm_g, p, l, tm):
    n = x.shape[0]
    row = lambda w: pl.BlockSpec((tm, w), lambda i: (i, 0))
    return pl.pallas_call(
        _merge_body, grid=(n // tm,),
        in_specs=[row(D_MODEL), row(POOL_WIDTH), row(MLSTM_WIDTH), row(GMLP_WIDTH)] + _merge_weight_specs(l),
        out_specs=row(D_MODEL), out_shape=jax.ShapeDtypeStruct((n, D_MODEL), F32),
        compiler_params=_params("arbitrary"), name="merge",
    )(x, hp, hm, hg, norm_g, p["wgt"], p["wbp"], p["wbm"], p["wbg"], p["wout"])


def _sample_mix_body(x_ref, hm_ref, zp_ref, buf_ref, wbd_ref, sp_ref, ugv_ref, lng_ref, lnb_ref, w00_ref, b0_ref,
                     g_ref, wgt_ref, wbp_ref, wbm_ref, wbg_ref, wout_ref, ga_ref, wq_ref,
                     o_ref, q_ref, nbuf_ref, vn_ref, hp_ref, hg_ref):
    _pool_step_body(zp_ref, buf_ref, wbd_ref, sp_ref, hp_ref, nbuf_ref)
    _gmlp_step_body(ugv_ref, lng_ref, lnb_ref, w00_ref, b0_ref, hg_ref, vn_ref)
    _merge_q_body(x_ref, hp_ref, hm_ref, hg_ref, g_ref, wgt_ref, wbp_ref, wbm_ref, wbg_ref, wout_ref,
                  ga_ref, wq_ref, o_ref, q_ref)


def _sample_mix_call(x, hm, zp, ugv, pool_t, norm_g, p, l):
    b = x.shape[0]
    full = lambda *shape: pl.BlockSpec(shape, lambda i: (0,) * len(shape))
    vec = _pick((1, GMLP_WIDTH), l)
    return pl.pallas_call(
        _sample_mix_body,
        grid=(1,),
        in_specs=[full(b, D_MODEL), full(b, MLSTM_WIDTH), full(b, POOL_WIDTH), _pick((POOL_BUF, b, POOL_WIDTH), l),
                  _pick((POOL_WIDTH, POOL_WIDTH), l), _pick((1, POOL_WIDTH), l), full(b, 2 * GMLP_WIDTH),
                  vec, vec, vec, vec] + _merge_weight_specs(l) + [_pick((2, D_MODEL), l, 2),
                                                                   _pick((D_MODEL, D_MODEL), l)],
        out_specs=[full(b, D_MODEL), full(b, D_MODEL), full(POOL_BUF, b, POOL_WIDTH), full(b, GMLP_WIDTH)],
        out_shape=[jax.ShapeDtypeStruct((b, D_MODEL), F32), jax.ShapeDtypeStruct((b, D_MODEL), F32),
                   jax.ShapeDtypeStruct((POOL_BUF, b, POOL_WIDTH), F32), jax.ShapeDtypeStruct((b, GMLP_WIDTH), F32)],
        scratch_shapes=[pltpu.VMEM((b, POOL_WIDTH), BF16), pltpu.VMEM((b, GMLP_WIDTH), BF16)],
        compiler_params=_params("arbitrary"),
        name="sample_mix",
    )(x, hm, zp, pool_t, p["wbd"], p["sp"], ugv, p["lng"], p["lnb"], p["w00"], p["b0"],
      norm_g, p["wgt"], p["wbp"], p["wbm"], p["wbg"], p["wout"], norm_g, p["wq"])


def _memkv_body(mem_ref, g_ref, wk_ref, wv_ref, k_ref, v_ref, kb_ref, vb_ref):
    mn = _rms(mem_ref[...], g_ref[...]).astype(BF16)
    k = _dot(mn, wk_ref[...])
    v = _dot(mn, wv_ref[...])
    kb_ref[...] = k.astype(BF16)
    vb_ref[...] = v.astype(BF16)
    for h in range(MEM_HEADS):
        for half in range(2):
            cols = slice(h * MEM_HDIM + half * 128, h * MEM_HDIM + (half + 1) * 128)
            k_ref[:, half * MEM_HEADS + h, :] = k[:, cols]
            v_ref[:, half * MEM_HEADS + h, :] = v[:, cols]


def _memkv(mem, g, wk, wv, tm):
    n = mem.shape[0]
    depth = wk.shape[0]
    per_layer = lambda *tail: pl.BlockSpec((None,) + tail, lambda l, j: (l,) + (0,) * len(tail))
    rows = lambda *tail: pl.BlockSpec((None, tm) + tail, lambda l, j: (l, j) + (0,) * len(tail))
    return pl.pallas_call(
        _memkv_body,
        grid=(depth, n // tm),
        in_specs=[pl.BlockSpec((tm, D_MODEL), lambda l, j: (j, 0)), per_layer(1, D_MODEL),
                  per_layer(D_MODEL, D_MODEL), per_layer(D_MODEL, D_MODEL)],
        out_specs=[rows(2 * MEM_HEADS, 128), rows(2 * MEM_HEADS, 128), rows(D_MODEL), rows(D_MODEL)],
        out_shape=[jax.ShapeDtypeStruct((depth, n, 2 * MEM_HEADS, 128), F32),
                   jax.ShapeDtypeStruct((depth, n, 2 * MEM_HEADS, 128), F32),
                   jax.ShapeDtypeStruct((depth, n, D_MODEL), BF16), jax.ShapeDtypeStruct((depth, n, D_MODEL), BF16)],
        compiler_params=_params("arbitrary", "arbitrary"),
        name="memkv",
    )(mem, g, wk, wv)


def _attn_prompt_body(x_ref, k_ref, v_ref, g_ref, wq_ref, wo_ref, o_ref):
    x = x_ref[0]
    xn = _rms(x, g_ref[0:1, :]).astype(BF16)
    q = _dot(xn, wq_ref[...]).astype(BF16)
    outs = []
    for h in range(MEM_HEADS):
        hs = slice(h * MEM_HDIM, (h + 1) * MEM_HDIM)
        s = _dot_nt(q[:, hs], k_ref[:, hs]) * (MEM_HDIM ** -0.5)
        e = jnp.exp(s - jnp.max(s, axis=-1, keepdims=True))
        p = e / jnp.sum(e, axis=-1, keepdims=True)
        outs.append(_dot(p.astype(BF16), v_ref[:, hs]).astype(BF16))
    o = jnp.concatenate(outs, axis=-1)
    y = _dot(o, wo_ref[...])
    o_ref[0] = x + _rms(y, g_ref[1:2, :])


def _attn_prompt(x, kb, vb, norm_g, wq, wo, l, tq):
    b, t, _ = x.shape
    xs = pl.BlockSpec((1, tq, D_MODEL), lambda i, j: (i, j, 0))
    ms = pl.BlockSpec((None, MEM_TOKENS, D_MODEL), lambda i, j: (l, i, 0))
    return pl.pallas_call(
        _attn_prompt_body,
        grid=(b, t // tq),
        in_specs=[xs, ms, ms, _pick((2, D_MODEL), l, 2), _pick((D_MODEL, D_MODEL), l), _pick((D_MODEL, D_MODEL), l)],
        out_specs=xs,
        out_shape=jax.ShapeDtypeStruct((b, t, D_MODEL), F32),
        compiler_params=_params("arbitrary", "arbitrary"),
        name="attn_prompt",
    )(x, kb, vb, norm_g, wq, wo)


def _attn_step_body(q_ref, k_ref, v_ref, o_ref):
    q = q_ref[...] * (MEM_HDIM ** -0.5)
    part = jnp.sum(k_ref[...] * q, axis=-1, keepdims=True)
    s = part + pltpu.roll(part, MEM_HEADS, 2)
    e = jnp.exp(s - jnp.max(s, axis=1, keepdims=True))
    o_ref[...] = jnp.sum(e * v_ref[...], axis=1, keepdims=True) / jnp.sum(e, axis=1, keepdims=True)


def _ffn_attn_body(x_ref, g_ref, wi_ref, wo_ref, q_ref, k_ref, v_ref, o_ref, a_ref, acc_ref):
    _attn_step_body(q_ref, k_ref, v_ref, a_ref)
    o_ref[...] = _swiglu_half_step(x_ref[...], g_ref, wi_ref, wo_ref, acc_ref)


def _ffn_with_attn(x, norm_g, w_in, w_out, l, sub, half, q, k_all, v_all):
    n = x.shape[0]
    b = q.shape[0]
    steps = 32
    tm, bb = n // steps, b // steps
    qs = pl.BlockSpec((bb, 1, 2 * MEM_HEADS, 128), lambda i: (i, 0, 0, 0))
    ms = pl.BlockSpec((None, bb, MEM_TOKENS, 2 * MEM_HEADS, 128), lambda i: (l, i, 0, 0, 0))
    return pl.pallas_call(
        _ffn_attn_body,
        grid=(steps,),
        in_specs=[pl.BlockSpec((tm, D_MODEL), lambda i: (i, 0)), _pick((2, D_MODEL), l, sub),
                  _pick((D_MODEL, 2 * D_FF), l, half), _pick((D_FF, D_MODEL), l, half), qs, ms, ms],
        out_specs=[pl.BlockSpec((tm, D_MODEL), lambda i: (i, 0)), qs],
        out_shape=[jax.ShapeDtypeStruct((n, D_MODEL), F32),
                   jax.ShapeDtypeStruct((b, 1, 2 * MEM_HEADS, 128), F32)],
        scratch_shapes=[pltpu.VMEM((tm, D_MODEL), F32)],
        compiler_params=_params("arbitrary"),
        name="ffn_attn",
    )(x, norm_g, w_in, w_out, q, k_all, v_all)


def _halves_major(a):
    lead = a.shape[:-2]
    return jnp.swapaxes(a.reshape(*lead, MEM_HEADS, 2, 128), -3, -2).reshape(*lead, 2 * MEM_HEADS, 128)


def _heads_major(a):
    lead = a.shape[:-2]
    return jnp.swapaxes(a.reshape(*lead, 2, MEM_HEADS, 128), -3, -2).reshape(*lead, D_MODEL)


def _prep_weights(w_ff_in, w_ff_out, w_in, b_igate, b_fgate, w_pool, s_pool, g_mlstm, gmlp_ln_g, gmlp_ln_b,
                  w_s, b_s, w_br_pool, w_br_mlstm, w_br_gmlp, w_out, g_mem, w_mq, w_mk, w_mv, w_mo):
    p = {}
    p["ffn_in"] = w_ff_in.astype(BF16)
    p["ffn_out"] = w_ff_out.astype(BF16)
    wt = jnp.swapaxes(w_in, 1, 2)
    w_if = jnp.pad(wt[:, OFF_IF:OFF_U], ((0, 0), (0, 128 - 2 * HEADS), (0, 0)))
    p["w1t"] = jnp.concatenate([wt[:, :OFF_IF], wt[:, OFF_U:OFF_GATE], w_if], axis=1).astype(BF16)
    p["wgt"] = wt[:, OFF_GATE:].astype(BF16)
    bif = jnp.concatenate([b_igate, b_fgate], axis=1)
    p["bif"] = jnp.pad(bif, ((0, 0), (0, 128 - 2 * HEADS))).reshape(DEPTH, 1, 128)
    same_group = jnp.eye(len(POOL_WINDOWS), dtype=bool)[None, :, None, :, None]
    wbd = jnp.where(same_group, w_pool[:, :, :, None, :], 0.0)
    p["wbd"] = wbd.reshape(DEPTH, POOL_WIDTH, POOL_WIDTH).astype(BF16)
    p["sp"] = s_pool.reshape(DEPTH, 1, POOL_WIDTH)
    p["gm"] = g_mlstm.reshape(DEPTH, 1, MLSTM_WIDTH)
    p["lng"] = gmlp_ln_g.reshape(DEPTH, 1, GMLP_WIDTH)
    p["lnb"] = gmlp_ln_b.reshape(DEPTH, 1, GMLP_WIDTH)
    p["ws"] = w_s.astype(BF16)
    p["bst"] = jnp.swapaxes(b_s, 1, 2)
    p["w00"] = jnp.repeat(w_s[:, :, 0, 0], GMLP_GDIM, axis=1).reshape(DEPTH, 1, GMLP_WIDTH)
    p["b0"] = jnp.repeat(b_s[:, :, 0], GMLP_GDIM, axis=1).reshape(DEPTH, 1, GMLP_WIDTH)
    p["wbp"] = w_br_pool.astype(BF16)
    p["wbm"] = w_br_mlstm.astype(BF16)
    p["wbg"] = w_br_gmlp.astype(BF16)
    p["wout"] = w_out.astype(BF16)
    p["gmem"] = g_mem.reshape(DEPTH, 1, D_MODEL)
    p["wq"] = w_mq.astype(BF16)
    p["wk"] = w_mk.astype(BF16)
    p["wv"] = w_mv.astype(BF16)
    p["wo"] = w_mo.astype(BF16)
    return p


def _ffn_mlstm_body(x_ref, g_ref, wi_ref, wo_ref, *rest, first):
    outs = rest[8:] if first else rest[9:]
    o_ref, h_ref, c_ref, n_ref, m_ref, acc_ref = outs
    if first:
        c_ref[1:] = jnp.zeros((c_ref.shape[0] - 1,) + c_ref.shape[1:], F32)
        c_ref = c_ref.at[0]
    _mlstm_step_body(*rest[0:8], h_ref, c_ref, n_ref, m_ref)
    o_ref[...] = _swiglu_half_step(x_ref[...], g_ref, wi_ref, wo_ref, acc_ref)


def _ffn_with_mlstm(x, norm_g, w_in, w_out, l, sub, half, qkv, zif, zo, bif, gm, c_all, n_all, m_all, c_prev):
    n = x.shape[0]
    b = qkv.shape[0]
    depth = c_all.shape[0]
    first = c_prev is None
    steps = 32
    tm, bb = n // steps, b // steps
    blk = lambda w: pl.BlockSpec((bb, 1, w), lambda i: (i, 0, 0))
    st_c = pl.BlockSpec((None, bb, HEADS, HDIM, HDIM), lambda i: (l, i, 0, 0, 0))
    st_n = pl.BlockSpec((None, bb, HEADS, HDIM), lambda i: (l, i, 0, 0))
    st_m = pl.BlockSpec((None, bb, 1, HEADS), lambda i: (l, i, 0, 0))
    row = pl.BlockSpec((tm, D_MODEL), lambda i: (i, 0))
    in_specs = [row, _pick((2, D_MODEL), l, sub), _pick((D_MODEL, 2 * D_FF), l, half), _pick((D_FF, D_MODEL), l, half),
                blk(3 * MLSTM_WIDTH), blk(128), blk(MLSTM_WIDTH), _pick((1, 128), l), _pick((1, MLSTM_WIDTH), l),
                st_c, st_n, st_m]
    args = [x, norm_g, w_in, w_out, qkv, zif, zo, bif, gm, c_all, n_all, m_all]
    if first:
        assert l == 0
        out_c = pl.BlockSpec((depth, bb, HEADS, HDIM, HDIM), lambda i: (0, i, 0, 0, 0))
        aliases = {}
    else:
        in_specs.append(pl.BlockSpec(memory_space=pl.ANY))
        args.append(c_prev)
        out_c = st_c
        aliases = {12: 2}
    return pl.pallas_call(
        functools.partial(_ffn_mlstm_body, first=first),
        grid=(steps,),
        in_specs=in_specs,
        out_specs=[row, blk(MLSTM_WIDTH), out_c, pl.BlockSpec((bb, HEADS, HDIM), lambda i: (i, 0, 0)),
                   pl.BlockSpec((bb, 1, HEADS), lambda i: (i, 0, 0))],
        out_shape=[
            jax.ShapeDtypeStruct((n, D_MODEL), F32),
            jax.ShapeDtypeStruct((b, 1, MLSTM_WIDTH), BF16),
            jax.ShapeDtypeStruct(c_all.shape, F32),
            jax.ShapeDtypeStruct((b, HEADS, HDIM), F32),
            jax.ShapeDtypeStruct((b, 1, HEADS), F32),
        ],
        input_output_aliases=aliases,
        scratch_shapes=[pltpu.VMEM((tm, D_MODEL), F32)],
        compiler_params=_params("arbitrary"),
        name="ffn_mlstm",
    )(*args)


def _prompt_layer(x, kb, vb, norm_g, p, l, batch, seq):
    tm = 1024
    qkv, zo, zif, hp, hg, tail = _inproj_mix(x, norm_g, p, l, tm, seq)
    b3 = lambda a: a.reshape(batch, seq, a.shape[-1])
    hm, c_new, n_new, m_new = _mlstm_chunks(b3(qkv), b3(zif), b3(zo), p["bif"], p["gm"], l, 8)
    x = _merge(x, hp, hm.reshape(batch * seq, MLSTM_WIDTH), hg, norm_g, p, l, tm)
    x = _attn_prompt(b3(x), kb, vb, norm_g, p["wq"], p["wo"], l, 1024).reshape(batch * seq, D_MODEL)
    return x, tail[:, 16 - POOL_BUF:, :], c_new, n_new.reshape(batch, HEADS, HDIM), m_new.reshape(batch, HEADS)


def _sample_mix(x, zp, ugv, hm, norm_g, p, l, pool_t):
    batch = x.shape[0]
    x, q, new_buf, vn = _sample_mix_call(x, hm.reshape(batch, MLSTM_WIDTH), zp, ugv, pool_t, norm_g, p, l)
    q8 = _halves_major(q.reshape(batch, 1, MEM_HEADS, MEM_HDIM))
    return x, q8, new_buf, vn.reshape(batch, 1, GMLP_WIDTH)


def kernel(x_prompt, x_sample, state_pool, state_mlstm_C, state_mlstm_n, state_mlstm_m, cache_mem_k, cache_mem_v, mem_prompt, norm_g, w_ff_in, w_ff_out, w_in, b_igate, b_fgate, w_pool, s_pool, g_mlstm, gmlp_ln_g, gmlp_ln_b, w_s, b_s, w_br_pool, w_br_mlstm, w_br_gmlp, w_out, g_mem, w_mq, w_mk, w_mv, w_mo):
    batch, seq, _ = x_prompt.shape
    dec_batch = x_sample.shape[0]
    assert x_sample.shape[1] == 1 and seq % CHUNK == 0 and PAST_LEN % CHUNK == 0 and PAST_LEN >= max(POOL_WINDOWS)
    p = _prep_weights(w_ff_in, w_ff_out, w_in, b_igate, b_fgate, w_pool, s_pool, g_mlstm, gmlp_ln_g, gmlp_ln_b,
                      w_s, b_s, w_br_pool, w_br_mlstm, w_br_gmlp, w_out, g_mem, w_mq, w_mk, w_mv, w_mo)
    xp = x_prompt.reshape(batch * seq, D_MODEL)
    xs = x_sample.reshape(dec_batch, D_MODEL)
    k32, v32, kb, vb = _memkv(mem_prompt.reshape(batch * MEM_TOKENS, D_MODEL), p["gmem"], p["wk"], p["wv"], 512)
    mk = _heads_major(k32).reshape(DEPTH, batch, MEM_TOKENS, MEM_HEADS, MEM_HDIM)
    mv = _heads_major(v32).reshape(DEPTH, batch, MEM_TOKENS, MEM_HEADS, MEM_HDIM)
    pool_t = jnp.swapaxes(state_pool, 1, 2)
    m_all = state_mlstm_m.reshape(DEPTH, dec_batch, 1, HEADS)
    k_dense = _halves_major(cache_mem_k)
    v_dense = _halves_major(cache_mem_v)
    outs = [[] for _ in range(8)]
    c_samp = None
    for l in range(DEPTH):
        xs, zp, qkv, zo, ugv, zif = _ffn_inproj(xs, norm_g, p["ffn_in"], p["ffn_out"], p["w1t"], l)
        r3 = lambda a: a.reshape(dec_batch, 1, a.shape[-1])
        xp, hm, c_samp, ns, ms = _ffn_with_mlstm(xp, norm_g, p["ffn_in"], p["ffn_out"], l, 0, 0, r3(qkv), r3(zif),
                                                 r3(zo), p["bif"], p["gm"], state_mlstm_C, state_mlstm_n, m_all,
                                                 c_samp)
        ms = ms.reshape(dec_batch, HEADS)
        xp, pb, c, n, m = _prompt_layer(xp, kb, vb, norm_g, p, l, batch, seq)
        xs, q8, pbs, vr = _sample_mix(xs, zp, ugv, hm, norm_g, p, l, pool_t)
        xp, a8 = _ffn_with_attn(xp, norm_g, p["ffn_in"], p["ffn_out"], l, 3, 1, q8, k_dense, v_dense)
        xs = _oproj_ffn(xs, _heads_major(a8.reshape(dec_batch, 2 * MEM_HEADS, 128)), norm_g, p["wo"],
                        p["ffn_in"], p["ffn_out"], l)
        for lst, val in zip(outs, (pb, pbs, c, n, m, ns, ms, vr)):
            lst.append(val)
    pb, pbs, c, n, m, ns, ms, vr = [jnp.stack(o) for o in outs]
    return (xp.reshape(batch, seq, D_MODEL), xs.reshape(dec_batch, 1, D_MODEL),
            pb, jnp.swapaxes(pbs, 1, 2), c, n, m, c_samp, ns, ms, mk, mv, vr)
```

```python
import functools

import jax
import jax.numpy as jnp
from jax import lax
from jax.experimental import pallas as pl
from jax.experimental.pallas import tpu as pltpu

F32 = jnp.float32
BF16 = jnp.bfloat16

D_MODEL = 1024
DEPTH = 4
PAST_LEN = 16384
POOL_WINDOWS = (2, 4, 8, 16)
POOL_WIDTH = 256
POOL_GDIM = 64
POOL_BUF = 15
HEADS = 4
MLSTM_WIDTH = 512
HDIM = 128
CHUNK = 128
GMLP_WIDTH = 256
GMLP_GROUPS = 4
GMLP_GDIM = 64
MEM_TOKENS = 256
MEM_HEADS = 4
MEM_HDIM = 256
D_FF = 2816
FF_CHUNK = 256
MERGE_CHUNK = 256
N_FF_CHUNKS = D_FF // FF_CHUNK
RMS_EPS = 1e-6
LN_EPS = 1e-5
OFF_Q = POOL_WIDTH
OFF_IF = OFF_Q + 4 * MLSTM_WIDTH
OFF_U = OFF_IF + 2 * HEADS
OFF_GATE = OFF_U + 2 * GMLP_WIDTH
W1_QKV = POOL_WIDTH
W1_O = W1_QKV + 3 * MLSTM_WIDTH
W1_UGV = W1_O + MLSTM_WIDTH
W1_IF = W1_UGV + 2 * GMLP_WIDTH
W1_ROWS = W1_IF + 128

VMEM_LIMIT_BYTES = 56 * 1024 * 1024


def _params(*sem):
    return pltpu.CompilerParams(dimension_semantics=sem, vmem_limit_bytes=VMEM_LIMIT_BYTES)


def _pick(tail, *idx):
    nt = len(tail)
    return pl.BlockSpec((None,) * len(idx) + tuple(tail), lambda *_: tuple(idx) + (0,) * nt,
                        pipeline_mode=pl.Buffered(1))


def _rms(x, g):
    return x * lax.rsqrt(jnp.mean(x * x, axis=-1, keepdims=True) + RMS_EPS) * g


def _dot(a, b):
    return jnp.dot(a, b, preferred_element_type=F32)


def _dot_nt(a, b):
    return lax.dot_general(a, b, (((1,), (1,)), ((), ())), preferred_element_type=F32)


def _sigmoid(x):
    return 1.0 / (1.0 + jnp.exp(-x))


def _gelu_tanh(x):
    return 0.5 * x * (1.0 + jnp.tanh(0.7978845608028654 * (x + 0.044715 * (x * x * x))))


def _log_sigmoid(x):
    return -(jnp.maximum(-x, 0.0) + jnp.log(1.0 + jnp.exp(-jnp.abs(x))))


def _swiglu_half_step(x, g_ref, wi_ref, wo_ref, acc_ref):
    xn = _rms(x, g_ref[0:1, :]).astype(BF16)
    for c in range(N_FF_CHUNKS):
        cols = slice(c * FF_CHUNK, (c + 1) * FF_CHUNK)
        gate = _dot(xn, wi_ref[:, cols])
        up = _dot(xn, wi_ref[:, D_FF + c * FF_CHUNK:D_FF + (c + 1) * FF_CHUNK])
        act = (gate * _sigmoid(gate) * up).astype(BF16)
        part = _dot(act, wo_ref[cols, :])
        if c == 0:
            acc_ref[...] = part
        else:
            acc_ref[...] += part
    return x + 0.5 * _rms(acc_ref[...], g_ref[1:2, :])


def _oproj_ffn_body(x_ref, a_ref, ga_ref, wmo_ref, g_ref, wi_ref, wo_ref, o_ref, acc_ref):
    a = jnp.concatenate([a_ref[:, 0, half * MEM_HEADS + h, :] for h in range(MEM_HEADS) for half in range(2)], axis=-1)
    y = _dot(a.astype(BF16), wmo_ref[...])
    x = x_ref[...] + _rms(y, ga_ref[1:2, :])
    o_ref[...] = _swiglu_half_step(x, g_ref, wi_ref, wo_ref, acc_ref)


def _oproj_ffn(x, a, norm_g, wmo, w_in, w_out, l):
    n = x.shape[0]
    full = pl.BlockSpec((n, D_MODEL), lambda i: (0, 0))
    a_spec = pl.BlockSpec((n, 1, 2 * MEM_HEADS, 128), lambda i: (0, 0, 0, 0))
    return pl.pallas_call(
        _oproj_ffn_body,
        grid=(1,),
        in_specs=[full, a_spec, _pick((2, D_MODEL), l, 2), _pick((D_MODEL, D_MODEL), l),
                  _pick((2, D_MODEL), l, 3), _pick((D_MODEL, 2 * D_FF), l, 1), _pick((D_FF, D_MODEL), l, 1)],
        out_specs=full,
        out_shape=jax.ShapeDtypeStruct((n, D_MODEL), F32),
        scratch_shapes=[pltpu.VMEM((n, D_MODEL), F32)],
        compiler_params=_params("arbitrary"),
        name="oproj_ffn",
    )(x, a, norm_g, wmo, norm_g, w_in, w_out)


def _inproj_body(x_ref, g_ref, w_ref, zp_ref, qkv_ref, zo_ref, ugv_ref, zif_ref):
    xn = _rms(x_ref[...], g_ref[0:1, :]).astype(BF16)
    z = _dot_nt(xn, w_ref[...])
    zp_ref[...] = z[:, :W1_QKV]
    q = z[:, W1_QKV:W1_QKV + MLSTM_WIDTH]
    k = z[:, W1_QKV + MLSTM_WIDTH:W1_QKV + 2 * MLSTM_WIDTH] * (HDIM ** -0.5)
    v = z[:, W1_QKV + 2 * MLSTM_WIDTH:W1_O]
    qkv_ref[:, 0, 0:MLSTM_WIDTH] = q
    qkv_ref[:, 0, MLSTM_WIDTH:2 * MLSTM_WIDTH] = k
    qkv_ref[:, 0, 2 * MLSTM_WIDTH:] = v
    zo_ref[:, 0, :] = z[:, W1_O:W1_UGV]
    ugv_ref[...] = z[:, W1_UGV:W1_IF]
    zif_ref[:, 0, :] = z[:, W1_IF:]


def _ffn_inproj_body(x_ref, g_ref, wi_ref, wo_ref, g1_ref, w1_ref,
                     o_ref, zp_ref, qkv_ref, zo_ref, ugv_ref, zif_ref, acc_ref):
    o_ref[...] = _swiglu_half_step(x_ref[...], g_ref, wi_ref, wo_ref, acc_ref)
    _inproj_body(o_ref, g1_ref, w1_ref, zp_ref, qkv_ref, zo_ref, ugv_ref, zif_ref)


def _ffn_inproj(x, norm_g, w_in, w_out, w1t, l):
    n = x.shape[0]
    full = lambda *shape: pl.BlockSpec(shape, lambda i: (0,) * len(shape))
    shapes = ((n, D_MODEL), (n, POOL_WIDTH), (n, 1, 3 * MLSTM_WIDTH), (n, 1, MLSTM_WIDTH), (n, 2 * GMLP_WIDTH),
              (n, 1, 128))
    return pl.pallas_call(
        _ffn_inproj_body,
        grid=(1,),
        in_specs=[full(n, D_MODEL), _pick((2, D_MODEL), l, 0), _pick((D_MODEL, 2 * D_FF), l, 0),
                  _pick((D_FF, D_MODEL), l, 0), _pick((2, D_MODEL), l, 1), _pick((W1_ROWS, D_MODEL), l)],
        out_specs=[full(*s) for s in shapes],
        out_shape=[jax.ShapeDtypeStruct(s, F32) for s in shapes],
        scratch_shapes=[pltpu.VMEM((n, D_MODEL), F32)],
        compiler_params=_params("arbitrary"),
        name="ffn_inproj",
    )(x, norm_g, w_in, w_out, norm_g, w1t)


def _pool_select(sums, u, count_of):
    lane = lax.broadcasted_iota(jnp.int32, u.shape, u.ndim - 1)
    mean = None
    for gi, w in enumerate(POOL_WINDOWS):
        m = sums[w] * (1.0 / count_of(w))
        mean = m if mean is None else jnp.where(lane >= gi * POOL_GDIM, m, mean)
    return mean - u


POOL_HALO = 2 * max(POOL_WINDOWS)


def _inproj_mix_body(x_ref, g_ref, w_ref, wbd_ref, sp_ref, lng_ref, lnb_ref, ws_ref, bst_ref,
                     qkv_ref, zo_ref, zif_ref, hp_ref, hg_ref, tail_ref,
                     full_ref, p2_ref, p4_ref, p8_ref, *, tm, tiles_per_seq):
    xn = _rms(x_ref[...], g_ref[0:1, :]).astype(BF16)
    proj = lambda lo, width: _dot_nt(xn, w_ref[lo:lo + width, :])
    u = proj(0, POOL_WIDTH)
    z_u = proj(W1_UGV, GMLP_WIDTH)
    z_gv = proj(W1_UGV + GMLP_WIDTH, GMLP_WIDTH)
    for c in range(3 * MLSTM_WIDTH // 256):
        part = proj(W1_QKV + c * 256, 256)
        if MLSTM_WIDTH <= c * 256 < 2 * MLSTM_WIDTH:
            part = part * (HDIM ** -0.5)
        qkv_ref[:, c * 256:(c + 1) * 256] = part.astype(qkv_ref.dtype)
    for c in range(MLSTM_WIDTH // 256):
        zo_ref[:, c * 256:(c + 1) * 256] = proj(W1_O + c * 256, 256)
    zif_ref[...] = proj(W1_IF, 128)

    t = lax.rem(pl.program_id(0), tiles_per_seq)
    hl = POOL_HALO
    n = hl + tm

    @pl.when(t == 0)
    def _():
        full_ref[0:hl, :] = jnp.zeros((hl, POOL_WIDTH), F32)

    @pl.when(t > 0)
    def _():
        full_ref[0:hl, :] = full_ref[tm:n, :]

    full_ref[hl:n, :] = u
    tail_ref[0] = u[tm - 16:, :]
    p2_ref[8:n, :] = full_ref[8:n, :] + full_ref[7:n - 1, :]
    p4_ref[16:n, :] = p2_ref[16:n, :] + p2_ref[14:n - 2, :]
    p8_ref[24:n, :] = p4_ref[24:n, :] + p4_ref[20:n - 4, :]
    sums = {2: p2_ref[hl:n, :], 4: p4_ref[hl:n, :], 8: p8_ref[hl:n, :],
            16: p8_ref[hl:n, :] + p8_ref[hl - 8:n - 8, :]}
    pos = t * tm + lax.broadcasted_iota(jnp.int32, (tm, 1), 0)
    diff = _pool_select(sums, u, lambda w: jnp.minimum(pos + 1, w).astype(F32))
    hp_ref[...] = (_dot(diff.astype(BF16), wbd_ref[...]) * sp_ref[...]).astype(hp_ref.dtype)

    row = lax.broadcasted_iota(jnp.int32, (CHUNK, CHUNK), 0)
    col = lax.broadcasted_iota(jnp.int32, (CHUNK, CHUNK), 1)
    causal = col <= row
    lane = lax.broadcasted_iota(jnp.int32, (CHUNK, GMLP_WIDTH), 1)
    ws = [jnp.where(causal, ws_ref[g], jnp.zeros((), ws_ref.dtype)) for g in range(GMLP_GROUPS)]
    for c in range(tm // CHUNK):
        rows = slice(c * CHUNK, (c + 1) * CHUNK)
        gu = _gelu_tanh(z_u[rows, :])
        gv = _gelu_tanh(z_gv[rows, :])
        vn = _layer_norm(gv, lng_ref[...], lnb_ref[...]).astype(BF16)
        mixed = None
        for g in range(GMLP_GROUPS):
            mg = _dot(ws[g], vn) + bst_ref[:, g:g + 1]
            mixed = mg if mixed is None else jnp.where(lane >= g * GMLP_GDIM, mg, mixed)
        hg_ref[rows, :] = (gu * mixed).astype(hg_ref.dtype)


def _inproj_mix(x, norm_g, p, l, tm, seq):
    n = x.shape[0]
    tiles_per_seq = seq // tm
    row = lambda w: pl.BlockSpec((tm, w), lambda i: (i, 0))
    return pl.pallas_call(
        functools.partial(_inproj_mix_body, tm=tm, tiles_per_seq=tiles_per_seq),
        grid=(n // tm,),
        in_specs=[row(D_MODEL), _pick((2, D_MODEL), l, 1), _pick((W1_ROWS, D_MODEL), l),
                  _pick((POOL_WIDTH, POOL_WIDTH), l), _pick((1, POOL_WIDTH), l),
                  _pick((1, GMLP_WIDTH), l), _pick((1, GMLP_WIDTH), l),
                  _pick((GMLP_GROUPS, CHUNK, CHUNK), l), _pick((CHUNK, GMLP_GROUPS), l)],
        out_specs=[row(3 * MLSTM_WIDTH), row(MLSTM_WIDTH), row(128), row(POOL_WIDTH), row(GMLP_WIDTH),
                   pl.BlockSpec((1, 16, POOL_WIDTH), lambda i: (i // tiles_per_seq, 0, 0))],
        out_shape=[
            jax.ShapeDtypeStruct((n, 3 * MLSTM_WIDTH), BF16),
            jax.ShapeDtypeStruct((n, MLSTM_WIDTH), F32),
            jax.ShapeDtypeStruct((n, 128), F32),
            jax.ShapeDtypeStruct((n, POOL_WIDTH), BF16),
            jax.ShapeDtypeStruct((n, GMLP_WIDTH), BF16),
            jax.ShapeDtypeStruct((n // seq, 16, POOL_WIDTH), F32),
        ],
        scratch_shapes=[pltpu.VMEM((tm + POOL_HALO, POOL_WIDTH), F32)] * 4,
        compiler_params=_params("arbitrary"),
        name="inproj_mix",
    )(x, norm_g, p["w1t"], p["wbd"], p["sp"], p["lng"], p["lnb"], p["ws"], p["bst"])


def _pool_step_body(zp_ref, buf_ref, wbd_ref, sp_ref, o_ref, nbuf_ref):
    u = zp_ref[...]
    acc = u
    sums = {}
    for j in range(1, max(POOL_WINDOWS)):
        acc = acc + buf_ref[POOL_BUF - j]
        if j + 1 in POOL_WINDOWS:
            sums[j + 1] = acc
    diff = _pool_select(sums, u, lambda w: float(w))
    y = _dot(diff.astype(BF16), wbd_ref[...]) * sp_ref[...]
    o_ref[...] = y.astype(o_ref.dtype)
    nbuf_ref[0:POOL_BUF - 1] = buf_ref[1:POOL_BUF]
    nbuf_ref[POOL_BUF - 1] = u


def _bdot(a, b, ca, cb):
    return lax.dot_general(a, b, (((ca,), (cb,)), ((0,), (0,))), preferred_element_type=F32)


def _mlstm_chunk_body(qkv_ref, zif_ref, zo_ref, bif_ref, gm_ref, h_ref, c_ref, n_ref, m_ref, *, bb):
    @pl.when(pl.program_id(1) == 0)
    def _():
        c_ref[...] = jnp.zeros(c_ref.shape, F32)
        n_ref[...] = jnp.zeros(n_ref.shape, F32)
        m_ref[...] = jnp.zeros(m_ref.shape, F32)

    L = CHUNK
    G = bb * HEADS
    row = lax.broadcasted_iota(jnp.int32, (L, L), 0)
    col = lax.broadcasted_iota(jnp.int32, (L, L), 1)
    causal = col <= row
    tri = jnp.where(causal, 1.0, 0.0).astype(BF16)
    lane = lax.broadcasted_iota(jnp.int32, (L, 128), 1)
    is_forget = (lane >= HEADS) & (lane < 2 * HEADS)
    gls = []
    for bi in range(bb):
        g = zif_ref[bi] + bif_ref[...]
        gls.append(jnp.where(is_forget, _log_sigmoid(g), g))
    gl_all = jnp.concatenate(gls, axis=1)
    hi = gl_all.astype(BF16)
    r1 = gl_all - hi.astype(F32)
    mid = r1.astype(BF16)
    lo = (r1 - mid.astype(F32)).astype(BF16)
    cs_all = _dot(tri, hi) + _dot(tri, mid) + _dot(tri, lo)
    ig_col, b_col, ig_row, b_row, qs, ks, vs, zos = [], [], [], [], [], [], [], []
    for bi in range(bb):
        gl = gls[bi]
        cs = cs_all[:, bi * 128:(bi + 1) * 128]
        gl_t = gl.T
        cs_t = cs.T
        for h in range(HEADS):
            hs = slice(h * HDIM, (h + 1) * HDIM)
            ig_col.append(gl[:, h:h + 1])
            b_col.append(cs[:, HEADS + h:HEADS + h + 1])
            ig_row.append(gl_t[h:h + 1, :])
            b_row.append(cs_t[HEADS + h:HEADS + h + 1, :])
            qs.append(qkv_ref[bi, :, hs])
            ks.append(qkv_ref[bi, :, MLSTM_WIDTH + h * HDIM:MLSTM_WIDTH + (h + 1) * HDIM])
            vs.append(qkv_ref[bi, :, 2 * MLSTM_WIDTH + h * HDIM:2 * MLSTM_WIDTH + (h + 1) * HDIM])
            zos.append(zo_ref[bi, :, hs])
    ig_col, b_col, ig_row, b_row = (jnp.stack(a) for a in (ig_col, b_col, ig_row, b_row))
    q, k, v, zo = (jnp.stack(a) for a in (qs, ks, vs, zos))
    gm = jnp.stack([gm_ref[:, h * HDIM:(h + 1) * HDIM] for h in range(HEADS)] * bb)
    c_old = c_ref[...].reshape(G, HDIM, HDIM)
    n_old = n_ref[...].reshape(G, 1, HDIM)
    m_prev = m_ref[...].reshape(G, 1, 1)

    dmat = jnp.where(causal[None], b_col - b_row + ig_row, -jnp.inf)
    inter = b_col + m_prev
    m_t = jnp.maximum(inter, jnp.max(dmat, axis=-1, keepdims=True))
    w_intra = jnp.exp(dmat - m_t)
    w_inter = jnp.exp(inter - m_t)
    s = _bdot(q, k, 2, 2) * w_intra
    num = _bdot(s.astype(BF16), v, 2, 1) + w_inter * _bdot(q, c_old.astype(BF16), 2, 2)
    qn = jnp.sum(q.astype(F32) * n_old, axis=-1, keepdims=True)
    den = jnp.sum(s, axis=-1, keepdims=True) + w_inter * qn
    hh = num / jnp.maximum(jnp.abs(den), jnp.exp(-m_t))
    hh = hh * lax.rsqrt(jnp.mean(hh * hh, axis=-1, keepdims=True) + RMS_EPS)
    hh = (hh * gm * _sigmoid(zo)).astype(h_ref.dtype)
    for bi in range(bb):
        for h in range(HEADS):
            h_ref[bi, :, h * HDIM:(h + 1) * HDIM] = hh[bi * HEADS + h]
    m_new = m_t[:, L - 1:L, :]
    b_last = b_col[:, L - 1:L, :]
    w_state = jnp.exp(b_last - b_col + ig_col - m_new)
    decay = jnp.exp(b_last + m_prev - m_new)
    vw = (v.astype(F32) * w_state).astype(BF16)
    c_new = decay * c_old + _bdot(vw, k, 1, 1)
    n_new = decay * n_old + jnp.sum(k.astype(F32) * w_state, axis=1, keepdims=True)
    c_ref[...] = c_new.reshape(c_ref.shape)
    n_ref[...] = n_new.reshape(n_ref.shape)
    m_ref[...] = m_new.reshape(m_ref.shape)


def _mlstm_chunks(qkv, zif, zo, bif, gm, l, bb):
    b, t, _ = qkv.shape
    blk = lambda w: pl.BlockSpec((bb, CHUNK, w), lambda i, j: (i, j, 0))
    state = lambda r, c: pl.BlockSpec((bb, HEADS, r, c), lambda i, j: (i, 0, 0, 0))
    return pl.pallas_call(
        functools.partial(_mlstm_chunk_body, bb=bb),
        grid=(b // bb, t // CHUNK),
        in_specs=[blk(3 * MLSTM_WIDTH), blk(128), blk(MLSTM_WIDTH), _pick((1, 128), l), _pick((1, MLSTM_WIDTH), l)],
        out_specs=[blk(MLSTM_WIDTH), state(HDIM, HDIM), state(1, HDIM), state(1, 1)],
        out_shape=[
            jax.ShapeDtypeStruct((b, t, MLSTM_WIDTH), BF16),
            jax.ShapeDtypeStruct((b, HEADS, HDIM, HDIM), F32),
            jax.ShapeDtypeStruct((b, HEADS, 1, HDIM), F32),
            jax.ShapeDtypeStruct((b, HEADS, 1, 1), F32),
        ],
        compiler_params=_params("arbitrary", "arbitrary"),
        name="mlstm_chunks",
    )(qkv, zif, zo, bif, gm)


def _mlstm_step_body(qkv_ref, zif_ref, zo_ref, bif_ref, gm_ref, c0_ref, n0_ref, m0_ref, h_ref, c_ref, n_ref, m_ref):
    g = zif_ref[...] + bif_ref[...]
    r = lax.broadcasted_iota(jnp.int32, (HDIM, HDIM), 0)
    c = lax.broadcasted_iota(jnp.int32, (HDIM, HDIM), 1)
    eye = (r == c)[None]
    for h in range(HEADS):
        hs = slice(h * HDIM, (h + 1) * HDIM)
        ig = g[:, :, h:h + 1]
        lf = _log_sigmoid(g[:, :, HEADS + h:HEADS + h + 1])
        m_prev = m0_ref[:, :, h:h + 1]
        q = qkv_ref[:, :, hs]
        k = qkv_ref[:, :, MLSTM_WIDTH + h * HDIM:MLSTM_WIDTH + (h + 1) * HDIM]
        v = qkv_ref[:, :, 2 * MLSTM_WIDTH + h * HDIM:2 * MLSTM_WIDTH + (h + 1) * HDIM]
        c_h = c0_ref[:, h]
        n_h = n0_ref[:, h:h + 1, :]
        inter = lf + m_prev
        m_t = jnp.maximum(inter, ig)
        w_intra = jnp.exp(ig - m_t)
        w_inter = jnp.exp(inter - m_t)
        s = jnp.sum(q * k, axis=-1, keepdims=True) * w_intra
        cq = jnp.sum(c_h * q, axis=-1, keepdims=True)
        v_col = jnp.sum(jnp.where(eye, v, 0.0), axis=-1, keepdims=True)
        num = s * v_col + w_inter * cq
        den = s + w_inter * jnp.sum(n_h * q, axis=-1, keepdims=True)
        hcol = num / jnp.maximum(jnp.abs(den), jnp.exp(-m_t))
        hcol = hcol * lax.rsqrt(jnp.mean(hcol * hcol, axis=1, keepdims=True) + RMS_EPS)
        hrow = jnp.sum(jnp.where(eye, hcol, 0.0), axis=1, keepdims=True)
        hrow = hrow * gm_ref[:, hs] * _sigmoid(zo_ref[:, :, hs])
        h_ref[:, :, hs] = hrow.astype(h_ref.dtype)
        w_state = jnp.exp(ig - m_t)
        decay = jnp.exp(lf + m_prev - m_t)
        c_ref[:, h] = decay * c_h + (w_state * v_col) * k
        n_ref[:, h:h + 1, :] = decay * n_h + w_state * k
        m_ref[:, :, h:h + 1] = m_t


def _layer_norm(v, g, b):
    vc = v - jnp.mean(v, axis=-1, keepdims=True)
    return vc * lax.rsqrt(jnp.mean(vc * vc, axis=-1, keepdims=True) + LN_EPS) * g + b


def _gmlp_step_body(ugv_ref, lng_ref, lnb_ref, w00_ref, b0_ref, o_ref, vn_ref):
    u = _gelu_tanh(ugv_ref[:, 0:GMLP_WIDTH])
    v = _gelu_tanh(ugv_ref[:, GMLP_WIDTH:])
    vn = _layer_norm(v, lng_ref[...], lnb_ref[...])
    vn_ref[...] = vn
    o_ref[...] = (u * (w00_ref[...] * vn + b0_ref[...])).astype(o_ref.dtype)


def _merge_body(x_ref, hp_ref, hm_ref, hg_ref, g_ref, wgt_ref, wbp_ref, wbm_ref, wbg_ref, wout_ref, o_ref):
    x = x_ref[...]
    xn = _rms(x, g_ref[0:1, :]).astype(BF16)
    y = None
    for j in range(D_MODEL // MERGE_CHUNK):
        cols = slice(j * MERGE_CHUNK, (j + 1) * MERGE_CHUNK)
        merged = None
        for i, (h_ref, w_ref) in enumerate(((hp_ref, wbp_ref), (hm_ref, wbm_ref), (hg_ref, wbg_ref))):
            gate_rows = slice(i * D_MODEL + j * MERGE_CHUNK, i * D_MODEL + (j + 1) * MERGE_CHUNK)
            gate = _sigmoid(_dot_nt(xn, wgt_ref[gate_rows, :]))
            term = gate * _dot(h_ref[...], w_ref[:, cols])
            merged = term if merged is None else merged + term
        part = _dot(merged.astype(BF16), wout_ref[cols, :])
        y = part if y is None else y + part
    o_ref[...] = x + _rms(y, g_ref[1:2, :])


def _merge_q_body(x_ref, hp_ref, hm_ref, hg_ref, g_ref, wgt_ref, wbp_ref, wbm_ref, wbg_ref, wout_ref,
                  ga_ref, wq_ref, o_ref, q_ref):
    _merge_body(x_ref, hp_ref, hm_ref, hg_ref, g_ref, wgt_ref, wbp_ref, wbm_ref, wbg_ref, wout_ref, o_ref)
    xn = _rms(o_ref[...], ga_ref[0:1, :]).astype(BF16)
    q = _dot(xn, wq_ref[...])
    for h in range(MEM_HEADS):
        for half in range(2):
            lo = h * MEM_HDIM + half * 128
            q_ref[:, 0, half * MEM_HEADS + h, :] = q[:, lo:lo + 128]


def _merge_weight_specs(l):
    return [_pick((2, D_MODEL), l, 1), _pick((3 * D_MODEL, D_MODEL), l), _pick((POOL_WIDTH, D_MODEL), l),
            _pick((MLSTM_WIDTH, D_MODEL), l), _pick((GMLP_WIDTH, D_MODEL), l), _pick((D_MODEL, D_MODEL), l)]


def _merge(x, hp, hm, hg, norm_g, p, l, tm):
    n = x.shape[0]
    row = lambda w: pl.BlockSpec((tm, w), lambda i: (i, 0))
    return pl.pallas_call(
        _merge_body, grid=(n // tm,),
        in_specs=[row(D_MODEL), row(POOL_WIDTH), row(MLSTM_WIDTH), row(GMLP_WIDTH)] + _merge_weight_specs(l),
        out_specs=row(D_MODEL), out_shape=jax.ShapeDtypeStruct((n, D_MODEL), F32),
        compiler_params=_params("arbitrary"), name="merge",
    )(x, hp, hm, hg, norm_g, p["wgt"], p["wbp"], p["wbm"], p["wbg"], p["wout"])


def _sample_mix_body(x_ref, hm_ref, zp_ref, buf_ref, wbd_ref, sp_ref, ugv_ref, lng_ref, lnb_ref, w00_ref, b0_ref,
                     g_ref, wgt_ref, wbp_ref, wbm_ref, wbg_ref, wout_ref, ga_ref, wq_ref,
                     o_ref, q_ref, nbuf_ref, vn_ref, hp_ref, hg_ref, hm2_ref):
    _pool_step_body(zp_ref, buf_ref, wbd_ref, sp_ref, hp_ref, nbuf_ref)
    _gmlp_step_body(ugv_ref, lng_ref, lnb_ref, w00_ref, b0_ref, hg_ref, vn_ref)
    hm2_ref[...] = hm_ref[:, 0, :]
    _merge_q_body(x_ref, hp_ref, hm2_ref, hg_ref, g_ref, wgt_ref, wbp_ref, wbm_ref, wbg_ref, wout_ref,
                  ga_ref, wq_ref, o_ref, q_ref)


def _sample_mix_call(x, hm, zp, ugv, pool_t, norm_g, p, l):
    b = x.shape[0]
    full = lambda *shape: pl.BlockSpec(shape, lambda i: (0,) * len(shape))
    vec = _pick((1, GMLP_WIDTH), l)
    return pl.pallas_call(
        _sample_mix_body,
        grid=(1,),
        in_specs=[full(b, D_MODEL), full(b, 1, MLSTM_WIDTH), full(b, POOL_WIDTH), _pick((POOL_BUF, b, POOL_WIDTH), l),
                  _pick((POOL_WIDTH, POOL_WIDTH), l), _pick((1, POOL_WIDTH), l), full(b, 2 * GMLP_WIDTH),
                  vec, vec, vec, vec] + _merge_weight_specs(l) + [_pick((2, D_MODEL), l, 2),
                                                                   _pick((D_MODEL, D_MODEL), l)],
        out_specs=[full(b, D_MODEL), full(b, 1, 2 * MEM_HEADS, 128), full(POOL_BUF, b, POOL_WIDTH),
                   full(b, GMLP_WIDTH)],
        out_shape=[jax.ShapeDtypeStruct((b, D_MODEL), F32), jax.ShapeDtypeStruct((b, 1, 2 * MEM_HEADS, 128), F32),
                   jax.ShapeDtypeStruct((POOL_BUF, b, POOL_WIDTH), F32), jax.ShapeDtypeStruct((b, GMLP_WIDTH), F32)],
        scratch_shapes=[pltpu.VMEM((b, POOL_WIDTH), BF16), pltpu.VMEM((b, GMLP_WIDTH), BF16),
                        pltpu.VMEM((b, MLSTM_WIDTH), BF16)],
        compiler_params=_params("arbitrary"),
        name="sample_mix",
    )(x, hm, zp, pool_t, p["wbd"], p["sp"], ugv, p["lng"], p["lnb"], p["w00"], p["b0"],
      norm_g, p["wgt"], p["wbp"], p["wbm"], p["wbg"], p["wout"], norm_g, p["wq"])


def _memkv_body(mem_ref, g_ref, wk_ref, wv_ref, k_ref, v_ref, kb_ref, vb_ref):
    mn = _rms(mem_ref[...], g_ref[...]).astype(BF16)
    k = _dot(mn, wk_ref[...])
    v = _dot(mn, wv_ref[...])
    kb_ref[...] = k.astype(BF16)
    vb_ref[...] = v.astype(BF16)
    for h in range(MEM_HEADS):
        for half in range(2):
            cols = slice(h * MEM_HDIM + half * 128, h * MEM_HDIM + (half + 1) * 128)
            k_ref[:, half * MEM_HEADS + h, :] = k[:, cols]
            v_ref[:, half * MEM_HEADS + h, :] = v[:, cols]


def _memkv(mem, g, wk, wv, tm):
    n = mem.shape[0]
    depth = wk.shape[0]
    per_layer = lambda *tail: pl.BlockSpec((None,) + tail, lambda l, j: (l,) + (0,) * len(tail))
    rows = lambda *tail: pl.BlockSpec((None, tm) + tail, lambda l, j: (l, j) + (0,) * len(tail))
    return pl.pallas_call(
        _memkv_body,
        grid=(depth, n // tm),
        in_specs=[pl.BlockSpec((tm, D_MODEL), lambda l, j: (j, 0)), per_layer(1, D_MODEL),
                  per_layer(D_MODEL, D_MODEL), per_layer(D_MODEL, D_MODEL)],
        out_specs=[rows(2 * MEM_HEADS, 128), rows(2 * MEM_HEADS, 128), rows(D_MODEL), rows(D_MODEL)],
        out_shape=[jax.ShapeDtypeStruct((depth, n, 2 * MEM_HEADS, 128), F32),
                   jax.ShapeDtypeStruct((depth, n, 2 * MEM_HEADS, 128), F32),
                   jax.ShapeDtypeStruct((depth, n, D_MODEL), BF16), jax.ShapeDtypeStruct((depth, n, D_MODEL), BF16)],
        compiler_params=_params("arbitrary", "arbitrary"),
        name="memkv",
    )(mem, g, wk, wv)


def _attn_prompt_body(x_ref, k_ref, v_ref, g_ref, wq_ref, wo_ref, o_ref):
    x = x_ref[0]
    xn = _rms(x, g_ref[0:1, :]).astype(BF16)
    q = _dot(xn, wq_ref[...]).astype(BF16)
    outs = []
    for h in range(MEM_HEADS):
        hs = slice(h * MEM_HDIM, (h + 1) * MEM_HDIM)
        s = _dot_nt(q[:, hs], k_ref[:, hs]) * (MEM_HDIM ** -0.5)
        e = jnp.exp(s - jnp.max(s, axis=-1, keepdims=True))
        p = e / jnp.sum(e, axis=-1, keepdims=True)
        outs.append(_dot(p.astype(BF16), v_ref[:, hs]).astype(BF16))
    o = jnp.concatenate(outs, axis=-1)
    y = _dot(o, wo_ref[...])
    o_ref[0] = x + _rms(y, g_ref[1:2, :])


def _attn_prompt(x, kb, vb, norm_g, wq, wo, l, tq):
    b, t, _ = x.shape
    xs = pl.BlockSpec((1, tq, D_MODEL), lambda i, j: (i, j, 0))
    ms = pl.BlockSpec((None, MEM_TOKENS, D_MODEL), lambda i, j: (l, i, 0))
    return pl.pallas_call(
        _attn_prompt_body,
        grid=(b, t // tq),
        in_specs=[xs, ms, ms, _pick((2, D_MODEL), l, 2), _pick((D_MODEL, D_MODEL), l), _pick((D_MODEL, D_MODEL), l)],
        out_specs=xs,
        out_shape=jax.ShapeDtypeStruct((b, t, D_MODEL), F32),
        compiler_params=_params("arbitrary", "arbitrary"),
        name="attn_prompt",
    )(x, kb, vb, norm_g, wq, wo)


def _attn_step_body(q_ref, k_ref, v_ref, o_ref):
    q = q_ref[...] * (MEM_HDIM ** -0.5)
    part = jnp.sum(k_ref[...] * q, axis=-1, keepdims=True)
    s = part + pltpu.roll(part, MEM_HEADS, 2)
    e = jnp.exp(s - jnp.max(s, axis=1, keepdims=True))
    o_ref[...] = jnp.sum(e * v_ref[...], axis=1, keepdims=True) / jnp.sum(e, axis=1, keepdims=True)


def _ffn_attn_body(x_ref, g_ref, wi_ref, wo_ref, q_ref, k_ref, v_ref, o_ref, a_ref, acc_ref):
    o_ref[...] = _swiglu_half_step(x_ref[...], g_ref, wi_ref, wo_ref, acc_ref)
    _attn_step_body(q_ref, k_ref, v_ref, a_ref)


def _ffn_with_attn(x, norm_g, w_in, w_out, l, sub, half, q, k_all, v_all):
    n = x.shape[0]
    b = q.shape[0]
    steps = 32
    tm, bb = n // steps, b // steps
    qs = pl.BlockSpec((bb, 1, 2 * MEM_HEADS, 128), lambda i: (i, 0, 0, 0))
    ms = pl.BlockSpec((None, bb, MEM_TOKENS, 2 * MEM_HEADS, 128), lambda i: (l, i, 0, 0, 0))
    return pl.pallas_call(
        _ffn_attn_body,
        grid=(steps,),
        in_specs=[pl.BlockSpec((tm, D_MODEL), lambda i: (i, 0)), _pick((2, D_MODEL), l, sub),
                  _pick((D_MODEL, 2 * D_FF), l, half), _pick((D_FF, D_MODEL), l, half), qs, ms, ms],
        out_specs=[pl.BlockSpec((tm, D_MODEL), lambda i: (i, 0)), qs],
        out_shape=[jax.ShapeDtypeStruct((n, D_MODEL), F32),
                   jax.ShapeDtypeStruct((b, 1, 2 * MEM_HEADS, 128), F32)],
        scratch_shapes=[pltpu.VMEM((tm, D_MODEL), F32)],
        compiler_params=_params("arbitrary"),
        name="ffn_attn",
    )(x, norm_g, w_in, w_out, q, k_all, v_all)


def _halves_major(a):
    lead = a.shape[:-2]
    return jnp.swapaxes(a.reshape(*lead, MEM_HEADS, 2, 128), -3, -2).reshape(*lead, 2 * MEM_HEADS, 128)


def _heads_major(a):
    lead = a.shape[:-2]
    return jnp.swapaxes(a.reshape(*lead, 2, MEM_HEADS, 128), -3, -2).reshape(*lead, D_MODEL)


def _prep_weights(w_ff_in, w_ff_out, w_in, b_igate, b_fgate, w_pool, s_pool, g_mlstm, gmlp_ln_g, gmlp_ln_b,
                  w_s, b_s, w_br_pool, w_br_mlstm, w_br_gmlp, w_out, g_mem, w_mq, w_mk, w_mv, w_mo):
    p = {}
    p["ffn_in"] = w_ff_in.astype(BF16)
    p["ffn_out"] = w_ff_out.astype(BF16)
    wt = jnp.swapaxes(w_in, 1, 2)
    w_if = jnp.pad(wt[:, OFF_IF:OFF_U], ((0, 0), (0, 128 - 2 * HEADS), (0, 0)))
    p["w1t"] = jnp.concatenate([wt[:, :OFF_IF], wt[:, OFF_U:OFF_GATE], w_if], axis=1).astype(BF16)
    p["wgt"] = wt[:, OFF_GATE:].astype(BF16)
    bif = jnp.concatenate([b_igate, b_fgate], axis=1)
    p["bif"] = jnp.pad(bif, ((0, 0), (0, 128 - 2 * HEADS))).reshape(DEPTH, 1, 128)
    same_group = jnp.eye(len(POOL_WINDOWS), dtype=bool)[None, :, None, :, None]
    wbd = jnp.where(same_group, w_pool[:, :, :, None, :], 0.0)
    p["wbd"] = wbd.reshape(DEPTH, POOL_WIDTH, POOL_WIDTH).astype(BF16)
    p["sp"] = s_pool.reshape(DEPTH, 1, POOL_WIDTH)
    p["gm"] = g_mlstm.reshape(DEPTH, 1, MLSTM_WIDTH)
    p["lng"] = gmlp_ln_g.reshape(DEPTH, 1, GMLP_WIDTH)
    p["lnb"] = gmlp_ln_b.reshape(DEPTH, 1, GMLP_WIDTH)
    p["ws"] = w_s.astype(BF16)
    p["bst"] = jnp.swapaxes(b_s, 1, 2)
    p["w00"] = jnp.repeat(w_s[:, :, 0, 0], GMLP_GDIM, axis=1).reshape(DEPTH, 1, GMLP_WIDTH)
    p["b0"] = jnp.repeat(b_s[:, :, 0], GMLP_GDIM, axis=1).reshape(DEPTH, 1, GMLP_WIDTH)
    p["wbp"] = w_br_pool.astype(BF16)
    p["wbm"] = w_br_mlstm.astype(BF16)
    p["wbg"] = w_br_gmlp.astype(BF16)
    p["wout"] = w_out.astype(BF16)
    p["gmem"] = g_mem.reshape(DEPTH, 1, D_MODEL)
    p["wq"] = w_mq.astype(BF16)
    p["wk"] = w_mk.astype(BF16)
    p["wv"] = w_mv.astype(BF16)
    p["wo"] = w_mo.astype(BF16)
    return p


def _ffn_mlstm_body(x_ref, g_ref, wi_ref, wo_ref, *rest, first):
    outs = rest[8:] if first else rest[9:]
    o_ref, h_ref, c_ref, n_ref, m_ref, acc_ref = outs
    if first:
        c_ref[1:] = jnp.zeros((c_ref.shape[0] - 1,) + c_ref.shape[1:], F32)
        c_ref = c_ref.at[0]
    _mlstm_step_body(*rest[0:8], h_ref, c_ref, n_ref, m_ref)
    o_ref[...] = _swiglu_half_step(x_ref[...], g_ref, wi_ref, wo_ref, acc_ref)


def _ffn_with_mlstm(x, norm_g, w_in, w_out, l, sub, half, qkv, zif, zo, bif, gm, c_all, n_all, m_all, c_prev):
    n = x.shape[0]
    b = qkv.shape[0]
    depth = c_all.shape[0]
    first = c_prev is None
    steps = 32
    tm, bb = n // steps, b // steps
    blk = lambda w: pl.BlockSpec((bb, 1, w), lambda i: (i, 0, 0))
    st_c = pl.BlockSpec((None, bb, HEADS, HDIM, HDIM), lambda i: (l, i, 0, 0, 0))
    st_n = pl.BlockSpec((None, bb, HEADS, HDIM), lambda i: (l, i, 0, 0))
    st_m = pl.BlockSpec((None, bb, 1, HEADS), lambda i: (l, i, 0, 0))
    row = pl.BlockSpec((tm, D_MODEL), lambda i: (i, 0))
    in_specs = [row, _pick((2, D_MODEL), l, sub), _pick((D_MODEL, 2 * D_FF), l, half), _pick((D_FF, D_MODEL), l, half),
                blk(3 * MLSTM_WIDTH), blk(128), blk(MLSTM_WIDTH), _pick((1, 128), l), _pick((1, MLSTM_WIDTH), l),
                st_c, st_n, st_m]
    args = [x, norm_g, w_in, w_out, qkv, zif, zo, bif, gm, c_all, n_all, m_all]
    if first:
        assert l == 0
        out_c = pl.BlockSpec((depth, bb, HEADS, HDIM, HDIM), lambda i: (0, i, 0, 0, 0))
        aliases = {}
    else:
        in_specs.append(pl.BlockSpec(memory_space=pl.ANY))
        args.append(c_prev)
        out_c = st_c
        aliases = {12: 2}
    return pl.pallas_call(
        functools.partial(_ffn_mlstm_body, first=first),
        grid=(steps,),
        in_specs=in_specs,
        out_specs=[row, blk(MLSTM_WIDTH), out_c, pl.BlockSpec((bb, HEADS, HDIM), lambda i: (i, 0, 0)),
                   pl.BlockSpec((bb, 1, HEADS), lambda i: (i, 0, 0))],
        out_shape=[
            jax.ShapeDtypeStruct((n, D_MODEL), F32),
            jax.ShapeDtypeStruct((b, 1, MLSTM_WIDTH), BF16),
            jax.ShapeDtypeStruct(c_all.shape, F32),
            jax.ShapeDtypeStruct((b, HEADS, HDIM), F32),
            jax.ShapeDtypeStruct((b, 1, HEADS), F32),
        ],
        input_output_aliases=aliases,
        scratch_shapes=[pltpu.VMEM((tm, D_MODEL), F32)],
        compiler_params=_params("arbitrary"),
        name="ffn_mlstm",
    )(*args)


def _prompt_layer(x, kb, vb, norm_g, p, l, batch, seq):
    tm = 1024
    qkv, zo, zif, hp, hg, tail = _inproj_mix(x, norm_g, p, l, tm, seq)
    b3 = lambda a: a.reshape(batch, seq, a.shape[-1])
    hm, c_new, n_new, m_new = _mlstm_chunks(b3(qkv), b3(zif), b3(zo), p["bif"], p["gm"], l, 8)
    x = _merge(x, hp, hm.reshape(batch * seq, MLSTM_WIDTH), hg, norm_g, p, l, tm)
    x = _attn_prompt(b3(x), kb, vb, norm_g, p["wq"], p["wo"], l, 1024).reshape(batch * seq, D_MODEL)
    return x, tail[:, 16 - POOL_BUF:, :], c_new, n_new.reshape(batch, HEADS, HDIM), m_new.reshape(batch, HEADS)


def _sample_mix(x, zp, ugv, hm, norm_g, p, l, pool_t):
    batch = x.shape[0]
    x, q8, new_buf, vn = _sample_mix_call(x, hm, zp, ugv, pool_t, norm_g, p, l)
    return x, q8, new_buf, vn.reshape(batch, 1, GMLP_WIDTH)


def kernel(x_prompt, x_sample, state_pool, state_mlstm_C, state_mlstm_n, state_mlstm_m, cache_mem_k, cache_mem_v, mem_prompt, norm_g, w_ff_in, w_ff_out, w_in, b_igate, b_fgate, w_pool, s_pool, g_mlstm, gmlp_ln_g, gmlp_ln_b, w_s, b_s, w_br_pool, w_br_mlstm, w_br_gmlp, w_out, g_mem, w_mq, w_mk, w_mv, w_mo):
    batch, seq, _ = x_prompt.shape
    dec_batch = x_sample.shape[0]
    assert x_sample.shape[1] == 1 and seq % CHUNK == 0 and PAST_LEN % CHUNK == 0 and PAST_LEN >= max(POOL_WINDOWS)
    p = _prep_weights(w_ff_in, w_ff_out, w_in, b_igate, b_fgate, w_pool, s_pool, g_mlstm, gmlp_ln_g, gmlp_ln_b,
                      w_s, b_s, w_br_pool, w_br_mlstm, w_br_gmlp, w_out, g_mem, w_mq, w_mk, w_mv, w_mo)
    xp = x_prompt.reshape(batch * seq, D_MODEL)
    xs = x_sample.reshape(dec_batch, D_MODEL)
    k32, v32, kb, vb = _memkv(mem_prompt.reshape(batch * MEM_TOKENS, D_MODEL), p["gmem"], p["wk"], p["wv"], 512)
    mk = _heads_major(k32).reshape(DEPTH, batch, MEM_TOKENS, MEM_HEADS, MEM_HDIM)
    mv = _heads_major(v32).reshape(DEPTH, batch, MEM_TOKENS, MEM_HEADS, MEM_HDIM)
    pool_t = jnp.swapaxes(state_pool, 1, 2)
    m_all = state_mlstm_m.reshape(DEPTH, dec_batch, 1, HEADS)
    k_dense = _halves_major(cache_mem_k)
    v_dense = _halves_major(cache_mem_v)
    outs = [[] for _ in range(8)]
    c_samp = None
    for l in range(DEPTH):
        xs, zp, qkv, zo, ugv, zif = _ffn_inproj(xs, norm_g, p["ffn_in"], p["ffn_out"], p["w1t"], l)
        xp, hm, c_samp, ns, ms = _ffn_with_mlstm(xp, norm_g, p["ffn_in"], p["ffn_out"], l, 0, 0, qkv, zif, zo,
                                                 p["bif"], p["gm"], state_mlstm_C, state_mlstm_n, m_all, c_samp)
        ms = ms.reshape(dec_batch, HEADS)
        xp, pb, c, n, m = _prompt_layer(xp, kb, vb, norm_g, p, l, batch, seq)
        xs, q8, pbs, vr = _sample_mix(xs, zp, ugv, hm, norm_g, p, l, pool_t)
        xp, a8 = _ffn_with_attn(xp, norm_g, p["ffn_in"], p["ffn_out"], l, 3, 1, q8, k_dense, v_dense)
        xs = _oproj_ffn(xs, a8, norm_g, p["wo"], p["ffn_in"], p["ffn_out"], l)
        for lst, val in zip(outs, (pb, pbs, c, n, m, ns, ms, vr)):
            lst.append(val)
    pb, pbs, c, n, m, ns, ms, vr = [jnp.stack(o) for o in outs]
    return (xp.reshape(batch, seq, D_MODEL), xs.reshape(dec_batch, 1, D_MODEL),
            pb, jnp.swapaxes(pbs, 1, 2), c, n, m, c_samp, ns, ms, mk, mv, vr)
```
